```python
import math
import jax, jax.numpy as jnp
from jax import lax
import numpy as np

D_MODEL = 2048
BATCH = 4
SEQ = 2048
DEPTH = 4

DA_HEADS = 4
DA_HD = 64
DA_W = DA_HEADS * 2 * DA_HD
SW_HEADS = 8
SW_KV = 2
SW_HD = 64
WINDOW = 128
SW_QW = SW_HEADS * SW_HD
SW_KW = SW_KV * SW_HD
MB_HEADS = 8
MB_HD = 64
MB_W = MB_HEADS * MB_HD
MB_BLOCK = 256
MB_TOPK = 3
MB_CHUNK = 32
Q_BLOCK = 128
N_BUCKETS = 32
MAX_DIST = 128
N_ATT_HEADS = DA_HEADS + SW_HEADS + MB_HEADS
N_BRANCH = 3
IN_W = 3 * DA_W + SW_QW + 2 * SW_KW + 3 * MB_W + N_BRANCH * D_MODEL
D_FF = 5632
CONV_W = 3
EPS = 1e-6

kernel_name = "hybrid_gated_diff_swa_moba_convffn"


def rmsnorm(x, g):
    xf = x.astype(jnp.float32)
    y = xf * lax.rsqrt(jnp.mean(xf * xf, axis=-1, keepdims=True) + EPS)
    return (y * g.astype(jnp.float32)).astype(x.dtype)


def rel_bucket(dist):
    n = jnp.maximum(dist, 0)
    max_exact = N_BUCKETS // 2
    nf = jnp.maximum(n, 1).astype(jnp.float32)
    large = max_exact + (jnp.log(nf / max_exact) / math.log(MAX_DIST / max_exact)
                         * (N_BUCKETS - max_exact)).astype(jnp.int32)
    large = jnp.minimum(large, N_BUCKETS - 1)
    return jnp.where(n < max_exact, n, large)


def diff_attention(q, k, v, lam, lam_init, subln_g, bias_t):
    B, S = q.shape[0], q.shape[1]
    nb = S // Q_BLOCK
    scale = DA_HD ** -0.5
    qb = jnp.moveaxis(q.reshape(B, nb, Q_BLOCK, DA_HEADS, 2, DA_HD), 1, 0)
    kpos = jnp.arange(S)

    def block(args):
        qi, i = args
        qpos = i * Q_BLOCK + jnp.arange(Q_BLOCK)
        dist = qpos[:, None] - kpos[None, :]
        bias = jnp.take(bias_t, rel_bucket(dist), axis=1)
        logits = jnp.einsum('bqhmd,bkhmd->bhmqk', qi, k).astype(jnp.float32) * scale + bias[None, :, None]
        logits = jnp.where(dist >= 0, logits, -jnp.inf)
        p = jax.nn.softmax(logits, axis=-1)
        attn = p[:, :, 0] - lam * p[:, :, 1]
        return jnp.einsum('bhqk,bkhe->bqhe', attn.astype(v.dtype), v)

    o = lax.map(block, (qb, jnp.arange(nb)))
    o = jnp.moveaxis(o, 0, 1).reshape(B, S, DA_HEADS, 2 * DA_HD)
    o = rmsnorm(o, subln_g) * (1.0 - lam_init)
    return o.reshape(B, S, DA_W)


def sliding_window_attention(q, k, v, sinks, bias_t):
    B, S = q.shape[0], q.shape[1]
    nb = S // Q_BLOCK
    G = SW_HEADS // SW_KV
    scale = SW_HD ** -0.5
    qb = q.reshape(B, nb, Q_BLOCK, SW_KV, G, SW_HD)
    kb = k.reshape(B, nb, Q_BLOCK, SW_KV, SW_HD)
    vb = v.reshape(B, nb, Q_BLOCK, SW_KV, SW_HD)
    shift = lambda t: jnp.concatenate([jnp.zeros_like(t[:, :1]), t[:, :-1]], axis=1)
    kc = jnp.concatenate([shift(kb), kb], axis=2)
    vc = jnp.concatenate([shift(vb), vb], axis=2)
    blk = jnp.arange(nb)[:, None] * Q_BLOCK
    qpos = blk + jnp.arange(Q_BLOCK)[None]
    kpos = blk - Q_BLOCK + jnp.arange(2 * Q_BLOCK)[None]
    dist = qpos[:, :, None] - kpos[:, None, :]
    valid = (dist >= 0) & (dist < WINDOW) & (kpos[:, None, :] >= 0)
    bias = jnp.take(bias_t, rel_bucket(dist), axis=1)
    bias = jnp.moveaxis(bias.reshape(SW_KV, G, nb, Q_BLOCK, 2 * Q_BLOCK), 2, 0)
    logits = jnp.einsum('bnqkgd,bnskd->bnkgqs', qb, kc).astype(jnp.float32) * scale + bias[None]
    logits = jnp.where(valid[None, :, None, None], logits, -jnp.inf)
    sink = jnp.broadcast_to(sinks.astype(jnp.float32).reshape(1, 1, SW_KV, G, 1, 1),
                            logits.shape[:-1] + (1,))
    p = jax.nn.softmax(jnp.concatenate([logits, sink], axis=-1), axis=-1)[..., :-1]
    o = jnp.einsum('bnkgqs,bnskd->bnqkgd', p.astype(vc.dtype), vc)
    return o.reshape(B, S, SW_QW)


def moba_attention(q, k, v, bias_t):
    B, S = q.shape[0], q.shape[1]
    nblk = -(-S // MB_BLOCK)
    Sp = nblk * MB_BLOCK
    padw = ((0, 0), (0, Sp - S), (0, 0), (0, 0))
    q, k, v = [jnp.pad(t, padw).transpose(0, 2, 1, 3) for t in (q, k, v)]
    scale = MB_HD ** -0.5
    kblk = k.reshape(B, MB_HEADS, nblk, MB_BLOCK, MB_HD)
    vblk = v.reshape(B, MB_HEADS, nblk, MB_BLOCK, MB_HD)
    kmean = jnp.mean(kblk.astype(jnp.float32), axis=3)
    own = jnp.arange(Sp) // MB_BLOCK
    gate = jnp.einsum('bhtd,bhnd->bhtn', q.astype(jnp.float32), kmean)
    past = jnp.arange(nblk)[None, :] < own[:, None]
    gate = jnp.where(past, gate, -jnp.inf)
    topk = min(MB_TOPK, nblk)
    _, idx = lax.top_k(gate, topk)

    nch = Sp // MB_CHUNK
    per_blk = MB_BLOCK // MB_CHUNK
    qc = q.reshape(B, MB_HEADS, nch, MB_CHUNK, MB_HD).transpose(2, 0, 1, 3, 4)
    idxc = idx.reshape(B, MB_HEADS, nch, MB_CHUNK, topk).transpose(2, 0, 1, 3, 4)
    bi = jnp.arange(B)[:, None, None, None]
    hi = jnp.arange(MB_HEADS)[None, :, None, None]
    hi5 = hi[..., None]
    s_off = jnp.arange(MB_BLOCK)

    def chunk(args):
        qi, ii, c = args
        qpos = c * MB_CHUNK + jnp.arange(MB_CHUNK)
        ob = c // per_blk
        ks = kblk[bi, hi, ii]
        vs = vblk[bi, hi, ii]
        dist_s = qpos[None, None, :, None, None] - (ii[..., None] * MB_BLOCK + s_off)
        ls = (jnp.einsum('bhcd,bhcjsd->bhcjs', qi, ks).astype(jnp.float32) * scale
              + bias_t[hi5, rel_bucket(dist_s)])
        ls = jnp.where((jnp.arange(topk) < ob)[:, None], ls, -jnp.inf)
        ls = ls.reshape(B, MB_HEADS, MB_CHUNK, topk * MB_BLOCK)
        ko = lax.dynamic_slice_in_dim(k, ob * MB_BLOCK, MB_BLOCK, axis=2)
        vo = lax.dynamic_slice_in_dim(v, ob * MB_BLOCK, MB_BLOCK, axis=2)
        dist_o = qpos[:, None] - (ob * MB_BLOCK + s_off)[None, :]
        lo = (jnp.einsum('bhcd,bhsd->bhcs', qi, ko).astype(jnp.float32) * scale
              + jnp.take(bias_t, rel_bucket(dist_o), axis=1)[None])
        lo = jnp.where(dist_o >= 0, lo, -jnp.inf)
        p = jax.nn.softmax(jnp.concatenate([ls, lo], axis=-1), axis=-1).astype(vs.dtype)
        ps = p[..., :topk * MB_BLOCK].reshape(B, MB_HEADS, MB_CHUNK, topk, MB_BLOCK)
        po = p[..., topk * MB_BLOCK:]
        return (jnp.einsum('bhcjs,bhcjsd->bhcd', ps, vs)
                + jnp.einsum('bhcs,bhsd->bhcd', po, vo))

    o = lax.map(chunk, (qc, idxc, jnp.arange(nch)))
    o = o.transpose(1, 0, 3, 2, 4).reshape(B, Sp, MB_W)
    return o[:, :S]


def conv_ffn(h, w_up, conv_w, conv_b, w_down):
    S = h.shape[1]
    u = h @ w_up
    up = jnp.pad(u, ((0, 0), (CONV_W - 1, 0), (0, 0)))
    c = conv_b
    for i in range(CONV_W):
        c = c + up[:, i:i + S] * conv_w[i]
    gate, val = jnp.split(c, 2, axis=-1)
    return (jax.nn.gelu(gate, approximate=True) * val) @ w_down


def setup_inputs(seed: int = 0) -> dict:
    key = jax.random.key(seed)
    ks = jax.random.split(key, 24)
    f32 = jnp.float32
    nrm = lambda k, shape, s: jax.random.normal(k, shape, f32) * s
    gain = lambda k: 1.0 + nrm(k, (DEPTH, D_MODEL), 0.02)
    return {
        "x": nrm(ks[0], (BATCH, SEQ, D_MODEL), 1.0),
        "rel_bias_table": nrm(ks[1], (N_BUCKETS, N_ATT_HEADS), 0.5),
        "w_in": nrm(ks[2], (DEPTH, D_MODEL, IN_W), D_MODEL ** -0.5),
        "b_gate": nrm(ks[3], (DEPTH, N_BRANCH * D_MODEL), 0.1),
        "lam_q1": nrm(ks[4], (DEPTH, DA_HD), 0.1),
        "lam_k1": nrm(ks[5], (DEPTH, DA_HD), 0.1),
        "lam_q2": nrm(ks[6], (DEPTH, DA_HD), 0.1),
        "lam_k2": nrm(ks[7], (DEPTH, DA_HD), 0.1),
        "diff_subln_g": 1.0 + nrm(ks[8], (DEPTH, 2 * DA_HD), 0.02),
        "sinks": nrm(ks[9], (DEPTH, SW_HEADS), 0.5),
        "w_oa": nrm(ks[10], (DEPTH, DA_W, D_MODEL), DA_W ** -0.5),
        "w_ob": nrm(ks[11], (DEPTH, SW_QW, D_MODEL), SW_QW ** -0.5),
        "w_oc": nrm(ks[12], (DEPTH, MB_W, D_MODEL), MB_W ** -0.5),
        "w_out": nrm(ks[13], (DEPTH, D_MODEL, D_MODEL), D_MODEL ** -0.5),
        "pre_mix_g": gain(ks[14]),
        "post_mix_g": gain(ks[15]),
        "pre_ffn_g": gain(ks[16]),
        "post_ffn_g": gain(ks[17]),
        "w_up": nrm(ks[18], (DEPTH, D_MODEL, 2 * D_FF), D_MODEL ** -0.5),
        "conv_w": nrm(ks[19], (DEPTH, CONV_W, 2 * D_FF), CONV_W ** -0.5),
        "conv_b": nrm(ks[20], (DEPTH, 2 * D_FF), 0.02),
        "w_down": nrm(ks[21], (DEPTH, D_FF, D_MODEL), D_FF ** -0.5),
    }


def reference(x, rel_bias_table, w_in, b_gate, lam_q1, lam_k1, lam_q2, lam_k2, diff_subln_g,
              sinks, w_oa, w_ob, w_oc, w_out, pre_mix_g, post_mix_g, pre_ffn_g, post_ffn_g,
              w_up, conv_w, conv_b, w_down):
    B, S = x.shape[0], x.shape[1]
    tab_t = rel_bias_table.T
    tab_a = tab_t[:DA_HEADS]
    tab_b = tab_t[DA_HEADS:DA_HEADS + SW_HEADS]
    tab_c = tab_t[DA_HEADS + SW_HEADS:]
    cuts = list(np.cumsum([DA_W, DA_W, DA_W, SW_QW, SW_KW, SW_KW, MB_W, MB_W, MB_W]))
    for l in range(DEPTH):
        h = rmsnorm(x, pre_mix_g[l])
        proj = h @ w_in[l]
        qa, ka, va, qb, kb, vb, qc, kc, vc, g = jnp.split(proj, cuts, axis=-1)
        gates = jax.nn.sigmoid((g + b_gate[l]).astype(jnp.float32)).astype(x.dtype)
        ga, gb, gc = jnp.split(gates, N_BRANCH, axis=-1)

        lam_init = 0.8 - 0.6 * math.exp(-0.3 * l)
        lam = (jnp.exp(jnp.sum(lam_q1[l].astype(jnp.float32) * lam_k1[l].astype(jnp.float32)))
               - jnp.exp(jnp.sum(lam_q2[l].astype(jnp.float32) * lam_k2[l].astype(jnp.float32)))
               + lam_init)
        ya = diff_attention(qa.reshape(B, S, DA_HEADS, 2, DA_HD), ka.reshape(B, S, DA_HEADS, 2, DA_HD),
                            va.reshape(B, S, DA_HEADS, 2 * DA_HD), lam, lam_init, diff_subln_g[l], tab_a)
        yb = sliding_window_attention(qb.reshape(B, S, SW_HEADS, SW_HD), kb.reshape(B, S, SW_KV, SW_HD),
                                      vb.reshape(B, S, SW_KV, SW_HD), sinks[l], tab_b)
        yc = moba_attention(qc.reshape(B, S, MB_HEADS, MB_HD), kc.reshape(B, S, MB_HEADS, MB_HD),
                            vc.reshape(B, S, MB_HEADS, MB_HD), tab_c)
        mix = ga * (ya @ w_oa[l]) + gb * (yb @ w_ob[l]) + gc * (yc @ w_oc[l])
        x = x + rmsnorm(mix @ w_out[l], post_mix_g[l])
        h = rmsnorm(x, pre_ffn_g[l])
        x = x + rmsnorm(conv_ffn(h, w_up[l], conv_w[l], conv_b[l], w_down[l]), post_ffn_g[l])
    return x
```

```python
import functools
import math

import jax
import jax.numpy as jnp
from jax import lax
from jax.experimental import pallas as pl
from jax.experimental.pallas import tpu as pltpu

DA_HEADS = 4
DA_HD = 64
DA_W = DA_HEADS * 2 * DA_HD
SW_HEADS = 8
SW_KV = 2
SW_HD = 64
WINDOW = 128
SW_QW = SW_HEADS * SW_HD
SW_KW = SW_KV * SW_HD
MB_HEADS = 8
MB_HD = 64
MB_W = MB_HEADS * MB_HD
MB_BLOCK = 256
MB_TOPK = 3
N_BUCKETS = 32
MAX_DIST = 128
N_ATT_HEADS = DA_HEADS + SW_HEADS + MB_HEADS
N_BRANCH = 3
QKV_W = 3 * DA_W + SW_QW + 2 * SW_KW + 3 * MB_W
EPS = 1e-6

LANES = 128
ATT_T = 256
SW_T = WINDOW
NEG = -1e30
VMEM_LIMIT = 56 * 1024 * 1024

_QA, _KA, _VA = 0, DA_W // LANES, 2 * DA_W // LANES
_QB = 3 * DA_W
_KB = (_QB + SW_QW) // LANES
_VB = _KB + 1
_QC = (_QB + SW_QW + 2 * SW_KW) // LANES
_KC = _QC + MB_W // LANES
_VC = _KC + MB_W // LANES
_SW_PERM = tuple(h // 2 + (h % 2) * (SW_HEADS // SW_KV) for h in range(SW_HEADS))

F32 = jnp.float32
BF16 = jnp.bfloat16


def _pick(n, candidates):
    for c in candidates:
        if n % c == 0:
            return c
    raise ValueError(f"no tile in {candidates} divides {n}")


def _params(*sem):
    return pltpu.CompilerParams(dimension_semantics=sem, vmem_limit_bytes=VMEM_LIMIT)


def _rms(x, g):
    return x * lax.rsqrt(jnp.mean(x * x, axis=-1, keepdims=True) + EPS) * g


def _dot_t(a, b, **kw):
    return lax.dot_general(a, b, (((1,), (1,)), ((), ())), preferred_element_type=F32, **kw)


def _rel_bucket(dist):
    n = jnp.maximum(dist, 0)
    max_exact = N_BUCKETS // 2
    nf = jnp.maximum(n, 1).astype(F32)
    large = max_exact + (jnp.log(nf / max_exact) / math.log(MAX_DIST / max_exact)
                         * (N_BUCKETS - max_exact)).astype(jnp.int32)
    large = jnp.minimum(large, N_BUCKETS - 1)
    return jnp.where(n < max_exact, n, large)


def _bias_kernel(tab_ref, diag_ref, near_ref, win_ref):
    h = pl.program_id(0)

    def lookup(dist):
        bucket = _rel_bucket(dist)
        acc = jnp.zeros(dist.shape, F32)
        for b in range(N_BUCKETS):
            acc = jnp.where(bucket == b, tab_ref[b * N_ATT_HEADS + h], acc)
        return acc

    d = (lax.broadcasted_iota(jnp.int32, (ATT_T, ATT_T), 0)
         - lax.broadcasted_iota(jnp.int32, (ATT_T, ATT_T), 1))
    diag_ref[0] = jnp.where(d >= 0, lookup(d), NEG)
    near_ref[0] = lookup(d + ATT_T)
    dw = (lax.broadcasted_iota(jnp.int32, (SW_T, 2 * SW_T), 0) + SW_T
          - lax.broadcasted_iota(jnp.int32, (SW_T, 2 * SW_T), 1))
    win_ref[0] = jnp.where((dw >= 0) & (dw < WINDOW), lookup(dw), NEG)


def _bias_tiles(tab_flat):
    return pl.pallas_call(
        _bias_kernel,
        grid=(N_ATT_HEADS,),
        in_specs=[pl.BlockSpec(memory_space=pltpu.SMEM)],
        out_specs=[pl.BlockSpec((1, ATT_T, ATT_T), lambda h: (h, 0, 0)),
                   pl.BlockSpec((1, ATT_T, ATT_T), lambda h: (h, 0, 0)),
                   pl.BlockSpec((1, SW_T, 2 * SW_T), lambda h: (h, 0, 0))],
        out_shape=[jax.ShapeDtypeStruct((N_ATT_HEADS, ATT_T, ATT_T), F32),
                   jax.ShapeDtypeStruct((N_ATT_HEADS, ATT_T, ATT_T), F32),
                   jax.ShapeDtypeStruct((N_ATT_HEADS, SW_T, 2 * SW_T), F32)],
        compiler_params=_params("arbitrary"),
        name="bias_tiles",
    )(tab_flat)


def _norm_kernel(x_ref, g_ref, o_ref):
    o_ref[...] = _rms(x_ref[...], g_ref[...]).astype(o_ref.dtype)


def _prenorm(x2, g):
    m, d = x2.shape
    tm = _pick(m, (512, 256, 128))
    return pl.pallas_call(
        _norm_kernel,
        grid=(m // tm,),
        in_specs=[pl.BlockSpec((tm, d), lambda i: (i, 0)), pl.BlockSpec((1, d), lambda i: (0, 0))],
        out_specs=pl.BlockSpec((tm, d), lambda i: (i, 0)),
        out_shape=jax.ShapeDtypeStruct((m, d), BF16),
        compiler_params=_params("arbitrary"),
        name="prenorm",
    )(x2, g.reshape(1, d))


def _proj_kernel(h_ref, w_ref, o_ref):
    o_ref[...] = jnp.dot(h_ref[...], w_ref[...], preferred_element_type=F32).astype(o_ref.dtype)


def _gate_kernel(h_ref, w_ref, b_ref, o_ref):
    acc = jnp.dot(h_ref[...], w_ref[...], preferred_element_type=F32)
    o_ref[...] = jax.nn.sigmoid(acc + b_ref[...]).astype(o_ref.dtype)


def _in_proj(h, w, bias=None):
    m, d = h.shape
    n = w.shape[1]
    tm = _pick(m, (1024, 512, 256, 128))
    tn = _pick(n, (1280, 1024, 768, 512, 256, 128))
    in_specs = [pl.BlockSpec((tm, d), lambda i, j: (i, 0)), pl.BlockSpec((d, tn), lambda i, j: (0, j))]
    args = [h, w]
    if bias is not None:
        in_specs.append(pl.BlockSpec((1, tn), lambda i, j: (0, j)))
        args.append(bias.reshape(1, n))
    return pl.pallas_call(
        _proj_kernel if bias is None else _gate_kernel,
        grid=(m // tm, n // tn),
        in_specs=in_specs,
        out_specs=pl.BlockSpec((tm, tn), lambda i, j: (i, j)),
        out_shape=jax.ShapeDtypeStruct((m, n), BF16),
        compiler_params=_params("arbitrary", "arbitrary"),
        name="in_proj" if bias is None else "in_gates",
    )(*args)


def _split_heads(q):
    lane = lax.broadcasted_iota(jnp.int32, q.shape, 1)
    zero = jnp.zeros_like(q)
    return jnp.concatenate([jnp.where(lane < LANES // 2, q, zero),
                            jnp.where(lane >= LANES // 2, q, zero)], axis=0)


def _merge_heads(o):
    t = o.shape[0] // 2
    lane = lax.broadcasted_iota(jnp.int32, (t, LANES), 1)
    return jnp.where(lane < LANES // 2, o[:t], o[t:])


def _softmax_first(s, v):
    m = jnp.max(s, axis=1, keepdims=True)
    p = jnp.exp(s - m)
    l = jnp.sum(p, axis=1, keepdims=True)
    acc = jnp.dot(p.astype(BF16), v, preferred_element_type=F32)
    return m, l, acc


def _softmax_next(s, v, carry):
    m, l, acc = carry
    m_new = jnp.maximum(m, jnp.max(s, axis=1, keepdims=True))
    alpha = jnp.exp(m - m_new)
    p = jnp.exp(s - m_new)
    l = alpha * l + jnp.sum(p, axis=1, keepdims=True)
    acc = alpha * acc + jnp.dot(p.astype(BF16), v, preferred_element_type=F32)
    return m_new, l, acc


def _kv_block(ref, kj):
    return ref[pl.ds(pl.multiple_of(kj * ATT_T, ATT_T), ATT_T), :]


def _diff_attn_kernel(tab_ref, lam_ref, g_ref, q_ref, k_ref, v_ref, diag_ref, near_ref, o_ref, *, lam_init):
    h = pl.program_id(1)
    qi = pl.program_id(2)
    t = ATT_T
    c_far = tab_ref[(N_BUCKETS - 1) * N_ATT_HEADS + h]
    q2 = _split_heads(q_ref[...] * (DA_HD ** -0.5))

    def logits(kj):
        return _dot_t(q2, _kv_block(k_ref, kj))

    b_diag = diag_ref[0]
    carry = _softmax_first(logits(qi) + jnp.concatenate([b_diag, b_diag], axis=0), _kv_block(v_ref, qi))

    def near(_, c):
        b_near = near_ref[0]
        return _softmax_next(logits(qi - 1) + jnp.concatenate([b_near, b_near], axis=0),
                             _kv_block(v_ref, qi - 1), c)

    carry = lax.fori_loop(0, jnp.minimum(qi, 1), near, carry)

    def far(kj, c):
        return _softmax_next(logits(kj) + c_far, _kv_block(v_ref, kj), c)

    _, l, acc = lax.fori_loop(0, jnp.maximum(qi - 1, 0), far, carry)

    o = acc / l
    lv = lam_ref[...]
    lam = (jnp.exp(jnp.sum(lv[0:1] * lv[1:2], axis=1, keepdims=True))
           - jnp.exp(jnp.sum(lv[2:3] * lv[3:4], axis=1, keepdims=True)) + lam_init)
    o = o[:t] - lam * o[t:]
    o_ref[...] = (_rms(o, g_ref[...]) * (1.0 - lam_init)).astype(o_ref.dtype)


def _diff_attention(tab_flat, lam_vecs, subln_g, qkv, diag, near, batch, seq, lam_init):
    m = batch * seq
    nq = seq // ATT_T
    return pl.pallas_call(
        functools.partial(_diff_attn_kernel, lam_init=lam_init),
        grid=(batch, DA_HEADS, nq),
        in_specs=[pl.BlockSpec(memory_space=pltpu.SMEM),
                  pl.BlockSpec((4, DA_HD), lambda b, h, i: (0, 0)),
                  pl.BlockSpec((1, 2 * DA_HD), lambda b, h, i: (0, 0)),
                  pl.BlockSpec((ATT_T, LANES), lambda b, h, i: (b * nq + i, _QA + h)),
                  pl.BlockSpec((seq, LANES), lambda b, h, i: (b, _KA + h)),
                  pl.BlockSpec((seq, LANES), lambda b, h, i: (b, _VA + h)),
                  pl.BlockSpec((1, ATT_T, ATT_T), lambda b, h, i: (h, 0, 0)),
                  pl.BlockSpec((1, ATT_T, ATT_T), lambda b, h, i: (h, 0, 0))],
        out_specs=pl.BlockSpec((ATT_T, LANES), lambda b, h, i: (b * nq + i, h)),
        out_shape=jax.ShapeDtypeStruct((m, DA_W), BF16),
        compiler_params=_params("arbitrary", "arbitrary", "arbitrary"),
        name="diff_attention",
    )(tab_flat, lam_vecs, subln_g.reshape(1, 2 * DA_HD), qkv, qkv, qkv, diag, near)


def _swa_kernel(sink_ref, q_ref, kp_ref, ko_ref, vp_ref, vo_ref, bias_ref, o_ref):
    qi = pl.program_id(1)
    t = SW_T
    groups = SW_QW // LANES
    q = q_ref[...] * (SW_HD ** -0.5)
    qs = jnp.concatenate([_split_heads(q[:, g * LANES:(g + 1) * LANES]) for g in range(groups)], axis=0)
    bias = bias_ref[...]
    no_prev = jnp.where(qi > 0, 0.0, NEG)
    s_prev = _dot_t(qs, kp_ref[...]) + bias[:, :t] + no_prev
    s_own = _dot_t(qs, ko_ref[...]) + bias[:, t:]
    sink = sink_ref[...]
    m = jnp.maximum(jnp.maximum(jnp.max(s_prev, axis=1, keepdims=True),
                                jnp.max(s_own, axis=1, keepdims=True)), sink)
    p_prev = jnp.exp(s_prev - m)
    p_own = jnp.exp(s_own - m)
    l = (jnp.sum(p_prev, axis=1, keepdims=True) + jnp.sum(p_own, axis=1, keepdims=True)
         + jnp.exp(sink - m))
    o = (jnp.dot(p_prev.astype(BF16), vp_ref[...], preferred_element_type=F32)
         + jnp.dot(p_own.astype(BF16), vo_ref[...], preferred_element_type=F32)) / l
    for g in range(groups):
        o_ref[:, g * LANES:(g + 1) * LANES] = _merge_heads(o[2 * g * t:(2 * g + 2) * t]).astype(o_ref.dtype)


def _swa_attention(sink_col, qkv, win_rows, batch, seq):
    m = batch * seq
    nq = seq // SW_T
    rows = SW_HEADS * SW_T
    prev = lambda b, i: (b * nq + jnp.maximum(i - 1, 0), _KB)
    own = lambda b, i: (b * nq + i, _KB)
    prev_v = lambda b, i: (b * nq + jnp.maximum(i - 1, 0), _VB)
    own_v = lambda b, i: (b * nq + i, _VB)
    return pl.pallas_call(
        _swa_kernel,
        grid=(batch, nq),
        in_specs=[pl.BlockSpec((rows, 1), lambda b, i: (0, 0)),
                  pl.BlockSpec((SW_T, SW_QW), lambda b, i: (b * nq + i, _QB // SW_QW)),
                  pl.BlockSpec((SW_T, LANES), prev),
                  pl.BlockSpec((SW_T, LANES), own),
                  pl.BlockSpec((SW_T, LANES), prev_v),
                  pl.BlockSpec((SW_T, LANES), own_v),
                  pl.BlockSpec((rows, 2 * SW_T), lambda b, i: (0, 0))],
        out_specs=pl.BlockSpec((SW_T, SW_QW), lambda b, i: (b * nq + i, 0)),
        out_shape=jax.ShapeDtypeStruct((m, SW_QW), BF16),
        compiler_params=_params("arbitrary", "arbitrary"),
        name="swa_attention",
    )(sink_col, qkv, qkv, qkv, qkv, qkv, win_rows)


def _moba_kernel(tab_ref, q_ref, k_ref, v_ref, diag_ref, near_ref, o_ref, *, n_blocks):
    hp = pl.program_id(1)
    qi = pl.program_id(2)
    t = ATT_T
    head0 = DA_HEADS + SW_HEADS + 2 * hp
    row = lax.broadcasted_iota(jnp.int32, (2 * t, 1), 0)
    c_far = jnp.where(row < t, tab_ref[(N_BUCKETS - 1) * N_ATT_HEADS + head0],
                      tab_ref[(N_BUCKETS - 1) * N_ATT_HEADS + head0 + 1])
    q_raw = _split_heads(q_ref[...])
    q2 = q_raw * (MB_HD ** -0.5)

    kf = k_ref[...].astype(F32).reshape(n_blocks, MB_BLOCK, LANES)
    kmean = jnp.sum(kf, axis=1) * (1.0 / MB_BLOCK)
    gate = _dot_t(q_raw.astype(F32), kmean, precision=lax.Precision.HIGHEST)
    blk = lax.broadcasted_iota(jnp.int32, gate.shape, 1)

    def not_selected(kj):
        g_kj = jnp.sum(jnp.where(blk == kj, gate, 0.0), axis=1, keepdims=True)
        beats = ((gate > g_kj) | ((gate == g_kj) & (blk < kj))) & (blk < qi)
        rank = jnp.sum(beats.astype(F32), axis=1, keepdims=True)
        return jnp.where(rank < MB_TOPK, 0.0, NEG)

    def logits(kj):
        return _dot_t(q2, _kv_block(k_ref, kj))

    carry = _softmax_first(logits(qi) + diag_ref[...], _kv_block(v_ref, qi))

    def near(_, c):
        return _softmax_next(logits(qi - 1) + near_ref[...] + not_selected(qi - 1),
                             _kv_block(v_ref, qi - 1), c)

    carry = lax.fori_loop(0, jnp.minimum(qi, 1), near, carry)

    def far(kj, c):
        return _softmax_next(logits(kj) + (c_far + not_selected(kj)), _kv_block(v_ref, kj), c)

    _, l, acc = lax.fori_loop(0, jnp.maximum(qi - 1, 0), far, carry)
    o_ref[...] = _merge_heads(acc / l).astype(o_ref.dtype)


def _moba_attention(tab_flat, qkv, diag_rows, near_rows, batch, seq):
    m = batch * seq
    nq = seq // ATT_T
    pair0 = (DA_HEADS + SW_HEADS) // 2
    return pl.pallas_call(
        functools.partial(_moba_kernel, n_blocks=seq // MB_BLOCK),
        grid=(batch, MB_HEADS // 2, nq),
        in_specs=[pl.BlockSpec(memory_space=pltpu.SMEM),
                  pl.BlockSpec((ATT_T, LANES), lambda b, h, i: (b * nq + i, _QC + h)),
                  pl.BlockSpec((seq, LANES), lambda b, h, i: (b, _KC + h)),
                  pl.BlockSpec((seq, LANES), lambda b, h, i: (b, _VC + h)),
                  pl.BlockSpec((2 * ATT_T, ATT_T), lambda b, h, i: (pair0 + h, 0)),
                  pl.BlockSpec((2 * ATT_T, ATT_T), lambda b, h, i: (pair0 + h, 0))],
        out_specs=pl.BlockSpec((ATT_T, LANES), lambda b, h, i: (b * nq + i, h)),
        out_shape=jax.ShapeDtypeStruct((m, MB_W), BF16),
        compiler_params=_params("arbitrary", "arbitrary", "arbitrary"),
        name="moba_attention",
    )(tab_flat, qkv, qkv, qkv, diag_rows, near_rows)


def _mix_out_kernel(ya_ref, yb_ref, yc_ref, ga_ref, gb_ref, gc_ref, x_ref, woa_ref, wob_ref, woc_ref,
                    wout_ref, pg_ref, ng_ref, xo_ref, ho_ref):
    def branch(y_ref, w_ref, g_ref):
        return g_ref[...].astype(F32) * jnp.dot(y_ref[...], w_ref[...], preferred_element_type=F32)

    mix = branch(ya_ref, woa_ref, ga_ref) + branch(yb_ref, wob_ref, gb_ref) + branch(yc_ref, woc_ref, gc_ref)
    z = jnp.dot(mix.astype(BF16), wout_ref[...], preferred_element_type=F32)
    xn = x_ref[...] + _rms(z, pg_ref[...])
    xo_ref[...] = xn
    ho_ref[...] = _rms(xn, ng_ref[...]).astype(ho_ref.dtype)


def _mix_out(ya, yb, yc, gates, x2, woa, wob, woc, wout, post_g, next_g):
    m, d = x2.shape
    tm = _pick(m, (256, 128))
    row = lambda i: (i, 0)
    const = lambda i: (0, 0)
    return pl.pallas_call(
        _mix_out_kernel,
        grid=(m // tm,),
        in_specs=[pl.BlockSpec((tm, DA_W), row), pl.BlockSpec((tm, SW_QW), row), pl.BlockSpec((tm, MB_W), row),
                  pl.BlockSpec((tm, d), lambda i: (i, 0)), pl.BlockSpec((tm, d), lambda i: (i, 1)),
                  pl.BlockSpec((tm, d), lambda i: (i, 2)),
                  pl.BlockSpec((tm, d), row),
                  pl.BlockSpec((DA_W, d), const), pl.BlockSpec((SW_QW, d), const), pl.BlockSpec((MB_W, d), const),
                  pl.BlockSpec((d, d), const), pl.BlockSpec((1, d), const), pl.BlockSpec((1, d), const)],
        out_specs=[pl.BlockSpec((tm, d), row), pl.BlockSpec((tm, d), row)],
        out_shape=[jax.ShapeDtypeStruct((m, d), F32), jax.ShapeDtypeStruct((m, d), BF16)],
        compiler_params=_params("arbitrary"),
        name="mix_out",
    )(ya, yb, yc, gates, gates, gates, x2, woa, wob, woc, wout, post_g.reshape(1, d), next_g.reshape(1, d))


CONV_W = 3
HALO = 8


def _ffn_up_kernel(h_ref, wg_ref, wv_ref, cwg_ref, cwv_ref, cbg_ref, cbv_ref, o_ref, ug_s, uv_s, *, tiles_per_seq):
    i = pl.program_id(1)
    tm = h_ref.shape[0]

    @pl.when(i % tiles_per_seq == 0)
    def _():
        ug_s[0:HALO, :] = jnp.zeros((HALO, ug_s.shape[1]), F32)
        uv_s[0:HALO, :] = jnp.zeros((HALO, uv_s.shape[1]), F32)

    h = h_ref[...]

    def conv(w_ref, cw_ref, cb_ref, u_s):
        u = jnp.dot(h, w_ref[...], preferred_element_type=F32)
        u_s[HALO:HALO + tm, :] = u
        cw = cw_ref[...]
        c = cb_ref[...] + u_s[HALO - 2:HALO - 2 + tm, :] * cw[0:1]
        c = c + u_s[HALO - 1:HALO - 1 + tm, :] * cw[1:2]
        c = c + u * cw[2:3]
        u_s[0:HALO, :] = u[tm - HALO:tm]
        return c

    gate = conv(wg_ref, cwg_ref, cbg_ref, ug_s)
    val = conv(wv_ref, cwv_ref, cbv_ref, uv_s)
    o_ref[...] = (jax.nn.gelu(gate, approximate=True) * val).astype(o_ref.dtype)


def _ffn_up(h, w_up, conv_w, conv_b, seq):
    m, d = h.shape
    f = w_up.shape[1] // 2
    tm = _pick(seq, (1024, 512, 256, 128))
    tn = _pick(f, (512, 384, 256, 128))
    nj = f // tn
    gate_col = lambda j, i: (0, j)
    val_col = lambda j, i: (0, nj + j)
    return pl.pallas_call(
        functools.partial(_ffn_up_kernel, tiles_per_seq=seq // tm),
        grid=(nj, m // tm),
        in_specs=[pl.BlockSpec((tm, d), lambda j, i: (i, 0)),
                  pl.BlockSpec((d, tn), gate_col), pl.BlockSpec((d, tn), val_col),
                  pl.BlockSpec((CONV_W, tn), gate_col), pl.BlockSpec((CONV_W, tn), val_col),
                  pl.BlockSpec((1, tn), gate_col), pl.BlockSpec((1, tn), val_col)],
        out_specs=pl.BlockSpec((tm, tn), lambda j, i: (i, j)),
        out_shape=jax.ShapeDtypeStruct((m, f), BF16),
        scratch_shapes=[pltpu.VMEM((HALO + tm, tn), F32), pltpu.VMEM((HALO + tm, tn), F32)],
        compiler_params=_params("arbitrary", "arbitrary"),
        name="ffn_up",
    )(h, w_up, w_up, conv_w, conv_w, conv_b.reshape(1, 2 * f), conv_b.reshape(1, 2 * f))


def _ffn_down_kernel(a_ref, w_ref, x_ref, pg_ref, ng_ref, xo_ref, *rest, emit_next):
    acc_s = rest[-1]
    k = pl.program_id(1)
    part = jnp.dot(a_ref[...], w_ref[...], preferred_element_type=F32)

    @pl.when(k == 0)
    def _():
        acc_s[...] = part

    @pl.when(k > 0)
    def _():
        acc_s[...] += part

    @pl.when(k == pl.num_programs(1) - 1)
    def _():
        xn = x_ref[...] + _rms(acc_s[...], pg_ref[...])
        xo_ref[...] = xn
        if emit_next:
            rest[0][...] = _rms(xn, ng_ref[...]).astype(rest[0].dtype)


def _ffn_down(a, w_down, x2, post_g, next_g):
    m, d = x2.shape
    f = a.shape[1]
    tm = _pick(m, (512, 256, 128))
    tk = _pick(f, (1408, 1024, 512, 384, 256, 128))
    emit_next = next_g is not None
    row = lambda i, k: (i, 0)
    const = lambda i, k: (0, 0)
    out_specs = [pl.BlockSpec((tm, d), row)]
    out_shape = [jax.ShapeDtypeStruct((m, d), F32)]
    if emit_next:
        out_specs.append(pl.BlockSpec((tm, d), row))
        out_shape.append(jax.ShapeDtypeStruct((m, d), BF16))
    ng = next_g if emit_next else post_g
    res = pl.pallas_call(
        functools.partial(_ffn_down_kernel, emit_next=emit_next),
        grid=(m // tm, f // tk),
        in_specs=[pl.BlockSpec((tm, tk), lambda i, k: (i, k)), pl.BlockSpec((tk, d), lambda i, k: (k, 0)),
                  pl.BlockSpec((tm, d), row), pl.BlockSpec((1, d), const), pl.BlockSpec((1, d), const)],
        out_specs=out_specs,
        out_shape=out_shape,
        scratch_shapes=[pltpu.VMEM((tm, d), F32)],
        compiler_params=_params("arbitrary", "arbitrary"),
        name="ffn_down",
    )(a, w_down, x2, post_g.reshape(1, d), ng.reshape(1, d))
    return (res[0], res[1]) if emit_next else (res[0], None)


def _permute_sw_heads(w, axis):
    shape = w.shape
    split = shape[:axis] + (SW_HEADS, SW_HD) + shape[axis + 1:]
    return jnp.take(w.reshape(split), jnp.array(_SW_PERM), axis=axis).reshape(shape)


@jax.jit
def _trunk(x, rel_bias_table, w_in, b_gate, lam_q1, lam_k1, lam_q2, lam_k2, diff_subln_g, sinks, w_oa, w_ob,
           w_oc, w_out, pre_mix_g, post_mix_g, pre_ffn_g, post_ffn_g, w_up, conv_w, conv_b, w_down):
    batch, seq, d = x.shape
    depth = w_in.shape[0]
    assert seq % ATT_T == 0 and seq % MB_BLOCK == 0 and d % LANES == 0
    assert w_in.shape[2] == QKV_W + N_BRANCH * d
    m = batch * seq

    tab_flat = rel_bias_table.astype(F32).reshape(-1)
    diag, near, win = _bias_tiles(tab_flat)
    diag_rows = diag.reshape(N_ATT_HEADS * ATT_T, ATT_T)
    near_rows = near.reshape(N_ATT_HEADS * ATT_T, ATT_T)
    sw_heads = DA_HEADS + jnp.array(_SW_PERM)
    win_rows = jnp.take(win, sw_heads, axis=0).reshape(SW_HEADS * SW_T, 2 * SW_T)

    x2 = x.reshape(m, d)
    h = _prenorm(x2, pre_mix_g[0])
    for l in range(depth):
        w_qkv = w_in[l, :, :QKV_W]
        w_qkv = jnp.concatenate([w_qkv[:, :_QB], _permute_sw_heads(w_qkv[:, _QB:_QB + SW_QW], 1),
                                 w_qkv[:, _QB + SW_QW:]], axis=1).astype(BF16)
        qkv = _in_proj(h, w_qkv)
        gates = _in_proj(h, w_in[l, :, QKV_W:].astype(BF16), b_gate[l])

        lam_init = 0.8 - 0.6 * math.exp(-0.3 * l)
        lam_vecs = jnp.stack([lam_q1[l], lam_k1[l], lam_q2[l], lam_k2[l]]).astype(F32)
        ya = _diff_attention(tab_flat, lam_vecs, diff_subln_g[l], qkv, diag, near, batch, seq, lam_init)
        sink_col = jnp.repeat(sinks[l].astype(F32)[jnp.array(_SW_PERM)], SW_T).reshape(SW_HEADS * SW_T, 1)
        yb = _swa_attention(sink_col, qkv, win_rows, batch, seq)
        yc = _moba_attention(tab_flat, qkv, diag_rows, near_rows, batch, seq)

        x2, h = _mix_out(ya, yb, yc, gates, x2, w_oa[l].astype(BF16),
                         _permute_sw_heads(w_ob[l], 0).astype(BF16), w_oc[l].astype(BF16),
                         w_out[l].astype(BF16), post_mix_g[l], pre_ffn_g[l])

        a = _ffn_up(h, w_up[l].astype(BF16), conv_w[l], conv_b[l], seq)
        next_g = pre_mix_g[l + 1] if l + 1 < depth else None
        x2, h = _ffn_down(a, w_down[l].astype(BF16), x2, post_ffn_g[l], next_g)
    return x2.reshape(batch, seq, d)


def kernel(x, rel_bias_table, w_in, b_gate, lam_q1, lam_k1, lam_q2, lam_k2, diff_subln_g, sinks, w_oa, w_ob, w_oc, w_out, pre_mix_g, post_mix_g, pre_ffn_g, post_ffn_g, w_up, conv_w, conv_b, w_down):
    return _trunk(x, rel_bias_table, w_in, b_gate, lam_q1, lam_k1, lam_q2, lam_k2, diff_subln_g, sinks, w_oa, w_ob,
                  w_oc, w_out, pre_mix_g, post_mix_g, pre_ffn_g, post_ffn_g, w_up, conv_w, conv_b, w_down)
```

```python
import functools
import math

import jax
import jax.numpy as jnp
from jax import lax
from jax.experimental import pallas as pl
from jax.experimental.pallas import tpu as pltpu

DA_HEADS = 4
DA_HD = 64
DA_W = DA_HEADS * 2 * DA_HD
SW_HEADS = 8
SW_KV = 2
SW_HD = 64
WINDOW = 128
SW_QW = SW_HEADS * SW_HD
SW_KW = SW_KV * SW_HD
MB_HEADS = 8
MB_HD = 64
MB_W = MB_HEADS * MB_HD
MB_BLOCK = 256
MB_TOPK = 3
N_BUCKETS = 32
MAX_DIST = 128
N_ATT_HEADS = DA_HEADS + SW_HEADS + MB_HEADS
N_BRANCH = 3
QKV_W = 3 * DA_W + SW_QW + 2 * SW_KW + 3 * MB_W
EPS = 1e-6

LANES = 128
ATT_T = 256
SW_T = WINDOW
NEG = -1e30
LOG2E = math.log2(math.e)
Q_SCALE = DA_HD ** -0.5 * LOG2E
VMEM_LIMIT = 56 * 1024 * 1024

_QA, _KA, _VA = 0, DA_W // LANES, 2 * DA_W // LANES
_QB = 3 * DA_W
_KB = (_QB + SW_QW) // LANES
_VB = _KB + 1
_QC = (_QB + SW_QW + 2 * SW_KW) // LANES
_KC = _QC + MB_W // LANES
_VC = _KC + MB_W // LANES
_SW_PERM = tuple(h // 2 + (h % 2) * (SW_HEADS // SW_KV) for h in range(SW_HEADS))

F32 = jnp.float32
BF16 = jnp.bfloat16


def _pick(n, candidates):
    for c in candidates:
        if n % c == 0:
            return c
    raise ValueError(f"no tile in {candidates} divides {n}")


def _params(*sem):
    return pltpu.CompilerParams(dimension_semantics=sem, vmem_limit_bytes=VMEM_LIMIT)


def _rms(x, g):
    return x * lax.rsqrt(jnp.mean(x * x, axis=-1, keepdims=True) + EPS) * g


def _dot_t(a, b, **kw):
    return lax.dot_general(a, b, (((1,), (1,)), ((), ())), preferred_element_type=F32, **kw)


def _rel_bucket(dist):
    n = jnp.maximum(dist, 0)
    max_exact = N_BUCKETS // 2
    nf = jnp.maximum(n, 1).astype(F32)
    large = max_exact + (jnp.log(nf / max_exact) / math.log(MAX_DIST / max_exact)
                         * (N_BUCKETS - max_exact)).astype(jnp.int32)
    large = jnp.minimum(large, N_BUCKETS - 1)
    return jnp.where(n < max_exact, n, large)


def _bias_lookup(tab_ref, head, dist):
    bucket = _rel_bucket(dist)
    acc = jnp.zeros(dist.shape, F32)
    for b in range(N_BUCKETS):
        acc = jnp.where(bucket == b, tab_ref[b * N_ATT_HEADS + head], acc)
    return acc * LOG2E


FAR, NEAR, DIAG = 0, 1, 2


def _tile_kind(kj, qi):
    return jnp.clip(kj - qi + DIAG, FAR, DIAG)


def _bias_kernel(tab_ref, o_ref):
    e = pl.program_id(0)
    moba0 = DA_HEADS + SW_HEADS + 2 * (e - DA_HEADS)
    heads = (jnp.where(e < DA_HEADS, e, moba0), jnp.where(e < DA_HEADS, e, moba0 + 1))
    d = (lax.broadcasted_iota(jnp.int32, (ATT_T, ATT_T), 1)
         - lax.broadcasted_iota(jnp.int32, (ATT_T, ATT_T), 0))
    for c, head in enumerate(heads):
        cols = slice(c * ATT_T, (c + 1) * ATT_T)
        o_ref[0, FAR, :, cols] = _bias_lookup(tab_ref, head, d + 2 * ATT_T)
        o_ref[0, NEAR, :, cols] = _bias_lookup(tab_ref, head, d + ATT_T)
        o_ref[0, DIAG, :, cols] = jnp.where(d >= 0, _bias_lookup(tab_ref, head, d), NEG)


def _window_kernel(tab_ref, win_ref):
    c = pl.program_id(0)
    head = DA_HEADS + c // 2 + (c % 2) * (SW_HEADS // SW_KV)
    d = (lax.broadcasted_iota(jnp.int32, (2 * SW_T, SW_T), 1) + SW_T
         - lax.broadcasted_iota(jnp.int32, (2 * SW_T, SW_T), 0))
    win_ref[...] = jnp.where((d >= 0) & (d < WINDOW), _bias_lookup(tab_ref, head, d), NEG)


def _bias_tiles(tab_flat):
    assert ATT_T + 1 >= MAX_DIST
    n = DA_HEADS + MB_HEADS // 2
    tiles = pl.pallas_call(
        _bias_kernel,
        grid=(n,),
        in_specs=[pl.BlockSpec(memory_space=pltpu.SMEM)],
        out_specs=pl.BlockSpec((1, 3, ATT_T, 2 * ATT_T), lambda e: (e, 0, 0, 0)),
        out_shape=jax.ShapeDtypeStruct((n, 3, ATT_T, 2 * ATT_T), F32),
        compiler_params=_params("arbitrary"),
        name="bias_tiles",
    )(tab_flat)
    win = pl.pallas_call(
        _window_kernel,
        grid=(SW_HEADS,),
        in_specs=[pl.BlockSpec(memory_space=pltpu.SMEM)],
        out_specs=pl.BlockSpec((2 * SW_T, SW_T), lambda c: (0, c)),
        out_shape=jax.ShapeDtypeStruct((2 * SW_T, SW_HEADS * SW_T), F32),
        compiler_params=_params("arbitrary"),
        name="window_tiles",
    )(tab_flat)
    return tiles, win


def _norm_kernel(x_ref, g_ref, o_ref):
    o_ref[...] = _rms(x_ref[...], g_ref[...]).astype(o_ref.dtype)


def _prenorm(x2, g):
    m, d = x2.shape
    tm = _pick(m, (512, 256, 128))
    return pl.pallas_call(
        _norm_kernel,
        grid=(m // tm,),
        in_specs=[pl.BlockSpec((tm, d), lambda i: (i, 0)), pl.BlockSpec((1, d), lambda i: (0, 0))],
        out_specs=pl.BlockSpec((tm, d), lambda i: (i, 0)),
        out_shape=jax.ShapeDtypeStruct((m, d), BF16),
        compiler_params=_params("arbitrary"),
        name="prenorm",
    )(x2, g.reshape(1, d))


def _proj_kernel(h_ref, w_ref, o_ref):
    o_ref[...] = jnp.dot(h_ref[...], w_ref[...], preferred_element_type=F32).astype(o_ref.dtype)


def _gate_kernel(h_ref, w_ref, b_ref, o_ref):
    acc = jnp.dot(h_ref[...], w_ref[...], preferred_element_type=F32)
    o_ref[...] = jax.nn.sigmoid(acc + b_ref[...]).astype(o_ref.dtype)


def _in_proj(h, w, bias=None):
    m, d = h.shape
    n = w.shape[1]
    tm = _pick(m, (1024, 512, 256, 128))
    tn = _pick(n, (1280, 1024, 768, 512, 256, 128))
    in_specs = [pl.BlockSpec((tm, d), lambda i, j: (i, 0)), pl.BlockSpec((d, tn), lambda i, j: (0, j))]
    args = [h, w]
    if bias is not None:
        in_specs.append(pl.BlockSpec((1, tn), lambda i, j: (0, j)))
        args.append(bias.reshape(1, n))
    return pl.pallas_call(
        _proj_kernel if bias is None else _gate_kernel,
        grid=(m // tm, n // tn),
        in_specs=in_specs,
        out_specs=pl.BlockSpec((tm, tn), lambda i, j: (i, j)),
        out_shape=jax.ShapeDtypeStruct((m, n), BF16),
        compiler_params=_params("arbitrary", "arbitrary"),
        name="in_proj" if bias is None else "in_gates",
    )(*args)


def _split_heads(q):
    lane = lax.broadcasted_iota(jnp.int32, q.shape, 1)
    zero = jnp.zeros_like(q)
    return jnp.concatenate([jnp.where(lane < LANES // 2, q, zero),
                            jnp.where(lane >= LANES // 2, q, zero)], axis=0)


def _merge_heads_t(o):
    t = o.shape[1] // 2
    feat = lax.broadcasted_iota(jnp.int32, (LANES, t), 0)
    return jnp.where(feat < LANES // 2, o[:, :t], o[:, t:]).T


def _pv_t(v, p):
    return lax.dot_general(v, p, (((0,), (0,)), ((), ())), preferred_element_type=F32)


def _softmax_stats(s, m, l, keep=None):
    m_tile = jnp.max(s, axis=0, keepdims=True)
    if keep is not None:
        m_tile = jnp.where(keep, m_tile, NEG)
    m_new = jnp.maximum(m, m_tile)
    alpha = jnp.exp2(m - m_new)
    p = jnp.exp2(s - (m_new if keep is None else jnp.where(keep, m_new, -NEG)))
    l = alpha * l + jnp.sum(p, axis=0, keepdims=True)
    return m_new, l, p.astype(BF16), alpha


def _flash(qi, logits_fn, v_ref, scratch, keep_fn=None):
    s_s, p_s, acc_s = scratch
    keep = (lambda kj: None) if keep_fn is None else keep_fn

    def stats(kj, m, l):
        m, l, p, alpha = _softmax_stats(s_s[...], m, l, keep(kj))
        p_s[...] = p
        return m, l, alpha

    n_queries = s_s.shape[1]
    s_s[...] = logits_fn(0)
    carry = stats(0, jnp.full((1, n_queries), -jnp.inf, F32), jnp.zeros((1, n_queries), F32))
    s_s[...] = logits_fn(jnp.minimum(1, qi))
    acc_s[...] = jnp.zeros(acc_s.shape, F32)

    def body(kj, carry):
        m, l, alpha_prev = carry
        acc_s[...] = alpha_prev * acc_s[...] + _pv_t(_kv_block(v_ref, kj - 1), p_s[...])
        carry = stats(kj, m, l)
        s_s[...] = logits_fn(jnp.minimum(kj + 1, qi))
        return carry

    _, l, alpha = lax.fori_loop(1, qi + 1, body, carry)
    return (alpha * acc_s[...] + _pv_t(_kv_block(v_ref, qi), p_s[...])) / l


def _flash_scratch(n_queries):
    return [pltpu.VMEM((ATT_T, n_queries), F32), pltpu.VMEM((ATT_T, n_queries), BF16),
            pltpu.VMEM((LANES, n_queries), F32)]


def _kv_block(ref, kj):
    return ref[pl.ds(pl.multiple_of(kj * ATT_T, ATT_T), ATT_T), :]


def _diff_attn_kernel(lam_ref, g_ref, q_ref, k_ref, v_ref, bias_ref, o_ref, *scratch, lam_init):
    qi = pl.program_id(2)
    t = ATT_T
    q2 = _split_heads(q_ref[...])

    def logits(kj):
        return _dot_t(_kv_block(k_ref, kj), q2) + bias_ref[0, _tile_kind(kj, qi)]

    o = _flash(qi, logits, v_ref, scratch)
    lv = lam_ref[...]
    lam = (jnp.exp(jnp.sum(lv[0:1] * lv[1:2], axis=1, keepdims=True))
           - jnp.exp(jnp.sum(lv[2:3] * lv[3:4], axis=1, keepdims=True)) + lam_init)
    o = o[:, :t] - lam * o[:, t:]
    o = o * lax.rsqrt(jnp.mean(o * o, axis=0, keepdims=True) + EPS) * (g_ref[...] * (1.0 - lam_init))
    o_ref[...] = o.T.astype(o_ref.dtype)


def _diff_attention(lam_vecs, subln_g, qkv, tiles, batch, seq, lam_init):
    m = batch * seq
    nq = seq // ATT_T
    return pl.pallas_call(
        functools.partial(_diff_attn_kernel, lam_init=lam_init),
        grid=(batch, DA_HEADS, nq),
        in_specs=[pl.BlockSpec((4, DA_HD), lambda b, h, i: (0, 0)),
                  pl.BlockSpec((2 * DA_HD, 1), lambda b, h, i: (0, 0)),
                  pl.BlockSpec((ATT_T, LANES), lambda b, h, i: (b * nq + i, _QA + h)),
                  pl.BlockSpec((seq, LANES), lambda b, h, i: (b, _KA + h)),
                  pl.BlockSpec((seq, LANES), lambda b, h, i: (b, _VA + h)),
                  pl.BlockSpec((1, 3, ATT_T, 2 * ATT_T), lambda b, h, i: (h, 0, 0, 0))],
        out_specs=pl.BlockSpec((ATT_T, LANES), lambda b, h, i: (b * nq + i, h)),
        out_shape=jax.ShapeDtypeStruct((m, DA_W), BF16),
        scratch_shapes=_flash_scratch(2 * ATT_T),
        compiler_params=_params("arbitrary", "arbitrary", "arbitrary"),
        name="diff_attention",
    )(lam_vecs, subln_g.astype(F32).reshape(2 * DA_HD, 1), qkv, qkv, qkv, tiles)


def _swa_kernel(sink_ref, q_ref, kp_ref, ko_ref, vp_ref, vo_ref, bias_ref, o_ref):
    qi = pl.program_id(1)
    t = SW_T
    groups = SW_QW // LANES
    q = q_ref[...]
    qs = jnp.concatenate([_split_heads(q[:, g * LANES:(g + 1) * LANES]) for g in range(groups)], axis=0)
    no_prev = jnp.where(qi > 0, 0.0, NEG)
    s_prev = _dot_t(kp_ref[...], qs) + bias_ref[0:t, :] + no_prev
    s_own = _dot_t(ko_ref[...], qs) + bias_ref[t:2 * t, :]
    sink = sink_ref[...]
    m = jnp.maximum(jnp.maximum(jnp.max(s_prev, axis=0, keepdims=True),
                                jnp.max(s_own, axis=0, keepdims=True)), sink)
    p_prev = jnp.exp2(s_prev - m)
    p_own = jnp.exp2(s_own - m)
    l = (jnp.sum(p_prev, axis=0, keepdims=True) + jnp.sum(p_own, axis=0, keepdims=True)
         + jnp.exp2(sink - m))
    o = (_pv_t(vp_ref[...], p_prev) + _pv_t(vo_ref[...], p_own)) / l
    for g in range(groups):
        o_ref[:, g * LANES:(g + 1) * LANES] = _merge_heads_t(o[:, 2 * g * t:(2 * g + 2) * t]).astype(o_ref.dtype)


def _swa_attention(sink_row, qkv, win, batch, seq):
    m = batch * seq
    nq = seq // SW_T
    cols = SW_HEADS * SW_T
    prev = lambda b, i: (b * nq + jnp.maximum(i - 1, 0), _KB)
    own = lambda b, i: (b * nq + i, _KB)
    prev_v = lambda b, i: (b * nq + jnp.maximum(i - 1, 0), _VB)
    own_v = lambda b, i: (b * nq + i, _VB)
    return pl.pallas_call(
        _swa_kernel,
        grid=(batch, nq),
        in_specs=[pl.BlockSpec((1, cols), lambda b, i: (0, 0)),
                  pl.BlockSpec((SW_T, SW_QW), lambda b, i: (b * nq + i, _QB // SW_QW)),
                  pl.BlockSpec((SW_T, LANES), prev),
                  pl.BlockSpec((SW_T, LANES), own),
                  pl.BlockSpec((SW_T, LANES), prev_v),
                  pl.BlockSpec((SW_T, LANES), own_v),
                  pl.BlockSpec((2 * SW_T, cols), lambda b, i: (0, 0))],
        out_specs=pl.BlockSpec((SW_T, SW_QW), lambda b, i: (b * nq + i, 0)),
        out_shape=jax.ShapeDtypeStruct((m, SW_QW), BF16),
        compiler_params=_params("arbitrary", "arbitrary"),
        name="swa_attention",
    )(sink_row, qkv, qkv, qkv, qkv, qkv, win)


def _moba_kernel(q_ref, k_ref, v_ref, bias_ref, o_ref, kmean_s, *scratch, n_blocks):
    qi = pl.program_id(2)
    q2 = _split_heads(q_ref[...])

    @pl.when(qi == 0)
    def _():
        kf = k_ref[...].astype(F32).reshape(n_blocks, MB_BLOCK, LANES)
        kmean_s[...] = jnp.sum(kf, axis=1) * (1.0 / MB_BLOCK)

    gate = _dot_t(kmean_s[...], q2.astype(F32), precision=lax.Precision.HIGHEST)
    blk = lax.broadcasted_iota(jnp.int32, gate.shape, 0)

    def selected(kj):
        g_kj = jnp.sum(jnp.where(blk == kj, gate, 0.0), axis=0, keepdims=True)
        beats = ((gate > g_kj) | ((gate == g_kj) & (blk < kj))) & (blk < qi)
        rank = jnp.sum(beats.astype(F32), axis=0, keepdims=True)
        return (rank < MB_TOPK) | (kj >= qi)

    def logits(kj):
        return _dot_t(_kv_block(k_ref, kj), q2) + bias_ref[0, _tile_kind(kj, qi)]

    o_ref[...] = _merge_heads_t(_flash(qi, logits, v_ref, scratch, selected)).astype(o_ref.dtype)


def _moba_attention(qkv, tiles, batch, seq):
    m = batch * seq
    nq = seq // ATT_T
    n_blocks = seq // MB_BLOCK
    return pl.pallas_call(
        functools.partial(_moba_kernel, n_blocks=n_blocks),
        grid=(batch, MB_HEADS // 2, nq),
        in_specs=[pl.BlockSpec((ATT_T, LANES), lambda b, h, i: (b * nq + i, _QC + h)),
                  pl.BlockSpec((seq, LANES), lambda b, h, i: (b, _KC + h)),
                  pl.BlockSpec((seq, LANES), lambda b, h, i: (b, _VC + h)),
                  pl.BlockSpec((1, 3, ATT_T, 2 * ATT_T), lambda b, h, i: (DA_HEADS + h, 0, 0, 0))],
        out_specs=pl.BlockSpec((ATT_T, LANES), lambda b, h, i: (b * nq + i, h)),
        out_shape=jax.ShapeDtypeStruct((m, MB_W), BF16),
        scratch_shapes=[pltpu.VMEM((n_blocks, LANES), F32)] + _flash_scratch(2 * ATT_T),
        compiler_params=_params("arbitrary", "arbitrary", "arbitrary"),
        name="moba_attention",
    )(qkv, qkv, qkv, tiles)


def _mix_out_kernel(ya_ref, yb_ref, yc_ref, ga_ref, gb_ref, gc_ref, x_ref, woa_ref, wob_ref, woc_ref,
                    wout_ref, pg_ref, ng_ref, xo_ref, ho_ref):
    def branch(y_ref, w_ref, g_ref):
        return g_ref[...].astype(F32) * jnp.dot(y_ref[...], w_ref[...], preferred_element_type=F32)

    mix = branch(ya_ref, woa_ref, ga_ref) + branch(yb_ref, wob_ref, gb_ref) + branch(yc_ref, woc_ref, gc_ref)
    z = jnp.dot(mix.astype(BF16), wout_ref[...], preferred_element_type=F32)
    xn = x_ref[...] + _rms(z, pg_ref[...])
    xo_ref[...] = xn
    ho_ref[...] = _rms(xn, ng_ref[...]).astype(ho_ref.dtype)


def _mix_out(ya, yb, yc, gates, x2, woa, wob, woc, wout, post_g, next_g):
    m, d = x2.shape
    tm = _pick(m, (256, 128))
    row = lambda i: (i, 0)
    const = lambda i: (0, 0)
    return pl.pallas_call(
        _mix_out_kernel,
        grid=(m // tm,),
        in_specs=[pl.BlockSpec((tm, DA_W), row), pl.BlockSpec((tm, SW_QW), row), pl.BlockSpec((tm, MB_W), row),
                  pl.BlockSpec((tm, d), lambda i: (i, 0)), pl.BlockSpec((tm, d), lambda i: (i, 1)),
                  pl.BlockSpec((tm, d), lambda i: (i, 2)),
                  pl.BlockSpec((tm, d), row),
                  pl.BlockSpec((DA_W, d), const), pl.BlockSpec((SW_QW, d), const), pl.BlockSpec((MB_W, d), const),
                  pl.BlockSpec((d, d), const), pl.BlockSpec((1, d), const), pl.BlockSpec((1, d), const)],
        out_specs=[pl.BlockSpec((tm, d), row), pl.BlockSpec((tm, d), row)],
        out_shape=[jax.ShapeDtypeStruct((m, d), F32), jax.ShapeDtypeStruct((m, d), BF16)],
        compiler_params=_params("arbitrary"),
        name="mix_out",
    )(ya, yb, yc, gates, gates, gates, x2, woa, wob, woc, wout, post_g.reshape(1, d), next_g.reshape(1, d))


CONV_W = 3
HALO = 8


def _ffn_up_kernel(h_ref, wg_ref, wv_ref, cwg_ref, cwv_ref, cbg_ref, cbv_ref, o_ref, ug_s, uv_s, *, tiles_per_seq):
    i = pl.program_id(1)
    tm = h_ref.shape[0]

    @pl.when(i % tiles_per_seq == 0)
    def _():
        ug_s[0:HALO, :] = jnp.zeros((HALO, ug_s.shape[1]), F32)
        uv_s[0:HALO, :] = jnp.zeros((HALO, uv_s.shape[1]), F32)

    h = h_ref[...]

    def conv(w_ref, cw_ref, cb_ref, u_s):
        u = jnp.dot(h, w_ref[...], preferred_element_type=F32)
        u_s[HALO:HALO + tm, :] = u
        cw = cw_ref[...]
        c = cb_ref[...] + u_s[HALO - 2:HALO - 2 + tm, :] * cw[0:1]
        c = c + u_s[HALO - 1:HALO - 1 + tm, :] * cw[1:2]
        c = c + u * cw[2:3]
        u_s[0:HALO, :] = u[tm - HALO:tm]
        return c

    gate = conv(wg_ref, cwg_ref, cbg_ref, ug_s)
    val = conv(wv_ref, cwv_ref, cbv_ref, uv_s)
    o_ref[...] = (jax.nn.gelu(gate, approximate=True) * val).astype(o_ref.dtype)


def _ffn_up(h, w_up, conv_w, conv_b, seq):
    m, d = h.shape
    f = w_up.shape[1] // 2
    tm = _pick(seq, (1024, 512, 256, 128))
    tn = _pick(f, (512, 384, 256, 128))
    nj = f // tn
    gate_col = lambda j, i: (0, j)
    val_col = lambda j, i: (0, nj + j)
    return pl.pallas_call(
        functools.partial(_ffn_up_kernel, tiles_per_seq=seq // tm),
        grid=(nj, m // tm),
        in_specs=[pl.BlockSpec((tm, d), lambda j, i: (i, 0)),
                  pl.BlockSpec((d, tn), gate_col), pl.BlockSpec((d, tn), val_col),
                  pl.BlockSpec((CONV_W, tn), gate_col), pl.BlockSpec((CONV_W, tn), val_col),
                  pl.BlockSpec((1, tn), gate_col), pl.BlockSpec((1, tn), val_col)],
        out_specs=pl.BlockSpec((tm, tn), lambda j, i: (i, j)),
        out_shape=jax.ShapeDtypeStruct((m, f), BF16),
        scratch_shapes=[pltpu.VMEM((HALO + tm, tn), F32), pltpu.VMEM((HALO + tm, tn), F32)],
        compiler_params=_params("arbitrary", "arbitrary"),
        name="ffn_up",
    )(h, w_up, w_up, conv_w, conv_w, conv_b.reshape(1, 2 * f), conv_b.reshape(1, 2 * f))


def _ffn_down_kernel(a_ref, w_ref, x_ref, pg_ref, ng_ref, xo_ref, *rest, emit_next):
    acc_s = rest[-1]
    k = pl.program_id(1)
    part = jnp.dot(a_ref[...], w_ref[...], preferred_element_type=F32)

    @pl.when(k == 0)
    def _():
        acc_s[...] = part

    @pl.when(k > 0)
    def _():
        acc_s[...] += part

    @pl.when(k == pl.num_programs(1) - 1)
    def _():
        xn = x_ref[...] + _rms(acc_s[...], pg_ref[...])
        xo_ref[...] = xn
        if emit_next:
            rest[0][...] = _rms(xn, ng_ref[...]).astype(rest[0].dtype)


def _ffn_down(a, w_down, x2, post_g, next_g):
    m, d = x2.shape
    f = a.shape[1]
    tm = _pick(m, (512, 256, 128))
    tk = _pick(f, (1408, 1024, 512, 384, 256, 128))
    emit_next = next_g is not None
    row = lambda i, k: (i, 0)
    const = lambda i, k: (0, 0)
    out_specs = [pl.BlockSpec((tm, d), row)]
    out_shape = [jax.ShapeDtypeStruct((m, d), F32)]
    if emit_next:
        out_specs.append(pl.BlockSpec((tm, d), row))
        out_shape.append(jax.ShapeDtypeStruct((m, d), BF16))
    ng = next_g if emit_next else post_g
    res = pl.pallas_call(
        functools.partial(_ffn_down_kernel, emit_next=emit_next),
        grid=(m // tm, f // tk),
        in_specs=[pl.BlockSpec((tm, tk), lambda i, k: (i, k)), pl.BlockSpec((tk, d), lambda i, k: (k, 0)),
                  pl.BlockSpec((tm, d), row), pl.BlockSpec((1, d), const), pl.BlockSpec((1, d), const)],
        out_specs=out_specs,
        out_shape=out_shape,
        scratch_shapes=[pltpu.VMEM((tm, d), F32)],
        compiler_params=_params("arbitrary", "arbitrary"),
        name="ffn_down",
    )(a, w_down, x2, post_g.reshape(1, d), ng.reshape(1, d))
    return (res[0], res[1]) if emit_next else (res[0], None)


def _permute_sw_heads(w, axis):
    shape = w.shape
    split = shape[:axis] + (SW_HEADS, SW_HD) + shape[axis + 1:]
    return jnp.take(w.reshape(split), jnp.array(_SW_PERM), axis=axis).reshape(shape)


@jax.jit
def _trunk(x, rel_bias_table, w_in, b_gate, lam_q1, lam_k1, lam_q2, lam_k2, diff_subln_g, sinks, w_oa, w_ob,
           w_oc, w_out, pre_mix_g, post_mix_g, pre_ffn_g, post_ffn_g, w_up, conv_w, conv_b, w_down):
    batch, seq, d = x.shape
    depth = w_in.shape[0]
    assert seq % ATT_T == 0 and seq % MB_BLOCK == 0 and d % LANES == 0
    assert w_in.shape[2] == QKV_W + N_BRANCH * d
    m = batch * seq

    tab_flat = rel_bias_table.astype(F32).reshape(-1)
    tiles, win = _bias_tiles(tab_flat)
    perm = jnp.array(_SW_PERM)

    x2 = x.reshape(m, d)
    h = _prenorm(x2, pre_mix_g[0])
    for l in range(depth):
        w_qkv = w_in[l, :, :QKV_W]
        w_qkv = jnp.concatenate(
            [w_qkv[:, :DA_W] * Q_SCALE, w_qkv[:, DA_W:_QB],
             _permute_sw_heads(w_qkv[:, _QB:_QB + SW_QW], 1) * Q_SCALE, w_qkv[:, _QB + SW_QW:_QC * LANES],
             w_qkv[:, _QC * LANES:_KC * LANES] * Q_SCALE, w_qkv[:, _KC * LANES:]], axis=1).astype(BF16)
        qkv = _in_proj(h, w_qkv)
        gates = _in_proj(h, w_in[l, :, QKV_W:].astype(BF16), b_gate[l])

        lam_init = 0.8 - 0.6 * math.exp(-0.3 * l)
        lam_vecs = jnp.stack([lam_q1[l], lam_k1[l], lam_q2[l], lam_k2[l]]).astype(F32)
        ya = _diff_attention(lam_vecs, diff_subln_g[l], qkv, tiles, batch, seq, lam_init)
        sink_row = jnp.repeat(sinks[l].astype(F32)[perm] * LOG2E, SW_T).reshape(1, SW_HEADS * SW_T)
        yb = _swa_attention(sink_row, qkv, win, batch, seq)
        yc = _moba_attention(qkv, tiles, batch, seq)

        x2, h = _mix_out(ya, yb, yc, gates, x2, w_oa[l].astype(BF16),
                         _permute_sw_heads(w_ob[l], 0).astype(BF16), w_oc[l].astype(BF16),
                         w_out[l].astype(BF16), post_mix_g[l], pre_ffn_g[l])

        a = _ffn_up(h, w_up[l].astype(BF16), conv_w[l], conv_b[l], seq)
        next_g = pre_mix_g[l + 1] if l + 1 < depth else None
        x2, h = _ffn_down(a, w_down[l].astype(BF16), x2, post_ffn_g[l], next_g)
    return x2.reshape(batch, seq, d)


def kernel(x, rel_bias_table, w_in, b_gate, lam_q1, lam_k1, lam_q2, lam_k2, diff_subln_g, sinks, w_oa, w_ob, w_oc, w_out, pre_mix_g, post_mix_g, pre_ffn_g, post_ffn_g, w_up, conv_w, conv_b, w_down):
    return _trunk(x, rel_bias_table, w_in, b_gate, lam_q1, lam_k1, lam_q2, lam_k2, diff_subln_g, sinks, w_oa, w_ob,
                  w_oc, w_out, pre_mix_g, post_mix_g, pre_ffn_g, post_ffn_g, w_up, conv_w, conv_b, w_down)
```

```python
import functools
import math

import jax
import jax.numpy as jnp
from jax import lax
from jax.experimental import pallas as pl
from jax.experimental.pallas import tpu as pltpu

DA_HEADS = 4
DA_HD = 64
DA_W = DA_HEADS * 2 * DA_HD
SW_HEADS = 8
SW_KV = 2
SW_HD = 64
WINDOW = 128
SW_QW = SW_HEADS * SW_HD
SW_KW = SW_KV * SW_HD
MB_HEADS = 8
MB_HD = 64
MB_W = MB_HEADS * MB_HD
MB_BLOCK = 256
MB_TOPK = 3
N_BUCKETS = 32
MAX_DIST = 128
N_ATT_HEADS = DA_HEADS + SW_HEADS + MB_HEADS
N_BRANCH = 3
QKV_W = 3 * DA_W + SW_QW + 2 * SW_KW + 3 * MB_W
EPS = 1e-6

LANES = 128
MXU_N = 256
ATT_T = 256
SW_T = WINDOW
NEG = -1e30
LOG2E = math.log2(math.e)
Q_SCALE = DA_HD ** -0.5 * LOG2E
VMEM_LIMIT = 56 * 1024 * 1024

_QA, _KA, _VA = 0, DA_W // LANES, 2 * DA_W // LANES
_QB = 3 * DA_W
_KB = (_QB + SW_QW) // LANES
_VB = _KB + 1
_QC = (_QB + SW_QW + 2 * SW_KW) // LANES
_KC = _QC + MB_W // LANES
_VC = _KC + MB_W // LANES
_SW_PERM = tuple(h // 2 + (h % 2) * (SW_HEADS // SW_KV) for h in range(SW_HEADS))

F32 = jnp.float32
BF16 = jnp.bfloat16


def _pick(n, candidates):
    for c in candidates:
        if n % c == 0:
            return c
    raise ValueError(f"no tile in {candidates} divides {n}")


def _params(*sem):
    return pltpu.CompilerParams(dimension_semantics=sem, vmem_limit_bytes=VMEM_LIMIT)


def _rms(x, g):
    return x * lax.rsqrt(jnp.mean(x * x, axis=-1, keepdims=True) + EPS) * g


def _dot_t(a, b, **kw):
    return lax.dot_general(a, b, (((1,), (1,)), ((), ())), preferred_element_type=F32, **kw)


def _rel_bucket(dist):
    n = jnp.maximum(dist, 0)
    max_exact = N_BUCKETS // 2
    nf = jnp.maximum(n, 1).astype(F32)
    large = max_exact + (jnp.log(nf / max_exact) / math.log(MAX_DIST / max_exact)
                         * (N_BUCKETS - max_exact)).astype(jnp.int32)
    large = jnp.minimum(large, N_BUCKETS - 1)
    return jnp.where(n < max_exact, n, large)


def _bias_lookup(tab_ref, head, dist):
    bucket = _rel_bucket(dist)
    acc = jnp.zeros(dist.shape, F32)
    for b in range(N_BUCKETS):
        acc = jnp.where(bucket == b, tab_ref[b * N_ATT_HEADS + head], acc)
    return acc * LOG2E


FAR, NEAR, DIAG = 0, 1, 2


def _tile_kind(kj, qi):
    return jnp.clip(kj - qi + DIAG, FAR, DIAG)


def _bias_kernel(tab_ref, o_ref):
    e = pl.program_id(0)
    moba0 = DA_HEADS + SW_HEADS + 2 * (e - DA_HEADS)
    heads = (jnp.where(e < DA_HEADS, e, moba0), jnp.where(e < DA_HEADS, e, moba0 + 1))
    d = (lax.broadcasted_iota(jnp.int32, (ATT_T, ATT_T), 1)
         - lax.broadcasted_iota(jnp.int32, (ATT_T, ATT_T), 0))
    for c, head in enumerate(heads):
        cols = slice(c * ATT_T, (c + 1) * ATT_T)
        o_ref[0, FAR, :, cols] = _bias_lookup(tab_ref, head, d + 2 * ATT_T)
        o_ref[0, NEAR, :, cols] = _bias_lookup(tab_ref, head, d + ATT_T)
        o_ref[0, DIAG, :, cols] = jnp.where(d >= 0, _bias_lookup(tab_ref, head, d), NEG)


def _window_kernel(tab_ref, win_ref):
    c = pl.program_id(0)
    head = DA_HEADS + c // 2 + (c % 2) * (SW_HEADS // SW_KV)
    d = (lax.broadcasted_iota(jnp.int32, (2 * SW_T, SW_T), 1) + SW_T
         - lax.broadcasted_iota(jnp.int32, (2 * SW_T, SW_T), 0))
    win_ref[...] = jnp.where((d >= 0) & (d < WINDOW), _bias_lookup(tab_ref, head, d), NEG)


def _bias_tiles(tab_flat):
    assert ATT_T + 1 >= MAX_DIST
    n = DA_HEADS + MB_HEADS // 2
    tiles = pl.pallas_call(
        _bias_kernel,
        grid=(n,),
        in_specs=[pl.BlockSpec(memory_space=pltpu.SMEM)],
        out_specs=pl.BlockSpec((1, 3, ATT_T, 2 * ATT_T), lambda e: (e, 0, 0, 0)),
        out_shape=jax.ShapeDtypeStruct((n, 3, ATT_T, 2 * ATT_T), F32),
        compiler_params=_params("arbitrary"),
        name="bias_tiles",
    )(tab_flat)
    win = pl.pallas_call(
        _window_kernel,
        grid=(SW_HEADS,),
        in_specs=[pl.BlockSpec(memory_space=pltpu.SMEM)],
        out_specs=pl.BlockSpec((2 * SW_T, SW_T), lambda c: (0, c)),
        out_shape=jax.ShapeDtypeStruct((2 * SW_T, SW_HEADS * SW_T), F32),
        compiler_params=_params("arbitrary"),
        name="window_tiles",
    )(tab_flat)
    return tiles, win


def _norm_kernel(x_ref, g_ref, o_ref):
    o_ref[...] = _rms(x_ref[...], g_ref[...]).astype(o_ref.dtype)


def _prenorm(x2, g):
    m, d = x2.shape
    tm = _pick(m, (512, 256, 128))
    return pl.pallas_call(
        _norm_kernel,
        grid=(m // tm,),
        in_specs=[pl.BlockSpec((tm, d), lambda i: (i, 0)), pl.BlockSpec((1, d), lambda i: (0, 0))],
        out_specs=pl.BlockSpec((tm, d), lambda i: (i, 0)),
        out_shape=jax.ShapeDtypeStruct((m, d), BF16),
        compiler_params=_params("arbitrary"),
        name="prenorm",
    )(x2, g.reshape(1, d))


def _proj_kernel(h_ref, w_ref, o_ref):
    o_ref[...] = jnp.dot(h_ref[...], w_ref[...], preferred_element_type=F32).astype(o_ref.dtype)


def _gate_kernel(h_ref, w_ref, b_ref, o_ref):
    acc = jnp.dot(h_ref[...], w_ref[...], preferred_element_type=F32)
    o_ref[...] = jax.nn.sigmoid(acc + b_ref[...]).astype(o_ref.dtype)


def _in_proj(h, w, bias=None):
    m, d = h.shape
    n = w.shape[1]
    tm = _pick(m, (1024, 512, 256, 128))
    tn = _pick(n, (1280, 1024, 768, 512, 256, 128))
    in_specs = [pl.BlockSpec((tm, d), lambda i, j: (i, 0)), pl.BlockSpec((d, tn), lambda i, j: (0, j))]
    args = [h, w]
    if bias is not None:
        in_specs.append(pl.BlockSpec((1, tn), lambda i, j: (0, j)))
        args.append(bias.reshape(1, n))
    return pl.pallas_call(
        _proj_kernel if bias is None else _gate_kernel,
        grid=(m // tm, n // tn),
        in_specs=in_specs,
        out_specs=pl.BlockSpec((tm, tn), lambda i, j: (i, j)),
        out_shape=jax.ShapeDtypeStruct((m, n), BF16),
        compiler_params=_params("arbitrary", "arbitrary"),
        name="in_proj" if bias is None else "in_gates",
    )(*args)


def _split_heads(q):
    lane = lax.broadcasted_iota(jnp.int32, q.shape, 1)
    zero = jnp.zeros_like(q)
    return jnp.concatenate([jnp.where(lane < LANES // 2, q, zero),
                            jnp.where(lane >= LANES // 2, q, zero)], axis=0)


def _merge_heads_t(o):
    t = o.shape[1] // 2
    feat = lax.broadcasted_iota(jnp.int32, (LANES, t), 0)
    return jnp.where(feat < LANES // 2, o[:, :t], o[:, t:]).T


def _pv_t(v, p):
    return lax.dot_general(v, p, (((0,), (0,)), ((), ())), preferred_element_type=F32)


def _softmax_stats(s, m, l, keep=None):
    m_tile = jnp.max(s, axis=0, keepdims=True)
    if keep is not None:
        m_tile = jnp.where(keep, m_tile, NEG)
    m_new = jnp.maximum(m, m_tile)
    alpha = jnp.exp2(m - m_new)
    p = jnp.exp2(s - (m_new if keep is None else jnp.where(keep, m_new, -NEG)))
    l = alpha * l + jnp.sum(p, axis=0, keepdims=True)
    return m_new, l, p.astype(BF16), alpha


def _flash(qi, logits_fn, v_ref, scratch, keep_fn=None):
    s_s, p_s, acc_s = scratch
    keep = (lambda kj: None) if keep_fn is None else keep_fn

    def stats(kj, m, l):
        m, l, p, alpha = _softmax_stats(s_s[...], m, l, keep(kj))
        p_s[...] = p
        return m, l, alpha

    n_queries = s_s.shape[1]
    s_s[...] = logits_fn(0)
    carry = stats(0, jnp.full((1, n_queries), -jnp.inf, F32), jnp.zeros((1, n_queries), F32))
    s_s[...] = logits_fn(jnp.minimum(1, qi))
    acc_s[...] = jnp.zeros(acc_s.shape, F32)

    def body(kj, carry):
        m, l, alpha_prev = carry
        acc_s[...] = alpha_prev * acc_s[...] + _pv_t(_kv_block(v_ref, kj - 1), p_s[...])
        carry = stats(kj, m, l)
        s_s[...] = logits_fn(jnp.minimum(kj + 1, qi))
        return carry

    _, l, alpha = lax.fori_loop(1, qi + 1, body, carry)
    return (alpha * acc_s[...] + _pv_t(_kv_block(v_ref, qi), p_s[...])) / l


def _flash_scratch(n_queries):
    return [pltpu.VMEM((ATT_T, n_queries), F32), pltpu.VMEM((ATT_T, n_queries), BF16),
            pltpu.VMEM((LANES, n_queries), F32)]


def _kv_block(ref, kj):
    return ref[pl.ds(pl.multiple_of(kj * ATT_T, ATT_T), ATT_T), :]


def _diff_attn_kernel(lam_ref, g_ref, q_ref, k_ref, v_ref, bias_ref, o_ref, *scratch, lam_init):
    qi = pl.program_id(2)
    t = ATT_T
    q2 = _split_heads(q_ref[...])

    def logits(kj):
        return _dot_t(_kv_block(k_ref, kj), q2) + bias_ref[0, _tile_kind(kj, qi)]

    o = _flash(qi, logits, v_ref, scratch)
    lv = lam_ref[...]
    lam = (jnp.exp(jnp.sum(lv[0:1] * lv[1:2], axis=1, keepdims=True))
           - jnp.exp(jnp.sum(lv[2:3] * lv[3:4], axis=1, keepdims=True)) + lam_init)
    o = o[:, :t] - lam * o[:, t:]
    o = o * lax.rsqrt(jnp.mean(o * o, axis=0, keepdims=True) + EPS) * (g_ref[...] * (1.0 - lam_init))
    o_ref[...] = o.T.astype(o_ref.dtype)


def _diff_attention(lam_vecs, subln_g, qkv, tiles, batch, seq, lam_init):
    m = batch * seq
    nq = seq // ATT_T
    return pl.pallas_call(
        functools.partial(_diff_attn_kernel, lam_init=lam_init),
        grid=(batch, DA_HEADS, nq),
        in_specs=[pl.BlockSpec((4, DA_HD), lambda b, h, i: (0, 0)),
                  pl.BlockSpec((2 * DA_HD, 1), lambda b, h, i: (0, 0)),
                  pl.BlockSpec((ATT_T, LANES), lambda b, h, i: (b * nq + i, _QA + h)),
                  pl.BlockSpec((seq, LANES), lambda b, h, i: (b, _KA + h)),
                  pl.BlockSpec((seq, LANES), lambda b, h, i: (b, _VA + h)),
                  pl.BlockSpec((1, 3, ATT_T, 2 * ATT_T), lambda b, h, i: (h, 0, 0, 0))],
        out_specs=pl.BlockSpec((ATT_T, LANES), lambda b, h, i: (b * nq + i, h)),
        out_shape=jax.ShapeDtypeStruct((m, DA_W), BF16),
        scratch_shapes=_flash_scratch(2 * ATT_T),
        compiler_params=_params("arbitrary", "arbitrary", "arbitrary"),
        name="diff_attention",
    )(lam_vecs, subln_g.astype(F32).reshape(2 * DA_HD, 1), qkv, qkv, qkv, tiles)


def _swa_kernel(sink_ref, q_ref, kp_ref, ko_ref, vp_ref, vo_ref, bias_ref, o_ref):
    qi = pl.program_id(1)
    t = SW_T
    groups = SW_QW // LANES
    q = q_ref[...]
    qs = jnp.concatenate([_split_heads(q[:, g * LANES:(g + 1) * LANES]) for g in range(groups)], axis=0)
    no_prev = jnp.where(qi > 0, 0.0, NEG)
    s_prev = _dot_t(kp_ref[...], qs) + bias_ref[0:t, :] + no_prev
    s_own = _dot_t(ko_ref[...], qs) + bias_ref[t:2 * t, :]
    sink = sink_ref[...]
    m = jnp.maximum(jnp.maximum(jnp.max(s_prev, axis=0, keepdims=True),
                                jnp.max(s_own, axis=0, keepdims=True)), sink)
    p_prev = jnp.exp2(s_prev - m)
    p_own = jnp.exp2(s_own - m)
    l = (jnp.sum(p_prev, axis=0, keepdims=True) + jnp.sum(p_own, axis=0, keepdims=True)
         + jnp.exp2(sink - m))
    o = (_pv_t(vp_ref[...], p_prev) + _pv_t(vo_ref[...], p_own)) / l
    for g in range(groups):
        o_ref[:, g * LANES:(g + 1) * LANES] = _merge_heads_t(o[:, 2 * g * t:(2 * g + 2) * t]).astype(o_ref.dtype)


def _swa_attention(sink_row, qkv, win, batch, seq):
    m = batch * seq
    nq = seq // SW_T
    cols = SW_HEADS * SW_T
    prev = lambda b, i: (b * nq + jnp.maximum(i - 1, 0), _KB)
    own = lambda b, i: (b * nq + i, _KB)
    prev_v = lambda b, i: (b * nq + jnp.maximum(i - 1, 0), _VB)
    own_v = lambda b, i: (b * nq + i, _VB)
    return pl.pallas_call(
        _swa_kernel,
        grid=(batch, nq),
        in_specs=[pl.BlockSpec((1, cols), lambda b, i: (0, 0)),
                  pl.BlockSpec((SW_T, SW_QW), lambda b, i: (b * nq + i, _QB // SW_QW)),
                  pl.BlockSpec((SW_T, LANES), prev),
                  pl.BlockSpec((SW_T, LANES), own),
                  pl.BlockSpec((SW_T, LANES), prev_v),
                  pl.BlockSpec((SW_T, LANES), own_v),
                  pl.BlockSpec((2 * SW_T, cols), lambda b, i: (0, 0))],
        out_specs=pl.BlockSpec((SW_T, SW_QW), lambda b, i: (b * nq + i, 0)),
        out_shape=jax.ShapeDtypeStruct((m, SW_QW), BF16),
        compiler_params=_params("arbitrary", "arbitrary"),
        name="swa_attention",
    )(sink_row, qkv, qkv, qkv, qkv, qkv, win)


def _moba_kernel(q_ref, k_ref, v_ref, bias_ref, o_ref, kmean_s, *scratch, n_blocks):
    qi = pl.program_id(2)
    q2 = _split_heads(q_ref[...])

    @pl.when(qi == 0)
    def _():
        kf = k_ref[...].astype(F32).reshape(n_blocks, MB_BLOCK, LANES)
        kmean_s[...] = jnp.sum(kf, axis=1) * (1.0 / MB_BLOCK)

    gate = _dot_t(kmean_s[...], q2.astype(F32), precision=lax.Precision.HIGHEST)
    blk = lax.broadcasted_iota(jnp.int32, gate.shape, 0)

    def selected(kj):
        g_kj = jnp.sum(jnp.where(blk == kj, gate, 0.0), axis=0, keepdims=True)
        beats = ((gate > g_kj) | ((gate == g_kj) & (blk < kj))) & (blk < qi)
        rank = jnp.sum(beats.astype(F32), axis=0, keepdims=True)
        return (rank < MB_TOPK) | (kj >= qi)

    def logits(kj):
        return _dot_t(_kv_block(k_ref, kj), q2) + bias_ref[0, _tile_kind(kj, qi)]

    o_ref[...] = _merge_heads_t(_flash(qi, logits, v_ref, scratch, selected)).astype(o_ref.dtype)


def _moba_attention(qkv, tiles, batch, seq):
    m = batch * seq
    nq = seq // ATT_T
    n_blocks = seq // MB_BLOCK
    return pl.pallas_call(
        functools.partial(_moba_kernel, n_blocks=n_blocks),
        grid=(batch, MB_HEADS // 2, nq),
        in_specs=[pl.BlockSpec((ATT_T, LANES), lambda b, h, i: (b * nq + i, _QC + h)),
                  pl.BlockSpec((seq, LANES), lambda b, h, i: (b, _KC + h)),
                  pl.BlockSpec((seq, LANES), lambda b, h, i: (b, _VC + h)),
                  pl.BlockSpec((1, 3, ATT_T, 2 * ATT_T), lambda b, h, i: (DA_HEADS + h, 0, 0, 0))],
        out_specs=pl.BlockSpec((ATT_T, LANES), lambda b, h, i: (b * nq + i, h)),
        out_shape=jax.ShapeDtypeStruct((m, MB_W), BF16),
        scratch_shapes=[pltpu.VMEM((n_blocks, LANES), F32)] + _flash_scratch(2 * ATT_T),
        compiler_params=_params("arbitrary", "arbitrary", "arbitrary"),
        name="moba_attention",
    )(qkv, qkv, qkv, tiles)


def _mix_out_kernel(ya_ref, yb_ref, yc_ref, ga_ref, gb_ref, gc_ref, x_ref, woa_ref, wob_ref, woc_ref,
                    wout_ref, pg_ref, ng_ref, xo_ref, ho_ref):
    def branch(y_ref, w_ref, g_ref, rows):
        return g_ref[rows, :].astype(F32) * jnp.dot(y_ref[rows, :], w_ref[...], preferred_element_type=F32)

    tm = x_ref.shape[0]
    sub = MXU_N if tm % MXU_N == 0 else tm
    for r0 in range(0, tm, sub):
        rows = slice(r0, r0 + sub)
        mix = (branch(ya_ref, woa_ref, ga_ref, rows) + branch(yb_ref, wob_ref, gb_ref, rows)
               + branch(yc_ref, woc_ref, gc_ref, rows))
        z = jnp.dot(mix.astype(BF16), wout_ref[...], preferred_element_type=F32)
        xn = x_ref[rows, :] + _rms(z, pg_ref[...])
        xo_ref[rows, :] = xn
        ho_ref[rows, :] = _rms(xn, ng_ref[...]).astype(ho_ref.dtype)


def _mix_out(ya, yb, yc, gates, x2, woa, wob, woc, wout, post_g, next_g):
    m, d = x2.shape
    tm = _pick(m, (512, 256, 128))
    row = lambda i: (i, 0)
    const = lambda i: (0, 0)
    once = pl.Buffered(1)
    return pl.pallas_call(
        _mix_out_kernel,
        grid=(m // tm,),
        in_specs=[pl.BlockSpec((tm, DA_W), row), pl.BlockSpec((tm, SW_QW), row), pl.BlockSpec((tm, MB_W), row),
                  pl.BlockSpec((tm, d), lambda i: (i, 0)), pl.BlockSpec((tm, d), lambda i: (i, 1)),
                  pl.BlockSpec((tm, d), lambda i: (i, 2)),
                  pl.BlockSpec((tm, d), row),
                  pl.BlockSpec((DA_W, d), const, pipeline_mode=once),
                  pl.BlockSpec((SW_QW, d), const, pipeline_mode=once),
                  pl.BlockSpec((MB_W, d), const, pipeline_mode=once),
                  pl.BlockSpec((d, d), const, pipeline_mode=once),
                  pl.BlockSpec((1, d), const), pl.BlockSpec((1, d), const)],
        out_specs=[pl.BlockSpec((tm, d), row), pl.BlockSpec((tm, d), row)],
        out_shape=[jax.ShapeDtypeStruct((m, d), F32), jax.ShapeDtypeStruct((m, d), BF16)],
        compiler_params=_params("arbitrary"),
        name="mix_out",
    )(ya, yb, yc, gates, gates, gates, x2, woa, wob, woc, wout, post_g.reshape(1, d), next_g.reshape(1, d))


CONV_W = 3
HALO = 8


def _ffn_up_kernel(h_ref, wg_ref, wv_ref, cwg_ref, cwv_ref, cbg_ref, cbv_ref, o_ref, ug_s, uv_s, *, tiles_per_seq):
    i = pl.program_id(1)
    tm = h_ref.shape[0]

    @pl.when(i % tiles_per_seq == 0)
    def _():
        ug_s[0:HALO, :] = jnp.zeros((HALO, ug_s.shape[1]), F32)
        uv_s[0:HALO, :] = jnp.zeros((HALO, uv_s.shape[1]), F32)

    h = h_ref[...]

    def conv(w_ref, cw_ref, cb_ref, u_s):
        u = jnp.dot(h, w_ref[...], preferred_element_type=F32)
        u_s[HALO:HALO + tm, :] = u
        cw = cw_ref[...]
        c = cb_ref[...] + u_s[HALO - 2:HALO - 2 + tm, :] * cw[0:1]
        c = c + u_s[HALO - 1:HALO - 1 + tm, :] * cw[1:2]
        c = c + u * cw[2:3]
        u_s[0:HALO, :] = u[tm - HALO:tm]
        return c

    gate = conv(wg_ref, cwg_ref, cbg_ref, ug_s)
    val = conv(wv_ref, cwv_ref, cbv_ref, uv_s)
    o_ref[...] = (jax.nn.gelu(gate, approximate=True) * val).astype(o_ref.dtype)


def _ffn_up(h, w_up, conv_w, conv_b, seq):
    m, d = h.shape
    f = w_up.shape[1] // 2
    tm = _pick(seq, (1024, 512, 256, 128))
    tn = _pick(f, (512, 384, 256, 128))
    nj = f // tn
    gate_col = lambda j, i: (0, j)
    val_col = lambda j, i: (0, nj + j)
    return pl.pallas_call(
        functools.partial(_ffn_up_kernel, tiles_per_seq=seq // tm),
        grid=(nj, m // tm),
        in_specs=[pl.BlockSpec((tm, d), lambda j, i: (i, 0)),
                  pl.BlockSpec((d, tn), gate_col), pl.BlockSpec((d, tn), val_col),
                  pl.BlockSpec((CONV_W, tn), gate_col), pl.BlockSpec((CONV_W, tn), val_col),
                  pl.BlockSpec((1, tn), gate_col), pl.BlockSpec((1, tn), val_col)],
        out_specs=pl.BlockSpec((tm, tn), lambda j, i: (i, j)),
        out_shape=jax.ShapeDtypeStruct((m, f), BF16),
        scratch_shapes=[pltpu.VMEM((HALO + tm, tn), F32), pltpu.VMEM((HALO + tm, tn), F32)],
        compiler_params=_params("arbitrary", "arbitrary"),
        name="ffn_up",
    )(h, w_up, w_up, conv_w, conv_w, conv_b.reshape(1, 2 * f), conv_b.reshape(1, 2 * f))


def _ffn_down_kernel(a_ref, w_ref, x_ref, pg_ref, ng_ref, xo_ref, *rest, emit_next):
    acc_s = rest[-1]
    k = pl.program_id(1)
    last = pl.num_programs(1) - 1

    @pl.when(k == 0)
    def _():
        acc_s[...] = jnp.zeros(acc_s.shape, F32)

    @pl.when(k < last)
    def _():
        acc_s[...] += jnp.dot(a_ref[...], w_ref[...], preferred_element_type=F32)

    @pl.when(k == last)
    def _():
        tm = x_ref.shape[0]
        sub = MXU_N if tm % MXU_N == 0 else tm
        for r0 in range(0, tm, sub):
            rows = slice(r0, r0 + sub)
            z = acc_s[rows, :] + jnp.dot(a_ref[rows, :], w_ref[...], preferred_element_type=F32)
            xn = x_ref[rows, :] + _rms(z, pg_ref[...])
            xo_ref[rows, :] = xn
            if emit_next:
                rest[0][rows, :] = _rms(xn, ng_ref[...]).astype(rest[0].dtype)


def _ffn_down(a, w_down, x2, post_g, next_g):
    m, d = x2.shape
    f = a.shape[1]
    tm = _pick(m, (512, 256, 128))
    tk = _pick(f, (1408, 1024, 512, 384, 256, 128))
    emit_next = next_g is not None
    row = lambda i, k: (i, 0)
    const = lambda i, k: (0, 0)
    out_specs = [pl.BlockSpec((tm, d), row)]
    out_shape = [jax.ShapeDtypeStruct((m, d), F32)]
    if emit_next:
        out_specs.append(pl.BlockSpec((tm, d), row))
        out_shape.append(jax.ShapeDtypeStruct((m, d), BF16))
    ng = next_g if emit_next else post_g
    res = pl.pallas_call(
        functools.partial(_ffn_down_kernel, emit_next=emit_next),
        grid=(m // tm, f // tk),
        in_specs=[pl.BlockSpec((tm, tk), lambda i, k: (i, k)), pl.BlockSpec((tk, d), lambda i, k: (k, 0)),
                  pl.BlockSpec((tm, d), row), pl.BlockSpec((1, d), const), pl.BlockSpec((1, d), const)],
        out_specs=out_specs,
        out_shape=out_shape,
        scratch_shapes=[pltpu.VMEM((tm, d), F32)],
        compiler_params=_params("arbitrary", "arbitrary"),
        name="ffn_down",
    )(a, w_down, x2, post_g.reshape(1, d), ng.reshape(1, d))
    return (res[0], res[1]) if emit_next else (res[0], None)


def _permute_sw_heads(w, axis):
    shape = w.shape
    split = shape[:axis] + (SW_HEADS, SW_HD) + shape[axis + 1:]
    return jnp.take(w.reshape(split), jnp.array(_SW_PERM), axis=axis).reshape(shape)


@jax.jit
def _trunk(x, rel_bias_table, w_in, b_gate, lam_q1, lam_k1, lam_q2, lam_k2, diff_subln_g, sinks, w_oa, w_ob,
           w_oc, w_out, pre_mix_g, post_mix_g, pre_ffn_g, post_ffn_g, w_up, conv_w, conv_b, w_down):
    batch, seq, d = x.shape
    depth = w_in.shape[0]
    assert seq % ATT_T == 0 and seq % MB_BLOCK == 0 and d % LANES == 0
    assert w_in.shape[2] == QKV_W + N_BRANCH * d
    m = batch * seq

    tab_flat = rel_bias_table.astype(F32).reshape(-1)
    tiles, win = _bias_tiles(tab_flat)
    perm = jnp.array(_SW_PERM)

    x2 = x.reshape(m, d)
    h = _prenorm(x2, pre_mix_g[0])
    for l in range(depth):
        w_qkv = w_in[l, :, :QKV_W]
        w_qkv = jnp.concatenate(
            [w_qkv[:, :DA_W] * Q_SCALE, w_qkv[:, DA_W:_QB],
             _permute_sw_heads(w_qkv[:, _QB:_QB + SW_QW], 1) * Q_SCALE, w_qkv[:, _QB + SW_QW:_QC * LANES],
             w_qkv[:, _QC * LANES:_KC * LANES] * Q_SCALE, w_qkv[:, _KC * LANES:]], axis=1).astype(BF16)
        qkv = _in_proj(h, w_qkv)
        gates = _in_proj(h, w_in[l, :, QKV_W:].astype(BF16), b_gate[l])

        lam_init = 0.8 - 0.6 * math.exp(-0.3 * l)
        lam_vecs = jnp.stack([lam_q1[l], lam_k1[l], lam_q2[l], lam_k2[l]]).astype(F32)
        ya = _diff_attention(lam_vecs, diff_subln_g[l], qkv, tiles, batch, seq, lam_init)
        sink_row = jnp.repeat(sinks[l].astype(F32)[perm] * LOG2E, SW_T).reshape(1, SW_HEADS * SW_T)
        yb = _swa_attention(sink_row, qkv, win, batch, seq)
        yc = _moba_attention(qkv, tiles, batch, seq)

        x2, h = _mix_out(ya, yb, yc, gates, x2, w_oa[l].astype(BF16),
                         _permute_sw_heads(w_ob[l], 0).astype(BF16), w_oc[l].astype(BF16),
                         w_out[l].astype(BF16), post_mix_g[l], pre_ffn_g[l])

        a = _ffn_up(h, w_up[l].astype(BF16), conv_w[l], conv_b[l], seq)
        next_g = pre_mix_g[l + 1] if l + 1 < depth else None
        x2, h = _ffn_down(a, w_down[l].astype(BF16), x2, post_ffn_g[l], next_g)
    return x2.reshape(batch, seq, d)


def kernel(x, rel_bias_table, w_in, b_gate, lam_q1, lam_k1, lam_q2, lam_k2, diff_subln_g, sinks, w_oa, w_ob, w_oc, w_out, pre_mix_g, post_mix_g, pre_ffn_g, post_ffn_g, w_up, conv_w, conv_b, w_down):
    return _trunk(x, rel_bias_table, w_in, b_gate, lam_q1, lam_k1, lam_q2, lam_k2, diff_subln_g, sinks, w_oa, w_ob,
                  w_oc, w_out, pre_mix_g, post_mix_g, pre_ffn_g, post_ffn_g, w_up, conv_w, conv_b, w_down)
```

```python
import functools
import math

import jax
import jax.numpy as jnp
from jax import lax
from jax.experimental import pallas as pl
from jax.experimental.pallas import tpu as pltpu

DA_HEADS = 4
DA_HD = 64
DA_W = DA_HEADS * 2 * DA_HD
SW_HEADS = 8
SW_KV = 2
SW_HD = 64
WINDOW = 128
SW_QW = SW_HEADS * SW_HD
SW_KW = SW_KV * SW_HD
MB_HEADS = 8
MB_HD = 64
MB_W = MB_HEADS * MB_HD
MB_BLOCK = 256
MB_TOPK = 3
N_BUCKETS = 32
MAX_DIST = 128
N_ATT_HEADS = DA_HEADS + SW_HEADS + MB_HEADS
N_BRANCH = 3
QKV_W = 3 * DA_W + SW_QW + 2 * SW_KW + 3 * MB_W
EPS = 1e-6

LANES = 128
MXU_N = 256
ATT_T = 256
SW_T = WINDOW
NEG = -1e30
LOG2E = math.log2(math.e)
Q_SCALE = DA_HD ** -0.5 * LOG2E
VMEM_LIMIT = 56 * 1024 * 1024

_QA, _KA, _VA = 0, DA_W // LANES, 2 * DA_W // LANES
_QB = 3 * DA_W
_KB = (_QB + SW_QW) // LANES
_VB = _KB + 1
_QC = (_QB + SW_QW + 2 * SW_KW) // LANES
_KC = _QC + MB_W // LANES
_VC = _KC + MB_W // LANES
_SW_PERM = tuple(h // 2 + (h % 2) * (SW_HEADS // SW_KV) for h in range(SW_HEADS))

F32 = jnp.float32
BF16 = jnp.bfloat16


def _pick(n, candidates):
    for c in candidates:
        if n % c == 0:
            return c
    raise ValueError(f"no tile in {candidates} divides {n}")


def _params(*sem):
    return pltpu.CompilerParams(dimension_semantics=sem, vmem_limit_bytes=VMEM_LIMIT)


def _rms(x, g):
    return x * lax.rsqrt(jnp.mean(x * x, axis=-1, keepdims=True) + EPS) * g


def _dot_t(a, b, **kw):
    return lax.dot_general(a, b, (((1,), (1,)), ((), ())), preferred_element_type=F32, **kw)


def _rel_bucket(dist):
    n = jnp.maximum(dist, 0)
    max_exact = N_BUCKETS // 2
    nf = jnp.maximum(n, 1).astype(F32)
    large = max_exact + (jnp.log(nf / max_exact) / math.log(MAX_DIST / max_exact)
                         * (N_BUCKETS - max_exact)).astype(jnp.int32)
    large = jnp.minimum(large, N_BUCKETS - 1)
    return jnp.where(n < max_exact, n, large)


def _bias_lookup(tab_ref, head, dist):
    bucket = _rel_bucket(dist)
    acc = jnp.zeros(dist.shape, F32)
    for b in range(N_BUCKETS):
        acc = jnp.where(bucket == b, tab_ref[b * N_ATT_HEADS + head], acc)
    return acc * LOG2E


FAR, NEAR, DIAG = 0, 1, 2


def _tile_kind(kj, qi):
    return jnp.clip(kj - qi + DIAG, FAR, DIAG)


def _bias_kernel(tab_ref, o_ref):
    e = pl.program_id(0)
    moba0 = DA_HEADS + SW_HEADS + 2 * (e - DA_HEADS)
    heads = (jnp.where(e < DA_HEADS, e, moba0), jnp.where(e < DA_HEADS, e, moba0 + 1))
    d = (lax.broadcasted_iota(jnp.int32, (ATT_T, ATT_T), 1)
         - lax.broadcasted_iota(jnp.int32, (ATT_T, ATT_T), 0))
    for c, head in enumerate(heads):
        cols = slice(c * ATT_T, (c + 1) * ATT_T)
        o_ref[0, FAR, :, cols] = _bias_lookup(tab_ref, head, d + 2 * ATT_T)
        o_ref[0, NEAR, :, cols] = _bias_lookup(tab_ref, head, d + ATT_T)
        o_ref[0, DIAG, :, cols] = jnp.where(d >= 0, _bias_lookup(tab_ref, head, d), NEG)


def _window_kernel(tab_ref, win_ref):
    c = pl.program_id(0)
    head = DA_HEADS + c // 2 + (c % 2) * (SW_HEADS // SW_KV)
    d = (lax.broadcasted_iota(jnp.int32, (2 * SW_T, SW_T), 1) + SW_T
         - lax.broadcasted_iota(jnp.int32, (2 * SW_T, SW_T), 0))
    win_ref[...] = jnp.where((d >= 0) & (d < WINDOW), _bias_lookup(tab_ref, head, d), NEG)


def _bias_tiles(tab_flat):
    assert ATT_T + 1 >= MAX_DIST
    n = DA_HEADS + MB_HEADS // 2
    tiles = pl.pallas_call(
        _bias_kernel,
        grid=(n,),
        in_specs=[pl.BlockSpec(memory_space=pltpu.SMEM)],
        out_specs=pl.BlockSpec((1, 3, ATT_T, 2 * ATT_T), lambda e: (e, 0, 0, 0)),
        out_shape=jax.ShapeDtypeStruct((n, 3, ATT_T, 2 * ATT_T), F32),
        compiler_params=_params("arbitrary"),
        name="bias_tiles",
    )(tab_flat)
    win = pl.pallas_call(
        _window_kernel,
        grid=(SW_HEADS,),
        in_specs=[pl.BlockSpec(memory_space=pltpu.SMEM)],
        out_specs=pl.BlockSpec((2 * SW_T, SW_T), lambda c: (0, c)),
        out_shape=jax.ShapeDtypeStruct((2 * SW_T, SW_HEADS * SW_T), F32),
        compiler_params=_params("arbitrary"),
        name="window_tiles",
    )(tab_flat)
    return tiles, win


def _norm_kernel(x_ref, g_ref, o_ref):
    o_ref[...] = _rms(x_ref[...], g_ref[...]).astype(o_ref.dtype)


def _prenorm(x2, g):
    m, d = x2.shape
    tm = _pick(m, (512, 256, 128))
    return pl.pallas_call(
        _norm_kernel,
        grid=(m // tm,),
        in_specs=[pl.BlockSpec((tm, d), lambda i: (i, 0)), pl.BlockSpec((1, d), lambda i: (0, 0))],
        out_specs=pl.BlockSpec((tm, d), lambda i: (i, 0)),
        out_shape=jax.ShapeDtypeStruct((m, d), BF16),
        compiler_params=_params("arbitrary"),
        name="prenorm",
    )(x2, g.reshape(1, d))


def _proj_kernel(h_ref, w_ref, o_ref):
    o_ref[...] = jnp.dot(h_ref[...], w_ref[...], preferred_element_type=F32).astype(o_ref.dtype)


def _in_proj(h, w):
    m, d = h.shape
    n = w.shape[1]
    tm = _pick(m, (1024, 512, 256, 128))
    tn = _pick(n, (1280, 1024, 768, 512, 256, 128))
    return pl.pallas_call(
        _proj_kernel,
        grid=(m // tm, n // tn),
        in_specs=[pl.BlockSpec((tm, d), lambda i, j: (i, 0)), pl.BlockSpec((d, tn), lambda i, j: (0, j))],
        out_specs=pl.BlockSpec((tm, tn), lambda i, j: (i, j)),
        out_shape=jax.ShapeDtypeStruct((m, n), BF16),
        compiler_params=_params("arbitrary", "arbitrary"),
        name="in_proj",
    )(h, w)


def _gate_kernel(h_ref, w_ref, b_ref, o_ref, wb_s):
    @pl.when(pl.program_id(1) == 0)
    def _():
        wb_s[...] = w_ref[...].astype(BF16)

    acc = jnp.dot(h_ref[...], wb_s[...], preferred_element_type=F32)
    o_ref[...] = jax.nn.sigmoid(acc + b_ref[...]).astype(o_ref.dtype)


def _in_gates(h, w_in, b_gate, layer):
    m, d = h.shape
    n = w_in.shape[2] - QKV_W
    tm = _pick(m, (1024, 512, 256, 128))
    tn = _pick(math.gcd(n, QKV_W), (768, 512, 256, 128))
    col0 = QKV_W // tn
    return pl.pallas_call(
        _gate_kernel,
        grid=(n // tn, m // tm),
        in_specs=[pl.BlockSpec((tm, d), lambda j, i: (i, 0)),
                  pl.BlockSpec((None, d, tn), lambda j, i: (layer, 0, col0 + j)),
                  pl.BlockSpec((None, 1, tn), lambda j, i: (layer, 0, j))],
        out_specs=pl.BlockSpec((tm, tn), lambda j, i: (i, j)),
        out_shape=jax.ShapeDtypeStruct((m, n), BF16),
        scratch_shapes=[pltpu.VMEM((d, tn), BF16)],
        compiler_params=_params("arbitrary", "arbitrary"),
        name="in_gates",
    )(h, w_in, b_gate.reshape(b_gate.shape[0], 1, n))


def _split_heads(q):
    lane = lax.broadcasted_iota(jnp.int32, q.shape, 1)
    zero = jnp.zeros_like(q)
    return jnp.concatenate([jnp.where(lane < LANES // 2, q, zero),
                            jnp.where(lane >= LANES // 2, q, zero)], axis=0)


def _merge_heads_t(o):
    t = o.shape[1] // 2
    feat = lax.broadcasted_iota(jnp.int32, (LANES, t), 0)
    return jnp.where(feat < LANES // 2, o[:, :t], o[:, t:]).T


def _pv_t(v, p):
    return lax.dot_general(v, p, (((0,), (0,)), ((), ())), preferred_element_type=F32)


def _softmax_stats(s, m, l, keep=None):
    m_tile = jnp.max(s, axis=0, keepdims=True)
    if keep is not None:
        m_tile = jnp.where(keep, m_tile, NEG)
    m_new = jnp.maximum(m, m_tile)
    alpha = jnp.exp2(m - m_new)
    p = jnp.exp2(s - (m_new if keep is None else jnp.where(keep, m_new, -NEG)))
    l = alpha * l + jnp.sum(p, axis=0, keepdims=True)
    return m_new, l, p.astype(BF16), alpha


def _flash(qi, logits_fn, v_ref, scratch, keep_fn=None):
    s_s, p_s, acc_s = scratch
    keep = (lambda kj: None) if keep_fn is None else keep_fn

    def stats(kj, m, l):
        m, l, p, alpha = _softmax_stats(s_s[...], m, l, keep(kj))
        p_s[...] = p
        return m, l, alpha

    n_queries = s_s.shape[1]
    s_s[...] = logits_fn(0)
    carry = stats(0, jnp.full((1, n_queries), -jnp.inf, F32), jnp.zeros((1, n_queries), F32))
    s_s[...] = logits_fn(jnp.minimum(1, qi))
    acc_s[...] = jnp.zeros(acc_s.shape, F32)

    def body(kj, carry):
        m, l, alpha_prev = carry
        acc_s[...] = alpha_prev * acc_s[...] + _pv_t(_kv_block(v_ref, kj - 1), p_s[...])
        carry = stats(kj, m, l)
        s_s[...] = logits_fn(jnp.minimum(kj + 1, qi))
        return carry

    _, l, alpha = lax.fori_loop(1, qi + 1, body, carry)
    return (alpha * acc_s[...] + _pv_t(_kv_block(v_ref, qi), p_s[...])) / l


def _flash_scratch(n_queries):
    return [pltpu.VMEM((ATT_T, n_queries), F32), pltpu.VMEM((ATT_T, n_queries), BF16),
            pltpu.VMEM((LANES, n_queries), F32)]


def _kv_block(ref, kj):
    return ref[pl.ds(pl.multiple_of(kj * ATT_T, ATT_T), ATT_T), :]


def _diff_attn_kernel(lam_ref, g_ref, q_ref, k_ref, v_ref, bias_ref, o_ref, *scratch, lam_init):
    qi = pl.program_id(2)
    t = ATT_T
    q2 = _split_heads(q_ref[...])

    def logits(kj):
        return _dot_t(_kv_block(k_ref, kj), q2) + bias_ref[0, _tile_kind(kj, qi)]

    o = _flash(qi, logits, v_ref, scratch)
    lv = lam_ref[...]
    lam = (jnp.exp(jnp.sum(lv[0:1] * lv[1:2], axis=1, keepdims=True))
           - jnp.exp(jnp.sum(lv[2:3] * lv[3:4], axis=1, keepdims=True)) + lam_init)
    o = o[:, :t] - lam * o[:, t:]
    o = o * lax.rsqrt(jnp.mean(o * o, axis=0, keepdims=True) + EPS) * (g_ref[...] * (1.0 - lam_init))
    o_ref[...] = o.T.astype(o_ref.dtype)


def _diff_attention(lam_vecs, subln_g, qkv, tiles, batch, seq, lam_init):
    m = batch * seq
    nq = seq // ATT_T
    return pl.pallas_call(
        functools.partial(_diff_attn_kernel, lam_init=lam_init),
        grid=(batch, DA_HEADS, nq),
        in_specs=[pl.BlockSpec((4, DA_HD), lambda b, h, i: (0, 0)),
                  pl.BlockSpec((2 * DA_HD, 1), lambda b, h, i: (0, 0)),
                  pl.BlockSpec((ATT_T, LANES), lambda b, h, i: (b * nq + i, _QA + h)),
                  pl.BlockSpec((seq, LANES), lambda b, h, i: (b, _KA + h)),
                  pl.BlockSpec((seq, LANES), lambda b, h, i: (b, _VA + h)),
                  pl.BlockSpec((1, 3, ATT_T, 2 * ATT_T), lambda b, h, i: (h, 0, 0, 0))],
        out_specs=pl.BlockSpec((ATT_T, LANES), lambda b, h, i: (b * nq + i, h)),
        out_shape=jax.ShapeDtypeStruct((m, DA_W), BF16),
        scratch_shapes=_flash_scratch(2 * ATT_T),
        compiler_params=_params("arbitrary", "arbitrary", "arbitrary"),
        name="diff_attention",
    )(lam_vecs, subln_g.astype(F32).reshape(2 * DA_HD, 1), qkv, qkv, qkv, tiles)


def _swa_kernel(sink_ref, q_ref, kp_ref, ko_ref, vp_ref, vo_ref, bias_ref, o_ref):
    qi = pl.program_id(1)
    t = SW_T
    groups = SW_QW // LANES
    q = q_ref[...]
    qs = jnp.concatenate([_split_heads(q[:, g * LANES:(g + 1) * LANES]) for g in range(groups)], axis=0)
    no_prev = jnp.where(qi > 0, 0.0, NEG)
    s_prev = _dot_t(kp_ref[...], qs) + bias_ref[0:t, :] + no_prev
    s_own = _dot_t(ko_ref[...], qs) + bias_ref[t:2 * t, :]
    sink = sink_ref[...]
    m = jnp.maximum(jnp.maximum(jnp.max(s_prev, axis=0, keepdims=True),
                                jnp.max(s_own, axis=0, keepdims=True)), sink)
    p_prev = jnp.exp2(s_prev - m)
    p_own = jnp.exp2(s_own - m)
    l = (jnp.sum(p_prev, axis=0, keepdims=True) + jnp.sum(p_own, axis=0, keepdims=True)
         + jnp.exp2(sink - m))
    o = (_pv_t(vp_ref[...], p_prev) + _pv_t(vo_ref[...], p_own)) / l
    for g in range(groups):
        o_ref[:, g * LANES:(g + 1) * LANES] = _merge_heads_t(o[:, 2 * g * t:(2 * g + 2) * t]).astype(o_ref.dtype)


def _swa_attention(sink_row, qkv, win, batch, seq):
    m = batch * seq
    nq = seq // SW_T
    cols = SW_HEADS * SW_T
    prev = lambda b, i: (b * nq + jnp.maximum(i - 1, 0), _KB)
    own = lambda b, i: (b * nq + i, _KB)
    prev_v = lambda b, i: (b * nq + jnp.maximum(i - 1, 0), _VB)
    own_v = lambda b, i: (b * nq + i, _VB)
    return pl.pallas_call(
        _swa_kernel,
        grid=(batch, nq),
        in_specs=[pl.BlockSpec((1, cols), lambda b, i: (0, 0)),
                  pl.BlockSpec((SW_T, SW_QW), lambda b, i: (b * nq + i, _QB // SW_QW)),
                  pl.BlockSpec((SW_T, LANES), prev),
                  pl.BlockSpec((SW_T, LANES), own),
                  pl.BlockSpec((SW_T, LANES), prev_v),
                  pl.BlockSpec((SW_T, LANES), own_v),
                  pl.BlockSpec((2 * SW_T, cols), lambda b, i: (0, 0))],
        out_specs=pl.BlockSpec((SW_T, SW_QW), lambda b, i: (b * nq + i, 0)),
        out_shape=jax.ShapeDtypeStruct((m, SW_QW), BF16),
        compiler_params=_params("arbitrary", "arbitrary"),
        name="swa_attention",
    )(sink_row, qkv, qkv, qkv, qkv, qkv, win)


def _moba_kernel(q_ref, k_ref, v_ref, bias_ref, o_ref, kmean_s, *scratch, n_blocks):
    qi = pl.program_id(2)
    q2 = _split_heads(q_ref[...])

    @pl.when(qi == 0)
    def _():
        kf = k_ref[...].astype(F32).reshape(n_blocks, MB_BLOCK, LANES)
        kmean_s[...] = jnp.sum(kf, axis=1) * (1.0 / MB_BLOCK)

    gate = _dot_t(kmean_s[...], q2.astype(F32), precision=lax.Precision.HIGHEST)
    blk = lax.broadcasted_iota(jnp.int32, gate.shape, 0)

    def selected(kj):
        g_kj = jnp.sum(jnp.where(blk == kj, gate, 0.0), axis=0, keepdims=True)
        beats = ((gate > g_kj) | ((gate == g_kj) & (blk < kj))) & (blk < qi)
        rank = jnp.sum(beats.astype(F32), axis=0, keepdims=True)
        return (rank < MB_TOPK) | (kj >= qi)

    def logits(kj):
        return _dot_t(_kv_block(k_ref, kj), q2) + bias_ref[0, _tile_kind(kj, qi)]

    o_ref[...] = _merge_heads_t(_flash(qi, logits, v_ref, scratch, selected)).astype(o_ref.dtype)


def _moba_attention(qkv, tiles, batch, seq):
    m = batch * seq
    nq = seq // ATT_T
    n_blocks = seq // MB_BLOCK
    return pl.pallas_call(
        functools.partial(_moba_kernel, n_blocks=n_blocks),
        grid=(batch, MB_HEADS // 2, nq),
        in_specs=[pl.BlockSpec((ATT_T, LANES), lambda b, h, i: (b * nq + i, _QC + h)),
                  pl.BlockSpec((seq, LANES), lambda b, h, i: (b, _KC + h)),
                  pl.BlockSpec((seq, LANES), lambda b, h, i: (b, _VC + h)),
                  pl.BlockSpec((1, 3, ATT_T, 2 * ATT_T), lambda b, h, i: (DA_HEADS + h, 0, 0, 0))],
        out_specs=pl.BlockSpec((ATT_T, LANES), lambda b, h, i: (b * nq + i, h)),
        out_shape=jax.ShapeDtypeStruct((m, MB_W), BF16),
        scratch_shapes=[pltpu.VMEM((n_blocks, LANES), F32)] + _flash_scratch(2 * ATT_T),
        compiler_params=_params("arbitrary", "arbitrary", "arbitrary"),
        name="moba_attention",
    )(qkv, qkv, qkv, tiles)


def _mix_out_kernel(ya_ref, yb_ref, yc_ref, ga_ref, gb_ref, gc_ref, x_ref, woa_ref, wob_ref, woc_ref,
                    wout_ref, pg_ref, ng_ref, xo_ref, ho_ref):
    def branch(y_ref, w_ref, g_ref, rows):
        return g_ref[rows, :].astype(F32) * jnp.dot(y_ref[rows, :], w_ref[...], preferred_element_type=F32)

    tm = x_ref.shape[0]
    sub = MXU_N if tm % MXU_N == 0 else tm
    for r0 in range(0, tm, sub):
        rows = slice(r0, r0 + sub)
        mix = (branch(ya_ref, woa_ref, ga_ref, rows) + branch(yb_ref, wob_ref, gb_ref, rows)
               + branch(yc_ref, woc_ref, gc_ref, rows))
        z = jnp.dot(mix.astype(BF16), wout_ref[...], preferred_element_type=F32)
        xn = x_ref[rows, :] + _rms(z, pg_ref[...])
        xo_ref[rows, :] = xn
        ho_ref[rows, :] = _rms(xn, ng_ref[...]).astype(ho_ref.dtype)


def _mix_out(ya, yb, yc, gates, x2, woa, wob, woc, wout, post_g, next_g):
    m, d = x2.shape
    tm = _pick(m, (512, 256, 128))
    row = lambda i: (i, 0)
    const = lambda i: (0, 0)
    once = pl.Buffered(1)
    return pl.pallas_call(
        _mix_out_kernel,
        grid=(m // tm,),
        in_specs=[pl.BlockSpec((tm, DA_W), row), pl.BlockSpec((tm, SW_QW), row), pl.BlockSpec((tm, MB_W), row),
                  pl.BlockSpec((tm, d), lambda i: (i, 0)), pl.BlockSpec((tm, d), lambda i: (i, 1)),
                  pl.BlockSpec((tm, d), lambda i: (i, 2)),
                  pl.BlockSpec((tm, d), row),
                  pl.BlockSpec((DA_W, d), const, pipeline_mode=once),
                  pl.BlockSpec((SW_QW, d), const, pipeline_mode=once),
                  pl.BlockSpec((MB_W, d), const, pipeline_mode=once),
                  pl.BlockSpec((d, d), const, pipeline_mode=once),
                  pl.BlockSpec((1, d), const), pl.BlockSpec((1, d), const)],
        out_specs=[pl.BlockSpec((tm, d), row), pl.BlockSpec((tm, d), row)],
        out_shape=[jax.ShapeDtypeStruct((m, d), F32), jax.ShapeDtypeStruct((m, d), BF16)],
        compiler_params=_params("arbitrary"),
        name="mix_out",
    )(ya, yb, yc, gates, gates, gates, x2, woa, wob, woc, wout, post_g.reshape(1, d), next_g.reshape(1, d))


CONV_W = 3
HALO = 8


def _ffn_up_kernel(h_ref, wgf_ref, wvf_ref, cwg_ref, cwv_ref, cbg_ref, cbv_ref, o_ref, ug_s, uv_s, wg_ref, wv_ref,
                   *, tiles_per_seq):
    i = pl.program_id(1)
    tm = h_ref.shape[0]

    @pl.when(i == 0)
    def _():
        wg_ref[...] = wgf_ref[...].astype(BF16)
        wv_ref[...] = wvf_ref[...].astype(BF16)

    @pl.when(i % tiles_per_seq == 0)
    def _():
        ug_s[0:HALO, :] = jnp.zeros((HALO, ug_s.shape[1]), F32)
        uv_s[0:HALO, :] = jnp.zeros((HALO, uv_s.shape[1]), F32)

    h = h_ref[...]

    def conv(w_ref, cw_ref, cb_ref, u_s):
        u = jnp.dot(h, w_ref[...], preferred_element_type=F32)
        u_s[HALO:HALO + tm, :] = u
        cw = cw_ref[...]
        c = cb_ref[...] + u_s[HALO - 2:HALO - 2 + tm, :] * cw[0:1]
        c = c + u_s[HALO - 1:HALO - 1 + tm, :] * cw[1:2]
        c = c + u * cw[2:3]
        u_s[0:HALO, :] = u[tm - HALO:tm]
        return c

    gate = conv(wg_ref, cwg_ref, cbg_ref, ug_s)
    val = conv(wv_ref, cwv_ref, cbv_ref, uv_s)
    o_ref[...] = (jax.nn.gelu(gate, approximate=True) * val).astype(o_ref.dtype)


def _ffn_up(h, w_up, conv_w, conv_b, layer, seq):
    m, d = h.shape
    f = w_up.shape[2] // 2
    tm = _pick(seq, (1024, 512, 256, 128))
    tn = _pick(f, (512, 384, 256, 128))
    nj = f // tn
    gate_col = lambda j, i: (layer, 0, j)
    val_col = lambda j, i: (layer, 0, nj + j)
    conv_b = conv_b.reshape(conv_b.shape[0], 1, 2 * f)
    return pl.pallas_call(
        functools.partial(_ffn_up_kernel, tiles_per_seq=seq // tm),
        grid=(nj, m // tm),
        in_specs=[pl.BlockSpec((tm, d), lambda j, i: (i, 0)),
                  pl.BlockSpec((None, d, tn), gate_col), pl.BlockSpec((None, d, tn), val_col),
                  pl.BlockSpec((None, CONV_W, tn), gate_col), pl.BlockSpec((None, CONV_W, tn), val_col),
                  pl.BlockSpec((None, 1, tn), gate_col), pl.BlockSpec((None, 1, tn), val_col)],
        out_specs=pl.BlockSpec((tm, tn), lambda j, i: (i, j)),
        out_shape=jax.ShapeDtypeStruct((m, f), BF16),
        scratch_shapes=[pltpu.VMEM((HALO + tm, tn), F32), pltpu.VMEM((HALO + tm, tn), F32),
                        pltpu.VMEM((d, tn), BF16), pltpu.VMEM((d, tn), BF16)],
        compiler_params=_params("arbitrary", "arbitrary"),
        name="ffn_up",
    )(h, w_up, w_up, conv_w, conv_w, conv_b, conv_b)


def _ffn_down_kernel(a_ref, w_ref, x_ref, pg_ref, ng_ref, xo_ref, *ho_ref):
    k = pl.program_id(1)
    last = pl.num_programs(1) - 1

    @pl.when(k == 0)
    def _():
        xo_ref[...] = jnp.zeros(xo_ref.shape, F32)

    @pl.when(k < last)
    def _():
        xo_ref[...] += jnp.dot(a_ref[...], w_ref[...], preferred_element_type=F32)

    @pl.when(k == last)
    def _():
        tm = x_ref.shape[0]
        sub = MXU_N if tm % MXU_N == 0 else tm
        for r0 in range(0, tm, sub):
            rows = slice(r0, r0 + sub)
            z = xo_ref[rows, :] + jnp.dot(a_ref[rows, :], w_ref[...], preferred_element_type=F32)
            xn = x_ref[rows, :] + _rms(z, pg_ref[...])
            xo_ref[rows, :] = xn
            if ho_ref:
                ho_ref[0][rows, :] = _rms(xn, ng_ref[...]).astype(ho_ref[0].dtype)


def _ffn_down(a, w_down, x2, post_g, next_g):
    m, d = x2.shape
    f = a.shape[1]
    tm = _pick(m, (1024, 512, 256, 128))
    tk = _pick(f, (1408, 1024, 512, 384, 256, 128))
    emit_next = next_g is not None
    row = lambda i, k: (i, 0)
    const = lambda i, k: (0, 0)
    once = pl.Buffered(1)
    out_specs = [pl.BlockSpec((tm, d), row, pipeline_mode=once)]
    out_shape = [jax.ShapeDtypeStruct((m, d), F32)]
    if emit_next:
        out_specs.append(pl.BlockSpec((tm, d), row, pipeline_mode=once))
        out_shape.append(jax.ShapeDtypeStruct((m, d), BF16))
    ng = next_g if emit_next else post_g
    res = pl.pallas_call(
        _ffn_down_kernel,
        grid=(m // tm, f // tk),
        in_specs=[pl.BlockSpec((tm, tk), lambda i, k: (i, k)), pl.BlockSpec((tk, d), lambda i, k: (k, 0)),
                  pl.BlockSpec((tm, d), row, pipeline_mode=once),
                  pl.BlockSpec((1, d), const), pl.BlockSpec((1, d), const)],
        out_specs=out_specs,
        out_shape=out_shape,
        compiler_params=_params("arbitrary", "arbitrary"),
        name="ffn_down",
    )(a, w_down, x2, post_g.reshape(1, d), ng.reshape(1, d))
    return (res[0], res[1]) if emit_next else (res[0], None)


def _permute_sw_heads(w, axis):
    shape = w.shape
    split = shape[:axis] + (SW_HEADS, SW_HD) + shape[axis + 1:]
    return jnp.take(w.reshape(split), jnp.array(_SW_PERM), axis=axis).reshape(shape)


@jax.jit
def _trunk(x, rel_bias_table, w_in, b_gate, lam_q1, lam_k1, lam_q2, lam_k2, diff_subln_g, sinks, w_oa, w_ob,
           w_oc, w_out, pre_mix_g, post_mix_g, pre_ffn_g, post_ffn_g, w_up, conv_w, conv_b, w_down):
    batch, seq, d = x.shape
    depth = w_in.shape[0]
    assert seq % ATT_T == 0 and seq % MB_BLOCK == 0 and d % LANES == 0
    assert w_in.shape[2] == QKV_W + N_BRANCH * d
    m = batch * seq

    tab_flat = rel_bias_table.astype(F32).reshape(-1)
    tiles, win = _bias_tiles(tab_flat)
    perm = jnp.array(_SW_PERM)

    x2 = x.reshape(m, d)
    h = _prenorm(x2, pre_mix_g[0])
    for l in range(depth):
        w_qkv = w_in[l, :, :QKV_W]
        w_qkv = jnp.concatenate(
            [w_qkv[:, :DA_W] * Q_SCALE, w_qkv[:, DA_W:_QB],
             _permute_sw_heads(w_qkv[:, _QB:_QB + SW_QW], 1) * Q_SCALE, w_qkv[:, _QB + SW_QW:_QC * LANES],
             w_qkv[:, _QC * LANES:_KC * LANES] * Q_SCALE, w_qkv[:, _KC * LANES:]], axis=1).astype(BF16)
        qkv = _in_proj(h, w_qkv)
        gates = _in_gates(h, w_in, b_gate, l)

        lam_init = 0.8 - 0.6 * math.exp(-0.3 * l)
        lam_vecs = jnp.stack([lam_q1[l], lam_k1[l], lam_q2[l], lam_k2[l]]).astype(F32)
        ya = _diff_attention(lam_vecs, diff_subln_g[l], qkv, tiles, batch, seq, lam_init)
        sink_row = jnp.repeat(sinks[l].astype(F32)[perm] * LOG2E, SW_T).reshape(1, SW_HEADS * SW_T)
        yb = _swa_attention(sink_row, qkv, win, batch, seq)
        yc = _moba_attention(qkv, tiles, batch, seq)

        x2, h = _mix_out(ya, yb, yc, gates, x2, w_oa[l].astype(BF16),
                         _permute_sw_heads(w_ob[l], 0).astype(BF16), w_oc[l].astype(BF16),
                         w_out[l].astype(BF16), post_mix_g[l], pre_ffn_g[l])

        a = _ffn_up(h, w_up, conv_w, conv_b, l, seq)
        next_g = pre_mix_g[l + 1] if l + 1 < depth else None
        x2, h = _ffn_down(a, w_down[l].astype(BF16), x2, post_ffn_g[l], next_g)
    return x2.reshape(batch, seq, d)


def kernel(x, rel_bias_table, w_in, b_gate, lam_q1, lam_k1, lam_q2, lam_k2, diff_subln_g, sinks, w_oa, w_ob, w_oc, w_out, pre_mix_g, post_mix_g, pre_ffn_g, post_ffn_g, w_up, conv_w, conv_b, w_down):
    return _trunk(x, rel_bias_table, w_in, b_gate, lam_q1, lam_k1, lam_q2, lam_k2, diff_subln_g, sinks, w_oa, w_ob,
                  w_oc, w_out, pre_mix_g, post_mix_g, pre_ffn_g, post_ffn_g, w_up, conv_w, conv_b, w_down)
```

```python
import functools
import math

import jax
import jax.numpy as jnp
from jax import lax
from jax.experimental import pallas as pl
from jax.experimental.pallas import tpu as pltpu

DA_HEADS = 4
DA_HD = 64
DA_W = DA_HEADS * 2 * DA_HD
SW_HEADS = 8
SW_KV = 2
SW_HD = 64
WINDOW = 128
SW_QW = SW_HEADS * SW_HD
SW_KW = SW_KV * SW_HD
MB_HEADS = 8
MB_HD = 64
MB_W = MB_HEADS * MB_HD
MB_BLOCK = 256
MB_TOPK = 3
N_BUCKETS = 32
MAX_DIST = 128
N_ATT_HEADS = DA_HEADS + SW_HEADS + MB_HEADS
N_BRANCH = 3
QKV_W = 3 * DA_W + SW_QW + 2 * SW_KW + 3 * MB_W
EPS = 1e-6

LANES = 128
MXU_N = 256
BF16_ROWS = 16
ATT_T = 256
SW_T = WINDOW
NEG = -1e30
LOG2E = math.log2(math.e)
Q_SCALE = DA_HD ** -0.5 * LOG2E
VMEM_LIMIT = 56 * 1024 * 1024

_QA, _KA, _VA = 0, DA_W // LANES, 2 * DA_W // LANES
_QC = 3 * DA_W // LANES
_KC = _QC + MB_W // LANES
_VC = _KC + MB_W // LANES
_QB = 3 * DA_W + 3 * MB_W
_KB = (_QB + SW_QW) // LANES
_VB = _KB + 1
_REF_QB = 3 * DA_W
_REF_KB = _REF_QB + SW_QW
_REF_QC = _REF_KB + 2 * SW_KW
_SW_PERM = tuple(h // 2 + (h % 2) * (SW_HEADS // SW_KV) for h in range(SW_HEADS))

F32 = jnp.float32
BF16 = jnp.bfloat16


def _pick(n, candidates):
    for c in candidates:
        if n % c == 0:
            return c
    raise ValueError(f"no tile in {candidates} divides {n}")


def _params(*sem):
    return pltpu.CompilerParams(dimension_semantics=sem, vmem_limit_bytes=VMEM_LIMIT)


def _rms(x, g):
    return x * lax.rsqrt(jnp.mean(x * x, axis=-1, keepdims=True) + EPS) * g


def _dot_t(a, b, **kw):
    return lax.dot_general(a, b, (((1,), (1,)), ((), ())), preferred_element_type=F32, **kw)


def _rel_bucket(dist):
    n = jnp.maximum(dist, 0)
    max_exact = N_BUCKETS // 2
    nf = jnp.maximum(n, 1).astype(F32)
    large = max_exact + (jnp.log(nf / max_exact) / math.log(MAX_DIST / max_exact)
                         * (N_BUCKETS - max_exact)).astype(jnp.int32)
    large = jnp.minimum(large, N_BUCKETS - 1)
    return jnp.where(n < max_exact, n, large)


def _bias_lookup(tab_ref, head, dist):
    bucket = _rel_bucket(dist)
    acc = jnp.zeros(dist.shape, F32)
    for b in range(N_BUCKETS):
        acc = jnp.where(bucket == b, tab_ref[b * N_ATT_HEADS + head], acc)
    return acc * LOG2E


FAR, NEAR, DIAG = 0, 1, 2


def _tile_kind(kj, qi):
    return jnp.clip(kj - qi + DIAG, FAR, DIAG)


def _bias_kernel(tab_ref, o_ref):
    e = pl.program_id(0)
    moba0 = DA_HEADS + SW_HEADS + 2 * (e - DA_HEADS)
    heads = (jnp.where(e < DA_HEADS, e, moba0), jnp.where(e < DA_HEADS, e, moba0 + 1))
    d = (lax.broadcasted_iota(jnp.int32, (ATT_T, ATT_T), 1)
         - lax.broadcasted_iota(jnp.int32, (ATT_T, ATT_T), 0))
    for c, head in enumerate(heads):
        cols = slice(c * ATT_T, (c + 1) * ATT_T)
        o_ref[0, FAR, :, cols] = _bias_lookup(tab_ref, head, d + 2 * ATT_T)
        o_ref[0, NEAR, :, cols] = _bias_lookup(tab_ref, head, d + ATT_T)
        o_ref[0, DIAG, :, cols] = jnp.where(d >= 0, _bias_lookup(tab_ref, head, d), NEG)


def _window_kernel(tab_ref, win_ref):
    c = pl.program_id(0)
    head = DA_HEADS + c // 2 + (c % 2) * (SW_HEADS // SW_KV)
    d = (lax.broadcasted_iota(jnp.int32, (2 * SW_T, SW_T), 1) + SW_T
         - lax.broadcasted_iota(jnp.int32, (2 * SW_T, SW_T), 0))
    win_ref[...] = jnp.where((d >= 0) & (d < WINDOW), _bias_lookup(tab_ref, head, d), NEG)


def _bias_tiles(tab_flat):
    assert ATT_T + 1 >= MAX_DIST
    n = DA_HEADS + MB_HEADS // 2
    tiles = pl.pallas_call(
        _bias_kernel,
        grid=(n,),
        in_specs=[pl.BlockSpec(memory_space=pltpu.SMEM)],
        out_specs=pl.BlockSpec((1, 3, ATT_T, 2 * ATT_T), lambda e: (e, 0, 0, 0)),
        out_shape=jax.ShapeDtypeStruct((n, 3, ATT_T, 2 * ATT_T), F32),
        compiler_params=_params("arbitrary"),
        name="bias_tiles",
    )(tab_flat)
    win = pl.pallas_call(
        _window_kernel,
        grid=(SW_HEADS,),
        in_specs=[pl.BlockSpec(memory_space=pltpu.SMEM)],
        out_specs=pl.BlockSpec((2 * SW_T, SW_T), lambda c: (0, c)),
        out_shape=jax.ShapeDtypeStruct((2 * SW_T, SW_HEADS * SW_T), F32),
        compiler_params=_params("arbitrary"),
        name="window_tiles",
    )(tab_flat)
    return tiles, win


def _norm_kernel(x_ref, g_ref, o_ref):
    o_ref[...] = _rms(x_ref[...], g_ref[...]).astype(o_ref.dtype)


def _prenorm(x2, g):
    m, d = x2.shape
    tm = _pick(m, (512, 256, 128))
    return pl.pallas_call(
        _norm_kernel,
        grid=(m // tm,),
        in_specs=[pl.BlockSpec((tm, d), lambda i: (i, 0)), pl.BlockSpec((1, d), lambda i: (0, 0))],
        out_specs=pl.BlockSpec((tm, d), lambda i: (i, 0)),
        out_shape=jax.ShapeDtypeStruct((m, d), BF16),
        compiler_params=_params("arbitrary"),
        name="prenorm",
    )(x2, g.reshape(1, d))


def _proj_kernel(h_ref, w_ref, o_ref):
    o_ref[...] = jnp.dot(h_ref[...], w_ref[...], preferred_element_type=F32).astype(o_ref.dtype)


def _in_proj(h, w):
    m, d = h.shape
    n = w.shape[1]
    tm = _pick(m, (1024, 512, 256, 128))
    tn = _pick(n, (1280, 1024, 768, 512, 256, 128))
    return pl.pallas_call(
        _proj_kernel,
        grid=(m // tm, n // tn),
        in_specs=[pl.BlockSpec((tm, d), lambda i, j: (i, 0)), pl.BlockSpec((d, tn), lambda i, j: (0, j))],
        out_specs=pl.BlockSpec((tm, tn), lambda i, j: (i, j)),
        out_shape=jax.ShapeDtypeStruct((m, n), BF16),
        compiler_params=_params("arbitrary", "arbitrary"),
        name="in_proj",
    )(h, w)


def _gate_kernel(h_ref, w_ref, b_ref, o_ref, wb_s):
    @pl.when(pl.program_id(1) == 0)
    def _():
        wb_s[...] = w_ref[...].astype(BF16)

    acc = jnp.dot(h_ref[...], wb_s[...], preferred_element_type=F32)
    o_ref[...] = jax.nn.sigmoid(acc + b_ref[...]).astype(o_ref.dtype)


def _in_gates(h, w_in, b_gate, layer):
    m, d = h.shape
    n = w_in.shape[2] - QKV_W
    tm = _pick(m, (1024, 512, 256, 128))
    tn = _pick(math.gcd(n, QKV_W), (768, 512, 256, 128))
    col0 = QKV_W // tn
    return pl.pallas_call(
        _gate_kernel,
        grid=(n // tn, m // tm),
        in_specs=[pl.BlockSpec((tm, d), lambda j, i: (i, 0)),
                  pl.BlockSpec((None, d, tn), lambda j, i: (layer, 0, col0 + j)),
                  pl.BlockSpec((None, 1, tn), lambda j, i: (layer, 0, j))],
        out_specs=pl.BlockSpec((tm, tn), lambda j, i: (i, j)),
        out_shape=jax.ShapeDtypeStruct((m, n), BF16),
        scratch_shapes=[pltpu.VMEM((d, tn), BF16)],
        compiler_params=_params("arbitrary", "arbitrary"),
        name="in_gates",
    )(h, w_in, b_gate.reshape(b_gate.shape[0], 1, n))


def _split_heads(q):
    lane = lax.broadcasted_iota(jnp.int32, q.shape, 1)
    zero = jnp.zeros_like(q)
    return jnp.concatenate([jnp.where(lane < LANES // 2, q, zero),
                            jnp.where(lane >= LANES // 2, q, zero)], axis=0)


def _merge_heads_t(o):
    t = o.shape[1] // 2
    feat = lax.broadcasted_iota(jnp.int32, (LANES, t), 0)
    return jnp.where(feat < LANES // 2, o[:, :t], o[:, t:]).T


def _pv_t(v, p):
    return lax.dot_general(v, p, (((0,), (0,)), ((), ())), preferred_element_type=F32)


def _softmax_stats(s, m, l, keep=None):
    m_tile = jnp.max(s, axis=0, keepdims=True)
    if keep is not None:
        m_tile = jnp.where(keep, m_tile, NEG)
    m_new = jnp.maximum(m, m_tile)
    alpha = jnp.exp2(m - m_new)
    p = jnp.exp2(s - (m_new if keep is None else jnp.where(keep, m_new, -NEG)))
    l = alpha * l + jnp.sum(p, axis=0, keepdims=True)
    return m_new, l, p.astype(BF16), alpha


def _flash(qi, streams):
    def stats(stream, kj, m, l):
        _, _, (s_s, p_s, _), keep_fn = stream
        m, l, p, alpha = _softmax_stats(s_s[...], m, l, None if keep_fn is None else keep_fn(kj))
        p_s[...] = p
        return m, l, alpha

    for logits_fn, _, (s_s, _, _), _ in streams:
        s_s[...] = logits_fn(0)
    carry = []
    for stream in streams:
        n_queries = stream[2][0].shape[1]
        carry.append(stats(stream, 0, jnp.full((1, n_queries), -jnp.inf, F32), jnp.zeros((1, n_queries), F32)))
    for logits_fn, _, (s_s, _, acc_s), _ in streams:
        s_s[...] = logits_fn(jnp.minimum(1, qi))
        acc_s[...] = jnp.zeros(acc_s.shape, F32)

    def body(kj, carry):
        out = []
        for stream, (m, l, alpha_prev) in zip(streams, carry):
            logits_fn, values_fn, (s_s, p_s, acc_s), _ = stream
            acc_s[...] = alpha_prev * acc_s[...] + _pv_t(values_fn(kj - 1), p_s[...])
            out.append(stats(stream, kj, m, l))
            s_s[...] = logits_fn(jnp.minimum(kj + 1, qi))
        return tuple(out)

    carry = lax.fori_loop(1, qi + 1, body, tuple(carry))
    return [(alpha * acc_s[...] + _pv_t(values_fn(qi), p_s[...])) / l
            for (_, values_fn, (_, p_s, acc_s), _), (_, l, alpha) in zip(streams, carry)]


N_STREAMS = 4


def _flash_scratch(n_queries):
    return N_STREAMS * [pltpu.VMEM((ATT_T, n_queries), F32), pltpu.VMEM((ATT_T, n_queries), BF16),
                        pltpu.VMEM((LANES, n_queries), F32)]


def _kv_block(ref, kj, group):
    return ref[pl.ds(pl.multiple_of(kj * ATT_T, ATT_T), ATT_T), group * LANES:(group + 1) * LANES]


def _diff_attn_kernel(lam_ref, g_ref, q_ref, k_ref, v_ref, bias_ref, o_ref, *scratch, lam_init):
    qi = pl.program_id(2)
    t = ATT_T

    def stream(g):
        q2 = _split_heads(q_ref[:, g * LANES:(g + 1) * LANES])

        def logits(kj):
            return _dot_t(_kv_block(k_ref, kj, g), q2) + bias_ref[g, _tile_kind(kj, qi)]

        return logits, lambda kj: _kv_block(v_ref, kj, g), scratch[3 * g:3 * g + 3], None

    outs = _flash(qi, [stream(g) for g in range(N_STREAMS)])
    lv = lam_ref[...]
    lam = (jnp.exp(jnp.sum(lv[0:1] * lv[1:2], axis=1, keepdims=True))
           - jnp.exp(jnp.sum(lv[2:3] * lv[3:4], axis=1, keepdims=True)) + lam_init)
    for g, o in enumerate(outs):
        o = o[:, :t] - lam * o[:, t:]
        o = o * lax.rsqrt(jnp.mean(o * o, axis=0, keepdims=True) + EPS) * (g_ref[...] * (1.0 - lam_init))
        o_ref[:, g * LANES:(g + 1) * LANES] = o.T.astype(o_ref.dtype)


def _diff_attention(lam_vecs, subln_g, qkv, tiles, batch, seq, lam_init):
    m = batch * seq
    nq = seq // ATT_T
    w = N_STREAMS * LANES
    return pl.pallas_call(
        functools.partial(_diff_attn_kernel, lam_init=lam_init),
        grid=(batch, DA_HEADS // N_STREAMS, nq),
        in_specs=[pl.BlockSpec((4, DA_HD), lambda b, h, i: (0, 0)),
                  pl.BlockSpec((2 * DA_HD, 1), lambda b, h, i: (0, 0)),
                  pl.BlockSpec((ATT_T, w), lambda b, h, i: (b * nq + i, _QA // N_STREAMS + h)),
                  pl.BlockSpec((seq, w), lambda b, h, i: (b, _KA // N_STREAMS + h)),
                  pl.BlockSpec((seq, w), lambda b, h, i: (b, _VA // N_STREAMS + h)),
                  pl.BlockSpec((N_STREAMS, 3, ATT_T, 2 * ATT_T), lambda b, h, i: (h, 0, 0, 0))],
        out_specs=pl.BlockSpec((ATT_T, w), lambda b, h, i: (b * nq + i, h)),
        out_shape=jax.ShapeDtypeStruct((m, DA_W), BF16),
        scratch_shapes=_flash_scratch(2 * ATT_T),
        compiler_params=_params("arbitrary", "arbitrary", "arbitrary"),
        name="diff_attention",
    )(lam_vecs, subln_g.astype(F32).reshape(2 * DA_HD, 1), qkv, qkv, qkv, tiles)


def _swa_kernel(sink_ref, q_ref, kp_ref, ko_ref, vp_ref, vo_ref, bias_ref, o_ref):
    qi = pl.program_id(1)
    t = SW_T
    groups = SW_QW // LANES
    q = q_ref[...]
    qs = jnp.concatenate([_split_heads(q[:, g * LANES:(g + 1) * LANES]) for g in range(groups)], axis=0)
    no_prev = jnp.where(qi > 0, 0.0, NEG)
    s_prev = _dot_t(kp_ref[...], qs) + bias_ref[0:t, :] + no_prev
    s_own = _dot_t(ko_ref[...], qs) + bias_ref[t:2 * t, :]
    sink = sink_ref[...]
    m = jnp.maximum(jnp.maximum(jnp.max(s_prev, axis=0, keepdims=True),
                                jnp.max(s_own, axis=0, keepdims=True)), sink)
    p_prev = jnp.exp2(s_prev - m)
    p_own = jnp.exp2(s_own - m)
    l = (jnp.sum(p_prev, axis=0, keepdims=True) + jnp.sum(p_own, axis=0, keepdims=True)
         + jnp.exp2(sink - m))
    o = (_pv_t(vp_ref[...], p_prev) + _pv_t(vo_ref[...], p_own)) / l
    for g in range(groups):
        o_ref[:, g * LANES:(g + 1) * LANES] = _merge_heads_t(o[:, 2 * g * t:(2 * g + 2) * t]).astype(o_ref.dtype)


def _swa_attention(sink_row, qkv, win, batch, seq):
    m = batch * seq
    nq = seq // SW_T
    cols = SW_HEADS * SW_T
    prev = lambda b, i: (b * nq + jnp.maximum(i - 1, 0), _KB)
    own = lambda b, i: (b * nq + i, _KB)
    prev_v = lambda b, i: (b * nq + jnp.maximum(i - 1, 0), _VB)
    own_v = lambda b, i: (b * nq + i, _VB)
    return pl.pallas_call(
        _swa_kernel,
        grid=(batch, nq),
        in_specs=[pl.BlockSpec((1, cols), lambda b, i: (0, 0)),
                  pl.BlockSpec((SW_T, SW_QW), lambda b, i: (b * nq + i, _QB // SW_QW)),
                  pl.BlockSpec((SW_T, LANES), prev),
                  pl.BlockSpec((SW_T, LANES), own),
                  pl.BlockSpec((SW_T, LANES), prev_v),
                  pl.BlockSpec((SW_T, LANES), own_v),
                  pl.BlockSpec((2 * SW_T, cols), lambda b, i: (0, 0))],
        out_specs=pl.BlockSpec((SW_T, SW_QW), lambda b, i: (b * nq + i, 0)),
        out_shape=jax.ShapeDtypeStruct((m, SW_QW), BF16),
        compiler_params=_params("arbitrary", "arbitrary"),
        name="swa_attention",
    )(sink_row, qkv, qkv, qkv, qkv, qkv, win)


def _moba_kernel(q_ref, k_ref, v_ref, bias_ref, o_ref, kmean_s, *scratch, n_blocks):
    qi = pl.program_id(2)
    stride = kmean_s.shape[0] // 3

    @pl.when(qi == 0)
    def _():
        kf = k_ref[...].astype(F32).reshape(n_blocks, MB_BLOCK, N_STREAMS * LANES)
        rest = jnp.sum(kf, axis=1) * (1.0 / MB_BLOCK)
        kmean_s[...] = jnp.zeros(kmean_s.shape, BF16)
        for part in range(3):
            term = rest.astype(BF16)
            kmean_s[part * stride:part * stride + n_blocks, :] = term
            rest = rest - term.astype(F32)

    def stream(g):
        q2 = _split_heads(q_ref[:, g * LANES:(g + 1) * LANES])
        parts = _dot_t(kmean_s[:, g * LANES:(g + 1) * LANES], q2)
        gate = parts[0:n_blocks] + parts[stride:stride + n_blocks] + parts[2 * stride:2 * stride + n_blocks]
        blk = lax.broadcasted_iota(jnp.int32, gate.shape, 0)

        def selected(kj):
            g_kj = jnp.sum(jnp.where(blk == kj, gate, 0.0), axis=0, keepdims=True)
            beats = ((gate > g_kj) | ((gate == g_kj) & (blk < kj))) & (blk < qi)
            rank = jnp.sum(beats.astype(F32), axis=0, keepdims=True)
            return (rank < MB_TOPK) | (kj >= qi)

        def logits(kj):
            return _dot_t(_kv_block(k_ref, kj, g), q2) + bias_ref[g, _tile_kind(kj, qi)]

        return logits, lambda kj: _kv_block(v_ref, kj, g), scratch[3 * g:3 * g + 3], selected

    outs = _flash(qi, [stream(g) for g in range(N_STREAMS)])
    for g, o in enumerate(outs):
        o_ref[:, g * LANES:(g + 1) * LANES] = _merge_heads_t(o).astype(o_ref.dtype)


def _moba_attention(qkv, tiles, batch, seq):
    m = batch * seq
    nq = seq // ATT_T
    n_blocks = seq // MB_BLOCK
    w = N_STREAMS * LANES
    tile0 = DA_HEADS // N_STREAMS
    return pl.pallas_call(
        functools.partial(_moba_kernel, n_blocks=n_blocks),
        grid=(batch, MB_HEADS // 2 // N_STREAMS, nq),
        in_specs=[pl.BlockSpec((ATT_T, w), lambda b, h, i: (b * nq + i, _QC // N_STREAMS + h)),
                  pl.BlockSpec((seq, w), lambda b, h, i: (b, _KC // N_STREAMS + h)),
                  pl.BlockSpec((seq, w), lambda b, h, i: (b, _VC // N_STREAMS + h)),
                  pl.BlockSpec((N_STREAMS, 3, ATT_T, 2 * ATT_T), lambda b, h, i: (tile0 + h, 0, 0, 0))],
        out_specs=pl.BlockSpec((ATT_T, w), lambda b, h, i: (b * nq + i, h)),
        out_shape=jax.ShapeDtypeStruct((m, MB_W), BF16),
        scratch_shapes=[pltpu.VMEM((3 * pl.cdiv(n_blocks, BF16_ROWS) * BF16_ROWS, w), BF16)]
        + _flash_scratch(2 * ATT_T),
        compiler_params=_params("arbitrary", "arbitrary", "arbitrary"),
        name="moba_attention",
    )(qkv, qkv, qkv, tiles)


def _mix_out_kernel(ya_ref, yb_ref, yc_ref, ga_ref, gb_ref, gc_ref, x_ref, woa_ref, wob_ref, woc_ref,
                    wout_ref, pg_ref, ng_ref, xo_ref, ho_ref):
    def branch(y_ref, w_ref, g_ref, rows):
        return g_ref[rows, :].astype(F32) * jnp.dot(y_ref[rows, :], w_ref[...], preferred_element_type=F32)

    tm = x_ref.shape[0]
    sub = MXU_N if tm % MXU_N == 0 else tm
    for r0 in range(0, tm, sub):
        rows = slice(r0, r0 + sub)
        mix = (branch(ya_ref, woa_ref, ga_ref, rows) + branch(yb_ref, wob_ref, gb_ref, rows)
               + branch(yc_ref, woc_ref, gc_ref, rows))
        z = jnp.dot(mix.astype(BF16), wout_ref[...], preferred_element_type=F32)
        xn = x_ref[rows, :] + _rms(z, pg_ref[...])
        xo_ref[rows, :] = xn
        ho_ref[rows, :] = _rms(xn, ng_ref[...]).astype(ho_ref.dtype)


def _mix_out(ya, yb, yc, gates, x2, woa, wob, woc, wout, post_g, next_g):
    m, d = x2.shape
    tm = _pick(m, (512, 256, 128))
    row = lambda i: (i, 0)
    const = lambda i: (0, 0)
    once = pl.Buffered(1)
    return pl.pallas_call(
        _mix_out_kernel,
        grid=(m // tm,),
        in_specs=[pl.BlockSpec((tm, DA_W), row), pl.BlockSpec((tm, SW_QW), row), pl.BlockSpec((tm, MB_W), row),
                  pl.BlockSpec((tm, d), lambda i: (i, 0)), pl.BlockSpec((tm, d), lambda i: (i, 1)),
                  pl.BlockSpec((tm, d), lambda i: (i, 2)),
                  pl.BlockSpec((tm, d), row),
                  pl.BlockSpec((DA_W, d), const, pipeline_mode=once),
                  pl.BlockSpec((SW_QW, d), const, pipeline_mode=once),
                  pl.BlockSpec((MB_W, d), const, pipeline_mode=once),
                  pl.BlockSpec((d, d), const, pipeline_mode=once),
                  pl.BlockSpec((1, d), const), pl.BlockSpec((1, d), const)],
        out_specs=[pl.BlockSpec((tm, d), row), pl.BlockSpec((tm, d), row)],
        out_shape=[jax.ShapeDtypeStruct((m, d), F32), jax.ShapeDtypeStruct((m, d), BF16)],
        compiler_params=_params("arbitrary"),
        name="mix_out",
    )(ya, yb, yc, gates, gates, gates, x2, woa, wob, woc, wout, post_g.reshape(1, d), next_g.reshape(1, d))


CONV_W = 3
HALO = 8


def _ffn_up_kernel(h_ref, wgf_ref, wvf_ref, cwg_ref, cwv_ref, cbg_ref, cbv_ref, o_ref, ug_s, uv_s, wg_ref, wv_ref,
                   *, tiles_per_seq):
    i = pl.program_id(1)
    tm = h_ref.shape[0]

    @pl.when(i == 0)
    def _():
        wg_ref[...] = wgf_ref[...].astype(BF16)
        wv_ref[...] = wvf_ref[...].astype(BF16)

    @pl.when(i % tiles_per_seq == 0)
    def _():
        ug_s[0:HALO, :] = jnp.zeros((HALO, ug_s.shape[1]), F32)
        uv_s[0:HALO, :] = jnp.zeros((HALO, uv_s.shape[1]), F32)

    h = h_ref[...]

    def conv(w_ref, cw_ref, cb_ref, u_s):
        u = jnp.dot(h, w_ref[...], preferred_element_type=F32)
        u_s[HALO:HALO + tm, :] = u
        cw = cw_ref[...]
        c = cb_ref[...] + u_s[HALO - 2:HALO - 2 + tm, :] * cw[0:1]
        c = c + u_s[HALO - 1:HALO - 1 + tm, :] * cw[1:2]
        c = c + u * cw[2:3]
        u_s[0:HALO, :] = u[tm - HALO:tm]
        return c

    gate = conv(wg_ref, cwg_ref, cbg_ref, ug_s)
    val = conv(wv_ref, cwv_ref, cbv_ref, uv_s)
    o_ref[...] = (jax.nn.gelu(gate, approximate=True) * val).astype(o_ref.dtype)


def _ffn_up(h, w_up, conv_w, conv_b, layer, seq):
    m, d = h.shape
    f = w_up.shape[2] // 2
    tm = _pick(seq, (1024, 512, 256, 128))
    tn = _pick(f, (512, 384, 256, 128))
    nj = f // tn
    gate_col = lambda j, i: (layer, 0, j)
    val_col = lambda j, i: (layer, 0, nj + j)
    conv_b = conv_b.reshape(conv_b.shape[0], 1, 2 * f)
    return pl.pallas_call(
        functools.partial(_ffn_up_kernel, tiles_per_seq=seq // tm),
        grid=(nj, m // tm),
        in_specs=[pl.BlockSpec((tm, d), lambda j, i: (i, 0)),
                  pl.BlockSpec((None, d, tn), gate_col), pl.BlockSpec((None, d, tn), val_col),
                  pl.BlockSpec((None, CONV_W, tn), gate_col), pl.BlockSpec((None, CONV_W, tn), val_col),
                  pl.BlockSpec((None, 1, tn), gate_col), pl.BlockSpec((None, 1, tn), val_col)],
        out_specs=pl.BlockSpec((tm, tn), lambda j, i: (i, j)),
        out_shape=jax.ShapeDtypeStruct((m, f), BF16),
        scratch_shapes=[pltpu.VMEM((HALO + tm, tn), F32), pltpu.VMEM((HALO + tm, tn), F32),
                        pltpu.VMEM((d, tn), BF16), pltpu.VMEM((d, tn), BF16)],
        compiler_params=_params("arbitrary", "arbitrary"),
        name="ffn_up",
    )(h, w_up, w_up, conv_w, conv_w, conv_b, conv_b)


def _ffn_down_kernel(a_ref, w_ref, x_ref, pg_ref, ng_ref, xo_ref, *ho_ref):
    k = pl.program_id(1)
    last = pl.num_programs(1) - 1

    @pl.when(k == 0)
    def _():
        xo_ref[...] = jnp.zeros(xo_ref.shape, F32)

    @pl.when(k < last)
    def _():
        xo_ref[...] += jnp.dot(a_ref[...], w_ref[...], preferred_element_type=F32)

    @pl.when(k == last)
    def _():
        tm = x_ref.shape[0]
        sub = MXU_N if tm % MXU_N == 0 else tm
        for r0 in range(0, tm, sub):
            rows = slice(r0, r0 + sub)
            z = xo_ref[rows, :] + jnp.dot(a_ref[rows, :], w_ref[...], preferred_element_type=F32)
            xn = x_ref[rows, :] + _rms(z, pg_ref[...])
            xo_ref[rows, :] = xn
            if ho_ref:
                ho_ref[0][rows, :] = _rms(xn, ng_ref[...]).astype(ho_ref[0].dtype)


def _ffn_down(a, w_down, x2, post_g, next_g):
    m, d = x2.shape
    f = a.shape[1]
    tm = _pick(m, (1024, 512, 256, 128))
    tk = _pick(f, (1408, 1024, 512, 384, 256, 128))
    emit_next = next_g is not None
    row = lambda i, k: (i, 0)
    const = lambda i, k: (0, 0)
    once = pl.Buffered(1)
    out_specs = [pl.BlockSpec((tm, d), row, pipeline_mode=once)]
    out_shape = [jax.ShapeDtypeStruct((m, d), F32)]
    if emit_next:
        out_specs.append(pl.BlockSpec((tm, d), row, pipeline_mode=once))
        out_shape.append(jax.ShapeDtypeStruct((m, d), BF16))
    ng = next_g if emit_next else post_g
    res = pl.pallas_call(
        _ffn_down_kernel,
        grid=(m // tm, f // tk),
        in_specs=[pl.BlockSpec((tm, tk), lambda i, k: (i, k)), pl.BlockSpec((tk, d), lambda i, k: (k, 0)),
                  pl.BlockSpec((tm, d), row, pipeline_mode=once),
                  pl.BlockSpec((1, d), const), pl.BlockSpec((1, d), const)],
        out_specs=out_specs,
        out_shape=out_shape,
        compiler_params=_params("arbitrary", "arbitrary"),
        name="ffn_down",
    )(a, w_down, x2, post_g.reshape(1, d), ng.reshape(1, d))
    return (res[0], res[1]) if emit_next else (res[0], None)


def _permute_sw_heads(w, axis):
    shape = w.shape
    split = shape[:axis] + (SW_HEADS, SW_HD) + shape[axis + 1:]
    return jnp.take(w.reshape(split), jnp.array(_SW_PERM), axis=axis).reshape(shape)


@jax.jit
def _trunk(x, rel_bias_table, w_in, b_gate, lam_q1, lam_k1, lam_q2, lam_k2, diff_subln_g, sinks, w_oa, w_ob,
           w_oc, w_out, pre_mix_g, post_mix_g, pre_ffn_g, post_ffn_g, w_up, conv_w, conv_b, w_down):
    batch, seq, d = x.shape
    depth = w_in.shape[0]
    assert seq % ATT_T == 0 and seq % MB_BLOCK == 0 and d % LANES == 0
    assert w_in.shape[2] == QKV_W + N_BRANCH * d
    m = batch * seq

    tab_flat = rel_bias_table.astype(F32).reshape(-1)
    tiles, win = _bias_tiles(tab_flat)
    perm = jnp.array(_SW_PERM)

    x2 = x.reshape(m, d)
    h = _prenorm(x2, pre_mix_g[0])
    for l in range(depth):
        w = w_in[l]
        w_qkv = jnp.concatenate(
            [w[:, :DA_W] * Q_SCALE, w[:, DA_W:_REF_QB],
             w[:, _REF_QC:_REF_QC + MB_W] * Q_SCALE, w[:, _REF_QC + MB_W:QKV_W],
             _permute_sw_heads(w[:, _REF_QB:_REF_KB], 1) * Q_SCALE, w[:, _REF_KB:_REF_QC]
             ], axis=1).astype(BF16)
        qkv = _in_proj(h, w_qkv)
        gates = _in_gates(h, w_in, b_gate, l)

        lam_init = 0.8 - 0.6 * math.exp(-0.3 * l)
        lam_vecs = jnp.stack([lam_q1[l], lam_k1[l], lam_q2[l], lam_k2[l]]).astype(F32)
        ya = _diff_attention(lam_vecs, diff_subln_g[l], qkv, tiles, batch, seq, lam_init)
        sink_row = jnp.repeat(sinks[l].astype(F32)[perm] * LOG2E, SW_T).reshape(1, SW_HEADS * SW_T)
        yb = _swa_attention(sink_row, qkv, win, batch, seq)
        yc = _moba_attention(qkv, tiles, batch, seq)

        x2, h = _mix_out(ya, yb, yc, gates, x2, w_oa[l].astype(BF16),
                         _permute_sw_heads(w_ob[l], 0).astype(BF16), w_oc[l].astype(BF16),
                         w_out[l].astype(BF16), post_mix_g[l], pre_ffn_g[l])

        a = _ffn_up(h, w_up, conv_w, conv_b, l, seq)
        next_g = pre_mix_g[l + 1] if l + 1 < depth else None
        x2, h = _ffn_down(a, w_down[l].astype(BF16), x2, post_ffn_g[l], next_g)
    return x2.reshape(batch, seq, d)


def kernel(x, rel_bias_table, w_in, b_gate, lam_q1, lam_k1, lam_q2, lam_k2, diff_subln_g, sinks, w_oa, w_ob, w_oc, w_out, pre_mix_g, post_mix_g, pre_ffn_g, post_ffn_g, w_up, conv_w, conv_b, w_down):
    return _trunk(x, rel_bias_table, w_in, b_gate, lam_q1, lam_k1, lam_q2, lam_k2, diff_subln_g, sinks, w_oa, w_ob,
                  w_oc, w_out, pre_mix_g, post_mix_g, pre_ffn_g, post_ffn_g, w_up, conv_w, conv_b, w_down)
```

```python
import functools
import math

import numpy as np
import jax
import jax.numpy as jnp
from jax import lax
from jax.experimental import pallas as pl
from jax.experimental.pallas import tpu as pltpu

DA_HEADS = 4
DA_HD = 64
DA_W = DA_HEADS * 2 * DA_HD
SW_HEADS = 8
SW_KV = 2
SW_HD = 64
WINDOW = 128
SW_QW = SW_HEADS * SW_HD
SW_KW = SW_KV * SW_HD
MB_HEADS = 8
MB_HD = 64
MB_W = MB_HEADS * MB_HD
MB_BLOCK = 256
MB_TOPK = 3
N_BUCKETS = 32
MAX_DIST = 128
N_ATT_HEADS = DA_HEADS + SW_HEADS + MB_HEADS
N_BRANCH = 3
QKV_W = 3 * DA_W + SW_QW + 2 * SW_KW + 3 * MB_W
EPS = 1e-6

LANES = 128
MXU_N = 256
BF16_ROWS = 16
ATT_T = 256
SW_T = WINDOW
NEG = -1e30
LOG2E = math.log2(math.e)
Q_SCALE = DA_HD ** -0.5 * LOG2E
VMEM_LIMIT = 56 * 1024 * 1024

_QA, _KA, _VA = 0, DA_W // LANES, 2 * DA_W // LANES
_QC = 3 * DA_W // LANES
_KC = _QC + MB_W // LANES
_VC = _KC + MB_W // LANES
_QB = 3 * DA_W + 3 * MB_W
_KB = (_QB + SW_QW) // LANES
_VB = _KB + 1
_SW_DIRECT = tuple(h for h in range(SW_HEADS) if h % 2 == h // (SW_HEADS // SW_KV))
_SW_SWAPPED = tuple(h for h in range(SW_HEADS) if h % 2 != h // (SW_HEADS // SW_KV))
_SW_ORDER = _SW_DIRECT + _SW_SWAPPED

F32 = jnp.float32
BF16 = jnp.bfloat16


def _pick(n, candidates):
    for c in candidates:
        if n % c == 0:
            return c
    raise ValueError(f"no tile in {candidates} divides {n}")


def _params(*sem):
    return pltpu.CompilerParams(dimension_semantics=sem, vmem_limit_bytes=VMEM_LIMIT)


def _rms(x, g):
    return x * lax.rsqrt(jnp.mean(x * x, axis=-1, keepdims=True) + EPS) * g


def _dot_t(a, b, **kw):
    return lax.dot_general(a, b, (((1,), (1,)), ((), ())), preferred_element_type=F32, **kw)


def _rel_bucket(dist):
    n = jnp.maximum(dist, 0)
    max_exact = N_BUCKETS // 2
    nf = jnp.maximum(n, 1).astype(F32)
    large = max_exact + (jnp.log(nf / max_exact) / math.log(MAX_DIST / max_exact)
                         * (N_BUCKETS - max_exact)).astype(jnp.int32)
    large = jnp.minimum(large, N_BUCKETS - 1)
    return jnp.where(n < max_exact, n, large)


def _bias_lookup(tab_ref, head, dist):
    bucket = _rel_bucket(dist)
    acc = jnp.zeros(dist.shape, F32)
    for b in range(N_BUCKETS):
        acc = jnp.where(bucket == b, tab_ref[b * N_ATT_HEADS + head], acc)
    return acc * LOG2E


FAR, NEAR, DIAG = 0, 1, 2


def _tile_kind(kj, qi):
    return jnp.clip(kj - qi + DIAG, FAR, DIAG)


def _bias_kernel(tab_ref, o_ref):
    e = pl.program_id(0)
    moba0 = DA_HEADS + SW_HEADS + 2 * (e - DA_HEADS)
    heads = (jnp.where(e < DA_HEADS, e, moba0), jnp.where(e < DA_HEADS, e, moba0 + 1))
    d = (lax.broadcasted_iota(jnp.int32, (ATT_T, ATT_T), 1)
         - lax.broadcasted_iota(jnp.int32, (ATT_T, ATT_T), 0))
    for c, head in enumerate(heads):
        cols = slice(c * ATT_T, (c + 1) * ATT_T)
        o_ref[0, FAR, :, cols] = _bias_lookup(tab_ref, head, d + 2 * ATT_T)
        o_ref[0, NEAR, :, cols] = _bias_lookup(tab_ref, head, d + ATT_T)
        o_ref[0, DIAG, :, cols] = jnp.where(d >= 0, _bias_lookup(tab_ref, head, d), NEG)


def _window_kernel(tab_ref, win_ref):
    c = pl.program_id(0)
    head = jnp.int32(DA_HEADS + _SW_ORDER[-1])
    for pos, h in enumerate(_SW_ORDER[:-1]):
        head = jnp.where(c == pos, DA_HEADS + h, head)
    d = (lax.broadcasted_iota(jnp.int32, (2 * SW_T, SW_T), 1) + SW_T
         - lax.broadcasted_iota(jnp.int32, (2 * SW_T, SW_T), 0))
    win_ref[...] = jnp.where((d >= 0) & (d < WINDOW), _bias_lookup(tab_ref, head, d), NEG)


def _bias_tiles(tab_flat):
    assert ATT_T + 1 >= MAX_DIST
    n = DA_HEADS + MB_HEADS // 2
    tiles = pl.pallas_call(
        _bias_kernel,
        grid=(n,),
        in_specs=[pl.BlockSpec(memory_space=pltpu.SMEM)],
        out_specs=pl.BlockSpec((1, 3, ATT_T, 2 * ATT_T), lambda e: (e, 0, 0, 0)),
        out_shape=jax.ShapeDtypeStruct((n, 3, ATT_T, 2 * ATT_T), F32),
        compiler_params=_params("arbitrary"),
        name="bias_tiles",
    )(tab_flat)
    win = pl.pallas_call(
        _window_kernel,
        grid=(SW_HEADS,),
        in_specs=[pl.BlockSpec(memory_space=pltpu.SMEM)],
        out_specs=pl.BlockSpec((2 * SW_T, SW_T), lambda c: (0, c)),
        out_shape=jax.ShapeDtypeStruct((2 * SW_T, SW_HEADS * SW_T), F32),
        compiler_params=_params("arbitrary"),
        name="window_tiles",
    )(tab_flat)
    return tiles, win


def _norm_kernel(x_ref, g_ref, o_ref):
    o_ref[...] = _rms(x_ref[...], g_ref[...]).astype(o_ref.dtype)


def _prenorm(x2, g):
    m, d = x2.shape
    tm = _pick(m, (512, 256, 128))
    return pl.pallas_call(
        _norm_kernel,
        grid=(m // tm,),
        in_specs=[pl.BlockSpec((tm, d), lambda i: (i, 0)), pl.BlockSpec((1, d), lambda i: (0, 0))],
        out_specs=pl.BlockSpec((tm, d), lambda i: (i, 0)),
        out_shape=jax.ShapeDtypeStruct((m, d), BF16),
        compiler_params=_params("arbitrary"),
        name="prenorm",
    )(x2, g.reshape(1, d))


def _proj_kernel(h_ref, *refs):
    w_refs, scale_ref, o_ref, wb_s = refs[:-3], refs[-3], refs[-2], refs[-1]

    @pl.when(pl.program_id(1) == 0)
    def _():
        for c, w_ref in enumerate(w_refs):
            cols = slice(c * PROJ_BLOCK, (c + 1) * PROJ_BLOCK)
            wb_s[:, cols] = (w_ref[...] * scale_ref[:, cols]).astype(BF16)

    o_ref[...] = jnp.dot(h_ref[...], wb_s[...], preferred_element_type=F32).astype(o_ref.dtype)


PROJ_BLOCK = 256
PROJ_BLOCKS_PER_STEP = 5


def _proj_source_block(jb):
    n_a, n_b, n_c = (3 * DA_W // PROJ_BLOCK, (SW_QW + 2 * SW_KW) // PROJ_BLOCK, 3 * MB_W // PROJ_BLOCK)
    return jnp.where(jb < n_a, jb, jnp.where(jb < n_a + n_c, jb + n_b, jb - n_c))


def _in_proj(h, w_in, layer):
    m, d = h.shape
    tm = _pick(m, (1024, 512, 256, 128))
    tn = PROJ_BLOCKS_PER_STEP * PROJ_BLOCK
    assert QKV_W % tn == 0 and (SW_QW + 2 * SW_KW) % PROJ_BLOCK == 0 and DA_W % PROJ_BLOCK == 0
    scale = np.ones((1, QKV_W), np.float32)
    for q0 in (_QA * LANES, _QC * LANES, _QB):
        scale[:, q0:q0 + DA_W] = Q_SCALE
    w_specs = [pl.BlockSpec((None, d, PROJ_BLOCK),
                            lambda j, i, c=c: (layer, 0, _proj_source_block(j * PROJ_BLOCKS_PER_STEP + c)))
               for c in range(PROJ_BLOCKS_PER_STEP)]
    return pl.pallas_call(
        _proj_kernel,
        grid=(QKV_W // tn, m // tm),
        in_specs=[pl.BlockSpec((tm, d), lambda j, i: (i, 0))] + w_specs + [pl.BlockSpec((1, tn), lambda j, i: (0, j))],
        out_specs=pl.BlockSpec((tm, tn), lambda j, i: (i, j)),
        out_shape=jax.ShapeDtypeStruct((m, QKV_W), BF16),
        scratch_shapes=[pltpu.VMEM((d, tn), BF16)],
        compiler_params=_params("arbitrary", "arbitrary"),
        name="in_proj",
    )(h, *([w_in] * PROJ_BLOCKS_PER_STEP), jnp.asarray(scale))


def _gate_kernel(h_ref, w_ref, b_ref, o_ref, wb_s):
    @pl.when(pl.program_id(1) == 0)
    def _():
        wb_s[...] = w_ref[...].astype(BF16)

    acc = jnp.dot(h_ref[...], wb_s[...], preferred_element_type=F32)
    o_ref[...] = jax.nn.sigmoid(acc + b_ref[...]).astype(o_ref.dtype)


def _in_gates(h, w_in, b_gate, layer):
    m, d = h.shape
    n = w_in.shape[2] - QKV_W
    tm = _pick(m, (1024, 512, 256, 128))
    tn = _pick(math.gcd(n, QKV_W), (768, 512, 256, 128))
    col0 = QKV_W // tn
    return pl.pallas_call(
        _gate_kernel,
        grid=(n // tn, m // tm),
        in_specs=[pl.BlockSpec((tm, d), lambda j, i: (i, 0)),
                  pl.BlockSpec((None, d, tn), lambda j, i: (layer, 0, col0 + j)),
                  pl.BlockSpec((None, 1, tn), lambda j, i: (layer, 0, j))],
        out_specs=pl.BlockSpec((tm, tn), lambda j, i: (i, j)),
        out_shape=jax.ShapeDtypeStruct((m, n), BF16),
        scratch_shapes=[pltpu.VMEM((d, tn), BF16)],
        compiler_params=_params("arbitrary", "arbitrary"),
        name="in_gates",
    )(h, w_in, b_gate.reshape(b_gate.shape[0], 1, n))


def _split_heads(q):
    lane = lax.broadcasted_iota(jnp.int32, q.shape, 1)
    zero = jnp.zeros_like(q)
    return jnp.concatenate([jnp.where(lane < LANES // 2, q, zero),
                            jnp.where(lane >= LANES // 2, q, zero)], axis=0)


def _merge_heads_t(o):
    t = o.shape[1] // 2
    feat = lax.broadcasted_iota(jnp.int32, (LANES, t), 0)
    return jnp.where(feat < LANES // 2, o[:, :t], o[:, t:]).T


def _pv_t(v, p):
    return lax.dot_general(v, p, (((0,), (0,)), ((), ())), preferred_element_type=F32)


def _softmax_stats(s, m, l, keep=None):
    m_tile = jnp.max(s, axis=0, keepdims=True)
    if keep is not None:
        m_tile = jnp.where(keep, m_tile, NEG)
    m_new = jnp.maximum(m, m_tile)
    alpha = jnp.exp2(m - m_new)
    p = jnp.exp2(s - (m_new if keep is None else jnp.where(keep, m_new, -NEG)))
    l = alpha * l + jnp.sum(p, axis=0, keepdims=True)
    return m_new, l, p.astype(BF16), alpha


def _flash(qi, streams):
    def stats(stream, kj, m, l):
        _, _, (s_s, p_s, _), keep_fn = stream
        m, l, p, alpha = _softmax_stats(s_s[...], m, l, None if keep_fn is None else keep_fn(kj))
        p_s[...] = p
        return m, l, alpha

    for logits_fn, _, (s_s, _, _), _ in streams:
        s_s[...] = logits_fn(0)
    carry = []
    for stream in streams:
        n_queries = stream[2][0].shape[1]
        carry.append(stats(stream, 0, jnp.full((1, n_queries), -jnp.inf, F32), jnp.zeros((1, n_queries), F32)))
    for logits_fn, _, (s_s, _, acc_s), _ in streams:
        s_s[...] = logits_fn(jnp.minimum(1, qi))
        acc_s[...] = jnp.zeros(acc_s.shape, F32)

    def body(kj, carry):
        out = []
        for stream, (m, l, alpha_prev) in zip(streams, carry):
            logits_fn, values_fn, (s_s, p_s, acc_s), _ = stream
            acc_s[...] = alpha_prev * acc_s[...] + _pv_t(values_fn(kj - 1), p_s[...])
            out.append(stats(stream, kj, m, l))
            s_s[...] = logits_fn(jnp.minimum(kj + 1, qi))
        return tuple(out)

    carry = lax.fori_loop(1, qi + 1, body, tuple(carry))
    return [(alpha * acc_s[...] + _pv_t(values_fn(qi), p_s[...])) / l
            for (_, values_fn, (_, p_s, acc_s), _), (_, l, alpha) in zip(streams, carry)]


N_STREAMS = 4


def _flash_scratch(n_queries):
    return N_STREAMS * [pltpu.VMEM((ATT_T, n_queries), F32), pltpu.VMEM((ATT_T, n_queries), BF16),
                        pltpu.VMEM((LANES, n_queries), F32)]


def _kv_block(ref, kj, group):
    return ref[pl.ds(pl.multiple_of(kj * ATT_T, ATT_T), ATT_T), group * LANES:(group + 1) * LANES]


def _diff_attn_kernel(lam_ref, g_ref, q_ref, k_ref, v_ref, bias_ref, o_ref, *scratch, lam_init):
    qi = pl.program_id(2)
    t = ATT_T

    def stream(g):
        q2 = _split_heads(q_ref[:, g * LANES:(g + 1) * LANES])

        def logits(kj):
            return _dot_t(_kv_block(k_ref, kj, g), q2) + bias_ref[g, _tile_kind(kj, qi)]

        return logits, lambda kj: _kv_block(v_ref, kj, g), scratch[3 * g:3 * g + 3], None

    outs = _flash(qi, [stream(g) for g in range(N_STREAMS)])
    lv = lam_ref[...]
    lam = (jnp.exp(jnp.sum(lv[0:1] * lv[1:2], axis=1, keepdims=True))
           - jnp.exp(jnp.sum(lv[2:3] * lv[3:4], axis=1, keepdims=True)) + lam_init)
    for g, o in enumerate(outs):
        o = o[:, :t] - lam * o[:, t:]
        o = o * lax.rsqrt(jnp.mean(o * o, axis=0, keepdims=True) + EPS) * (g_ref[...] * (1.0 - lam_init))
        o_ref[:, g * LANES:(g + 1) * LANES] = o.T.astype(o_ref.dtype)


def _diff_attention(lam_vecs, subln_g, qkv, tiles, batch, seq, lam_init):
    m = batch * seq
    nq = seq // ATT_T
    w = N_STREAMS * LANES
    return pl.pallas_call(
        functools.partial(_diff_attn_kernel, lam_init=lam_init),
        grid=(batch, DA_HEADS // N_STREAMS, nq),
        in_specs=[pl.BlockSpec((4, DA_HD), lambda b, h, i: (0, 0)),
                  pl.BlockSpec((2 * DA_HD, 1), lambda b, h, i: (0, 0)),
                  pl.BlockSpec((ATT_T, w), lambda b, h, i: (b * nq + i, _QA // N_STREAMS + h)),
                  pl.BlockSpec((seq, w), lambda b, h, i: (b, _KA // N_STREAMS + h)),
                  pl.BlockSpec((seq, w), lambda b, h, i: (b, _VA // N_STREAMS + h)),
                  pl.BlockSpec((N_STREAMS, 3, ATT_T, 2 * ATT_T), lambda b, h, i: (h, 0, 0, 0))],
        out_specs=pl.BlockSpec((ATT_T, w), lambda b, h, i: (b * nq + i, h)),
        out_shape=jax.ShapeDtypeStruct((m, DA_W), BF16),
        scratch_shapes=_flash_scratch(2 * ATT_T),
        compiler_params=_params("arbitrary", "arbitrary", "arbitrary"),
        name="diff_attention",
    )(lam_vecs, subln_g.astype(F32).reshape(2 * DA_HD, 1), qkv, qkv, qkv, tiles)


def _swap_halves(x):
    return pltpu.roll(x.astype(F32), LANES // 2, 1).astype(x.dtype)


def _swa_kernel(sink_ref, q_ref, kp_ref, ko_ref, vp_ref, vo_ref, bias_ref, o_ref):
    qi = pl.program_id(1)
    t = SW_T
    half = LANES // 2
    lane = lax.broadcasted_iota(jnp.int32, (t, LANES), 1)
    no_prev = jnp.where(qi > 0, 0.0, NEG)
    out = {}
    for s, heads in enumerate((_SW_DIRECT, _SW_SWAPPED)):
        kp, ko, vp, vo = (r[...] if s == 0 else _swap_halves(r[...]) for r in (kp_ref, ko_ref, vp_ref, vo_ref))
        qs = []
        for h in heads:
            q = q_ref[:, (h // 2) * LANES:(h // 2 + 1) * LANES]
            qs.append(jnp.where(lane >= half if h % 2 else lane < half, q, jnp.zeros_like(q)))
        qs = jnp.concatenate(qs, axis=0)
        cols = slice(s * len(heads) * t, (s + 1) * len(heads) * t)
        s_prev = _dot_t(kp, qs) + bias_ref[0:t, cols] + no_prev
        s_own = _dot_t(ko, qs) + bias_ref[t:2 * t, cols]
        sink = sink_ref[:, cols]
        m = jnp.maximum(jnp.maximum(jnp.max(s_prev, axis=0, keepdims=True),
                                    jnp.max(s_own, axis=0, keepdims=True)), sink)
        p_prev = jnp.exp2(s_prev - m)
        p_own = jnp.exp2(s_own - m)
        l = (jnp.sum(p_prev, axis=0, keepdims=True) + jnp.sum(p_own, axis=0, keepdims=True)
             + jnp.exp2(sink - m))
        o = (_pv_t(vp, p_prev.astype(BF16)) + _pv_t(vo, p_own.astype(BF16))) / l
        for c, h in enumerate(heads):
            out[h] = o[:, c * t:(c + 1) * t]
    feat = lax.broadcasted_iota(jnp.int32, (LANES, t), 0)
    for g in range(SW_QW // LANES):
        o_ref[:, g * LANES:(g + 1) * LANES] = jnp.where(feat < half, out[2 * g], out[2 * g + 1]).T.astype(o_ref.dtype)


def _swa_attention(sink_row, qkv, win, batch, seq):
    m = batch * seq
    nq = seq // SW_T
    cols = SW_HEADS * SW_T
    prev = lambda b, i: (b * nq + jnp.maximum(i - 1, 0), _KB)
    own = lambda b, i: (b * nq + i, _KB)
    prev_v = lambda b, i: (b * nq + jnp.maximum(i - 1, 0), _VB)
    own_v = lambda b, i: (b * nq + i, _VB)
    return pl.pallas_call(
        _swa_kernel,
        grid=(batch, nq),
        in_specs=[pl.BlockSpec((1, cols), lambda b, i: (0, 0)),
                  pl.BlockSpec((SW_T, SW_QW), lambda b, i: (b * nq + i, _QB // SW_QW)),
                  pl.BlockSpec((SW_T, LANES), prev),
                  pl.BlockSpec((SW_T, LANES), own),
                  pl.BlockSpec((SW_T, LANES), prev_v),
                  pl.BlockSpec((SW_T, LANES), own_v),
                  pl.BlockSpec((2 * SW_T, cols), lambda b, i: (0, 0))],
        out_specs=pl.BlockSpec((SW_T, SW_QW), lambda b, i: (b * nq + i, 0)),
        out_shape=jax.ShapeDtypeStruct((m, SW_QW), BF16),
        compiler_params=_params("arbitrary", "arbitrary"),
        name="swa_attention",
    )(sink_row, qkv, qkv, qkv, qkv, qkv, win)


def _moba_kernel(q_ref, k_ref, v_ref, bias_ref, o_ref, kmean_s, *scratch, n_blocks):
    qi = pl.program_id(2)
    stride = kmean_s.shape[0] // 3

    @pl.when(qi == 0)
    def _():
        kf = k_ref[...].astype(F32).reshape(n_blocks, MB_BLOCK, N_STREAMS * LANES)
        rest = jnp.sum(kf, axis=1) * (1.0 / MB_BLOCK)
        kmean_s[...] = jnp.zeros(kmean_s.shape, BF16)
        for part in range(3):
            term = rest.astype(BF16)
            kmean_s[part * stride:part * stride + n_blocks, :] = term
            rest = rest - term.astype(F32)

    def stream(g):
        q2 = _split_heads(q_ref[:, g * LANES:(g + 1) * LANES])
        parts = _dot_t(kmean_s[:, g * LANES:(g + 1) * LANES], q2)
        gate = parts[0:n_blocks] + parts[stride:stride + n_blocks] + parts[2 * stride:2 * stride + n_blocks]
        blk = lax.broadcasted_iota(jnp.int32, gate.shape, 0)

        def selected(kj):
            g_kj = jnp.sum(jnp.where(blk == kj, gate, 0.0), axis=0, keepdims=True)
            beats = ((gate > g_kj) | ((gate == g_kj) & (blk < kj))) & (blk < qi)
            rank = jnp.sum(beats.astype(F32), axis=0, keepdims=True)
            return (rank < MB_TOPK) | (kj >= qi)

        def logits(kj):
            return _dot_t(_kv_block(k_ref, kj, g), q2) + bias_ref[g, _tile_kind(kj, qi)]

        return logits, lambda kj: _kv_block(v_ref, kj, g), scratch[3 * g:3 * g + 3], selected

    outs = _flash(qi, [stream(g) for g in range(N_STREAMS)])
    for g, o in enumerate(outs):
        o_ref[:, g * LANES:(g + 1) * LANES] = _merge_heads_t(o).astype(o_ref.dtype)


def _moba_attention(qkv, tiles, batch, seq):
    m = batch * seq
    nq = seq // ATT_T
    n_blocks = seq // MB_BLOCK
    w = N_STREAMS * LANES
    tile0 = DA_HEADS // N_STREAMS
    return pl.pallas_call(
        functools.partial(_moba_kernel, n_blocks=n_blocks),
        grid=(batch, MB_HEADS // 2 // N_STREAMS, nq),
        in_specs=[pl.BlockSpec((ATT_T, w), lambda b, h, i: (b * nq + i, _QC // N_STREAMS + h)),
                  pl.BlockSpec((seq, w), lambda b, h, i: (b, _KC // N_STREAMS + h)),
                  pl.BlockSpec((seq, w), lambda b, h, i: (b, _VC // N_STREAMS + h)),
                  pl.BlockSpec((N_STREAMS, 3, ATT_T, 2 * ATT_T), lambda b, h, i: (tile0 + h, 0, 0, 0))],
        out_specs=pl.BlockSpec((ATT_T, w), lambda b, h, i: (b * nq + i, h)),
        out_shape=jax.ShapeDtypeStruct((m, MB_W), BF16),
        scratch_shapes=[pltpu.VMEM((3 * pl.cdiv(n_blocks, BF16_ROWS) * BF16_ROWS, w), BF16)]
        + _flash_scratch(2 * ATT_T),
        compiler_params=_params("arbitrary", "arbitrary", "arbitrary"),
        name="moba_attention",
    )(qkv, qkv, qkv, tiles)


def _mix_out_kernel(ya_ref, yb_ref, yc_ref, ga_ref, gb_ref, gc_ref, x_ref, woa_ref, wob_ref, woc_ref,
                    wout_ref, pg_ref, ng_ref, xo_ref, ho_ref):
    def branch(y_ref, w_ref, g_ref, rows):
        return g_ref[rows, :].astype(F32) * jnp.dot(y_ref[rows, :], w_ref[...], preferred_element_type=F32)

    tm = x_ref.shape[0]
    sub = MXU_N if tm % MXU_N == 0 else tm
    for r0 in range(0, tm, sub):
        rows = slice(r0, r0 + sub)
        mix = (branch(ya_ref, woa_ref, ga_ref, rows) + branch(yb_ref, wob_ref, gb_ref, rows)
               + branch(yc_ref, woc_ref, gc_ref, rows))
        z = jnp.dot(mix.astype(BF16), wout_ref[...], preferred_element_type=F32)
        xn = x_ref[rows, :] + _rms(z, pg_ref[...])
        xo_ref[rows, :] = xn
        ho_ref[rows, :] = _rms(xn, ng_ref[...]).astype(ho_ref.dtype)


def _mix_out(ya, yb, yc, gates, x2, woa, wob, woc, wout, post_g, next_g):
    m, d = x2.shape
    tm = _pick(m, (512, 256, 128))
    row = lambda i: (i, 0)
    const = lambda i: (0, 0)
    once = pl.Buffered(1)
    return pl.pallas_call(
        _mix_out_kernel,
        grid=(m // tm,),
        in_specs=[pl.BlockSpec((tm, DA_W), row), pl.BlockSpec((tm, SW_QW), row), pl.BlockSpec((tm, MB_W), row),
                  pl.BlockSpec((tm, d), lambda i: (i, 0)), pl.BlockSpec((tm, d), lambda i: (i, 1)),
                  pl.BlockSpec((tm, d), lambda i: (i, 2)),
                  pl.BlockSpec((tm, d), row),
                  pl.BlockSpec((DA_W, d), const, pipeline_mode=once),
                  pl.BlockSpec((SW_QW, d), const, pipeline_mode=once),
                  pl.BlockSpec((MB_W, d), const, pipeline_mode=once),
                  pl.BlockSpec((d, d), const, pipeline_mode=once),
                  pl.BlockSpec((1, d), const), pl.BlockSpec((1, d), const)],
        out_specs=[pl.BlockSpec((tm, d), row), pl.BlockSpec((tm, d), row)],
        out_shape=[jax.ShapeDtypeStruct((m, d), F32), jax.ShapeDtypeStruct((m, d), BF16)],
        compiler_params=_params("arbitrary"),
        name="mix_out",
    )(ya, yb, yc, gates, gates, gates, x2, woa, wob, woc, wout, post_g.reshape(1, d), next_g.reshape(1, d))


CONV_W = 3
HALO = 8


def _ffn_up_kernel(h_ref, wgf_ref, wvf_ref, cwg_ref, cwv_ref, cbg_ref, cbv_ref, o_ref, ug_s, uv_s, wg_ref, wv_ref,
                   *, tiles_per_seq):
    i = pl.program_id(1)
    tm = h_ref.shape[0]

    @pl.when(i == 0)
    def _():
        wg_ref[...] = wgf_ref[...].astype(BF16)
        wv_ref[...] = wvf_ref[...].astype(BF16)

    @pl.when(i % tiles_per_seq == 0)
    def _():
        ug_s[0:HALO, :] = jnp.zeros((HALO, ug_s.shape[1]), F32)
        uv_s[0:HALO, :] = jnp.zeros((HALO, uv_s.shape[1]), F32)

    h = h_ref[...]

    def conv(w_ref, cw_ref, cb_ref, u_s):
        u = jnp.dot(h, w_ref[...], preferred_element_type=F32)
        u_s[HALO:HALO + tm, :] = u
        cw = cw_ref[...]
        c = cb_ref[...] + u_s[HALO - 2:HALO - 2 + tm, :] * cw[0:1]
        c = c + u_s[HALO - 1:HALO - 1 + tm, :] * cw[1:2]
        c = c + u * cw[2:3]
        u_s[0:HALO, :] = u[tm - HALO:tm]
        return c

    gate = conv(wg_ref, cwg_ref, cbg_ref, ug_s)
    val = conv(wv_ref, cwv_ref, cbv_ref, uv_s)
    o_ref[...] = (jax.nn.gelu(gate, approximate=True) * val).astype(o_ref.dtype)


def _ffn_up(h, w_up, conv_w, conv_b, layer, seq):
    m, d = h.shape
    f = w_up.shape[2] // 2
    tm = _pick(seq, (1024, 512, 256, 128))
    tn = _pick(f, (512, 384, 256, 128))
    nj = f // tn
    gate_col = lambda j, i: (layer, 0, j)
    val_col = lambda j, i: (layer, 0, nj + j)
    conv_b = conv_b.reshape(conv_b.shape[0], 1, 2 * f)
    return pl.pallas_call(
        functools.partial(_ffn_up_kernel, tiles_per_seq=seq // tm),
        grid=(nj, m // tm),
        in_specs=[pl.BlockSpec((tm, d), lambda j, i: (i, 0)),
                  pl.BlockSpec((None, d, tn), gate_col), pl.BlockSpec((None, d, tn), val_col),
                  pl.BlockSpec((None, CONV_W, tn), gate_col), pl.BlockSpec((None, CONV_W, tn), val_col),
                  pl.BlockSpec((None, 1, tn), gate_col), pl.BlockSpec((None, 1, tn), val_col)],
        out_specs=pl.BlockSpec((tm, tn), lambda j, i: (i, j)),
        out_shape=jax.ShapeDtypeStruct((m, f), BF16),
        scratch_shapes=[pltpu.VMEM((HALO + tm, tn), F32), pltpu.VMEM((HALO + tm, tn), F32),
                        pltpu.VMEM((d, tn), BF16), pltpu.VMEM((d, tn), BF16)],
        compiler_params=_params("arbitrary", "arbitrary"),
        name="ffn_up",
    )(h, w_up, w_up, conv_w, conv_w, conv_b, conv_b)


def _ffn_down_kernel(a_ref, w_ref, x_ref, pg_ref, ng_ref, xo_ref, *ho_ref):
    k = pl.program_id(1)
    last = pl.num_programs(1) - 1

    @pl.when(k == 0)
    def _():
        xo_ref[...] = jnp.zeros(xo_ref.shape, F32)

    @pl.when(k < last)
    def _():
        xo_ref[...] += jnp.dot(a_ref[...], w_ref[...], preferred_element_type=F32)

    @pl.when(k == last)
    def _():
        tm = x_ref.shape[0]
        sub = MXU_N if tm % MXU_N == 0 else tm
        for r0 in range(0, tm, sub):
            rows = slice(r0, r0 + sub)
            z = xo_ref[rows, :] + jnp.dot(a_ref[rows, :], w_ref[...], preferred_element_type=F32)
            xn = x_ref[rows, :] + _rms(z, pg_ref[...])
            xo_ref[rows, :] = xn
            if ho_ref:
                ho_ref[0][rows, :] = _rms(xn, ng_ref[...]).astype(ho_ref[0].dtype)


def _ffn_down(a, w_down, x2, post_g, next_g):
    m, d = x2.shape
    f = a.shape[1]
    tm = _pick(m, (1024, 512, 256, 128))
    tk = _pick(f, (1408, 1024, 512, 384, 256, 128))
    emit_next = next_g is not None
    row = lambda i, k: (i, 0)
    const = lambda i, k: (0, 0)
    once = pl.Buffered(1)
    out_specs = [pl.BlockSpec((tm, d), row, pipeline_mode=once)]
    out_shape = [jax.ShapeDtypeStruct((m, d), F32)]
    if emit_next:
        out_specs.append(pl.BlockSpec((tm, d), row, pipeline_mode=once))
        out_shape.append(jax.ShapeDtypeStruct((m, d), BF16))
    ng = next_g if emit_next else post_g
    res = pl.pallas_call(
        _ffn_down_kernel,
        grid=(m // tm, f // tk),
        in_specs=[pl.BlockSpec((tm, tk), lambda i, k: (i, k)), pl.BlockSpec((tk, d), lambda i, k: (k, 0)),
                  pl.BlockSpec((tm, d), row, pipeline_mode=once),
                  pl.BlockSpec((1, d), const), pl.BlockSpec((1, d), const)],
        out_specs=out_specs,
        out_shape=out_shape,
        compiler_params=_params("arbitrary", "arbitrary"),
        name="ffn_down",
    )(a, w_down, x2, post_g.reshape(1, d), ng.reshape(1, d))
    return (res[0], res[1]) if emit_next else (res[0], None)


@jax.jit
def _trunk(x, rel_bias_table, w_in, b_gate, lam_q1, lam_k1, lam_q2, lam_k2, diff_subln_g, sinks, w_oa, w_ob,
           w_oc, w_out, pre_mix_g, post_mix_g, pre_ffn_g, post_ffn_g, w_up, conv_w, conv_b, w_down):
    batch, seq, d = x.shape
    depth = w_in.shape[0]
    assert seq % ATT_T == 0 and seq % MB_BLOCK == 0 and d % LANES == 0
    assert w_in.shape[2] == QKV_W + N_BRANCH * d
    m = batch * seq

    tab_flat = rel_bias_table.astype(F32).reshape(-1)
    tiles, win = _bias_tiles(tab_flat)
    sw_order = jnp.array(_SW_ORDER)

    x2 = x.reshape(m, d)
    h = _prenorm(x2, pre_mix_g[0])
    for l in range(depth):
        qkv = _in_proj(h, w_in, l)
        gates = _in_gates(h, w_in, b_gate, l)

        lam_init = 0.8 - 0.6 * math.exp(-0.3 * l)
        lam_vecs = jnp.stack([lam_q1[l], lam_k1[l], lam_q2[l], lam_k2[l]]).astype(F32)
        ya = _diff_attention(lam_vecs, diff_subln_g[l], qkv, tiles, batch, seq, lam_init)
        sink_row = jnp.repeat(sinks[l].astype(F32)[sw_order] * LOG2E, SW_T).reshape(1, SW_HEADS * SW_T)
        yb = _swa_attention(sink_row, qkv, win, batch, seq)
        yc = _moba_attention(qkv, tiles, batch, seq)

        x2, h = _mix_out(ya, yb, yc, gates, x2, w_oa[l].astype(BF16), w_ob[l].astype(BF16),
                         w_oc[l].astype(BF16), w_out[l].astype(BF16), post_mix_g[l], pre_ffn_g[l])

        a = _ffn_up(h, w_up, conv_w, conv_b, l, seq)
        next_g = pre_mix_g[l + 1] if l + 1 < depth else None
        x2, h = _ffn_down(a, w_down[l].astype(BF16), x2, post_ffn_g[l], next_g)
    return x2.reshape(batch, seq, d)


def kernel(x, rel_bias_table, w_in, b_gate, lam_q1, lam_k1, lam_q2, lam_k2, diff_subln_g, sinks, w_oa, w_ob, w_oc, w_out, pre_mix_g, post_mix_g, pre_ffn_g, post_ffn_g, w_up, conv_w, conv_b, w_down):
    return _trunk(x, rel_bias_table, w_in, b_gate, lam_q1, lam_k1, lam_q2, lam_k2, diff_subln_g, sinks, w_oa, w_ob,
                  w_oc, w_out, pre_mix_g, post_mix_g, pre_ffn_g, post_ffn_g, w_up, conv_w, conv_b, w_down)
```

```python
import functools
import math

import numpy as np
import jax
import jax.numpy as jnp
from jax import lax
from jax.experimental import pallas as pl
from jax.experimental.pallas import tpu as pltpu

DA_HEADS = 4
DA_HD = 64
DA_W = DA_HEADS * 2 * DA_HD
SW_HEADS = 8
SW_KV = 2
SW_HD = 64
WINDOW = 128
SW_QW = SW_HEADS * SW_HD
SW_KW = SW_KV * SW_HD
MB_HEADS = 8
MB_HD = 64
MB_W = MB_HEADS * MB_HD
MB_BLOCK = 256
MB_TOPK = 3
N_BUCKETS = 32
MAX_DIST = 128
N_ATT_HEADS = DA_HEADS + SW_HEADS + MB_HEADS
N_BRANCH = 3
QKV_W = 3 * DA_W + SW_QW + 2 * SW_KW + 3 * MB_W
EPS = 1e-6

LANES = 128
MXU_N = 256
BF16_ROWS = 16
ATT_T = 256
SW_T = WINDOW
NEG = -1e30
LOG2E = math.log2(math.e)
Q_SCALE = DA_HD ** -0.5 * LOG2E
VMEM_LIMIT = 56 * 1024 * 1024

_QA, _KA, _VA = 0, DA_W // LANES, 2 * DA_W // LANES
_QC = 3 * DA_W // LANES
_KC = _QC + MB_W // LANES
_VC = _KC + MB_W // LANES
_QB = 3 * DA_W + 3 * MB_W
_KB = (_QB + SW_QW) // LANES
_VB = _KB + 1
_SW_DIRECT = tuple(h for h in range(SW_HEADS) if h % 2 == h // (SW_HEADS // SW_KV))
_SW_SWAPPED = tuple(h for h in range(SW_HEADS) if h % 2 != h // (SW_HEADS // SW_KV))
_SW_ORDER = _SW_DIRECT + _SW_SWAPPED

F32 = jnp.float32
BF16 = jnp.bfloat16


def _pick(n, candidates):
    for c in candidates:
        if n % c == 0:
            return c
    raise ValueError(f"no tile in {candidates} divides {n}")


def _params(*sem):
    return pltpu.CompilerParams(dimension_semantics=sem, vmem_limit_bytes=VMEM_LIMIT)


def _rms(x, g):
    return x * lax.rsqrt(jnp.mean(x * x, axis=-1, keepdims=True) + EPS) * g


def _dot_t(a, b, **kw):
    return lax.dot_general(a, b, (((1,), (1,)), ((), ())), preferred_element_type=F32, **kw)


def _rel_bucket(dist):
    n = jnp.maximum(dist, 0)
    max_exact = N_BUCKETS // 2
    nf = jnp.maximum(n, 1).astype(F32)
    large = max_exact + (jnp.log(nf / max_exact) / math.log(MAX_DIST / max_exact)
                         * (N_BUCKETS - max_exact)).astype(jnp.int32)
    large = jnp.minimum(large, N_BUCKETS - 1)
    return jnp.where(n < max_exact, n, large)


def _bias_lookup(tab_ref, head, dist):
    bucket = _rel_bucket(dist)
    acc = jnp.zeros(dist.shape, F32)
    for b in range(N_BUCKETS):
        acc = jnp.where(bucket == b, tab_ref[b * N_ATT_HEADS + head], acc)
    return acc * LOG2E


FAR, NEAR, DIAG = 0, 1, 2


def _tile_kind(kj, qi):
    return jnp.clip(kj - qi + DIAG, FAR, DIAG)


def _bias_kernel(tab_ref, o_ref):
    e = pl.program_id(0)
    moba0 = DA_HEADS + SW_HEADS + 2 * (e - DA_HEADS)
    heads = (jnp.where(e < DA_HEADS, e, moba0), jnp.where(e < DA_HEADS, e, moba0 + 1))
    d = (lax.broadcasted_iota(jnp.int32, (ATT_T, ATT_T), 1)
         - lax.broadcasted_iota(jnp.int32, (ATT_T, ATT_T), 0))
    half = ATT_T // 2
    for c, head in enumerate(heads):
        cols = slice(c * ATT_T, (c + 1) * ATT_T)
        last = tab_ref[(N_BUCKETS - 1) * N_ATT_HEADS + head] * LOG2E
        o_ref[0, FAR, :, cols] = jnp.full((ATT_T, ATT_T), last, F32)
        o_ref[0, NEAR, 0:half, cols] = jnp.full((half, ATT_T), last, F32)
        o_ref[0, NEAR, half:ATT_T, cols] = _bias_lookup(tab_ref, head, d[half:] + ATT_T)
        o_ref[0, DIAG, :, cols] = jnp.where(d >= 0, _bias_lookup(tab_ref, head, d), NEG)


def _window_kernel(tab_ref, win_ref):
    c = pl.program_id(0)
    head = jnp.int32(DA_HEADS + _SW_ORDER[-1])
    for pos, h in enumerate(_SW_ORDER[:-1]):
        head = jnp.where(c == pos, DA_HEADS + h, head)
    d = (lax.broadcasted_iota(jnp.int32, (2 * SW_T, SW_T), 1) + SW_T
         - lax.broadcasted_iota(jnp.int32, (2 * SW_T, SW_T), 0))
    win_ref[...] = jnp.where((d >= 0) & (d < WINDOW), _bias_lookup(tab_ref, head, d), NEG)


def _bias_tiles(tab_flat):
    assert ATT_T // 2 + 1 >= MAX_DIST
    n = DA_HEADS + MB_HEADS // 2
    tiles = pl.pallas_call(
        _bias_kernel,
        grid=(n,),
        in_specs=[pl.BlockSpec(memory_space=pltpu.SMEM)],
        out_specs=pl.BlockSpec((1, 3, ATT_T, 2 * ATT_T), lambda e: (e, 0, 0, 0)),
        out_shape=jax.ShapeDtypeStruct((n, 3, ATT_T, 2 * ATT_T), F32),
        compiler_params=_params("arbitrary"),
        name="bias_tiles",
    )(tab_flat)
    win = pl.pallas_call(
        _window_kernel,
        grid=(SW_HEADS,),
        in_specs=[pl.BlockSpec(memory_space=pltpu.SMEM)],
        out_specs=pl.BlockSpec((2 * SW_T, SW_T), lambda c: (0, c)),
        out_shape=jax.ShapeDtypeStruct((2 * SW_T, SW_HEADS * SW_T), F32),
        compiler_params=_params("arbitrary"),
        name="window_tiles",
    )(tab_flat)
    return tiles, win


def _norm_kernel(x_ref, g_ref, o_ref):
    o_ref[...] = _rms(x_ref[...], g_ref[...]).astype(o_ref.dtype)


def _prenorm(x2, g):
    m, d = x2.shape
    tm = _pick(m, (512, 256, 128))
    return pl.pallas_call(
        _norm_kernel,
        grid=(m // tm,),
        in_specs=[pl.BlockSpec((tm, d), lambda i: (i, 0)), pl.BlockSpec((1, d), lambda i: (0, 0))],
        out_specs=pl.BlockSpec((tm, d), lambda i: (i, 0)),
        out_shape=jax.ShapeDtypeStruct((m, d), BF16),
        compiler_params=_params("arbitrary"),
        name="prenorm",
    )(x2, g.reshape(1, d))


def _proj_kernel(h_ref, *refs):
    w_refs, scale_ref, o_ref, wb_s = refs[:-3], refs[-3], refs[-2], refs[-1]

    @pl.when(pl.program_id(1) == 0)
    def _():
        for c, w_ref in enumerate(w_refs):
            cols = slice(c * PROJ_BLOCK, (c + 1) * PROJ_BLOCK)
            wb_s[:, cols] = (w_ref[...] * scale_ref[:, cols]).astype(BF16)

    o_ref[...] = jnp.dot(h_ref[...], wb_s[...], preferred_element_type=F32).astype(o_ref.dtype)


PROJ_BLOCK = 256
PROJ_BLOCKS_PER_STEP = 5


def _proj_source_block(jb):
    n_a, n_b, n_c = (3 * DA_W // PROJ_BLOCK, (SW_QW + 2 * SW_KW) // PROJ_BLOCK, 3 * MB_W // PROJ_BLOCK)
    return jnp.where(jb < n_a, jb, jnp.where(jb < n_a + n_c, jb + n_b, jb - n_c))


def _in_proj(h, w_in, layer):
    m, d = h.shape
    tm = _pick(m, (1024, 512, 256, 128))
    tn = PROJ_BLOCKS_PER_STEP * PROJ_BLOCK
    assert QKV_W % tn == 0 and (SW_QW + 2 * SW_KW) % PROJ_BLOCK == 0 and DA_W % PROJ_BLOCK == 0
    scale = np.ones((1, QKV_W), np.float32)
    for q0 in (_QA * LANES, _QC * LANES, _QB):
        scale[:, q0:q0 + DA_W] = Q_SCALE
    w_specs = [pl.BlockSpec((None, d, PROJ_BLOCK),
                            lambda j, i, c=c: (layer, 0, _proj_source_block(j * PROJ_BLOCKS_PER_STEP + c)))
               for c in range(PROJ_BLOCKS_PER_STEP)]
    return pl.pallas_call(
        _proj_kernel,
        grid=(QKV_W // tn, m // tm),
        in_specs=[pl.BlockSpec((tm, d), lambda j, i: (i, 0))] + w_specs + [pl.BlockSpec((1, tn), lambda j, i: (0, j))],
        out_specs=pl.BlockSpec((tm, tn), lambda j, i: (i, j)),
        out_shape=jax.ShapeDtypeStruct((m, QKV_W), BF16),
        scratch_shapes=[pltpu.VMEM((d, tn), BF16)],
        compiler_params=_params("arbitrary", "arbitrary"),
        name="in_proj",
    )(h, *([w_in] * PROJ_BLOCKS_PER_STEP), jnp.asarray(scale))


def _gate_kernel(h_ref, w_ref, b_ref, o_ref, wb_s):
    @pl.when(pl.program_id(1) == 0)
    def _():
        wb_s[...] = w_ref[...].astype(BF16)

    acc = jnp.dot(h_ref[...], wb_s[...], preferred_element_type=F32)
    o_ref[...] = (0.5 * jnp.tanh(0.5 * (acc + b_ref[...])) + 0.5).astype(o_ref.dtype)


def _in_gates(h, w_in, b_gate, layer):
    m, d = h.shape
    n = w_in.shape[2] - QKV_W
    tm = _pick(m, (1024, 512, 256, 128))
    tn = _pick(math.gcd(n, QKV_W), (768, 512, 256, 128))
    col0 = QKV_W // tn
    return pl.pallas_call(
        _gate_kernel,
        grid=(n // tn, m // tm),
        in_specs=[pl.BlockSpec((tm, d), lambda j, i: (i, 0)),
                  pl.BlockSpec((None, d, tn), lambda j, i: (layer, 0, col0 + j)),
                  pl.BlockSpec((None, 1, tn), lambda j, i: (layer, 0, j))],
        out_specs=pl.BlockSpec((tm, tn), lambda j, i: (i, j)),
        out_shape=jax.ShapeDtypeStruct((m, n), BF16),
        scratch_shapes=[pltpu.VMEM((d, tn), BF16)],
        compiler_params=_params("arbitrary", "arbitrary"),
        name="in_gates",
    )(h, w_in, b_gate.reshape(b_gate.shape[0], 1, n))


def _split_heads(q):
    lane = lax.broadcasted_iota(jnp.int32, q.shape, 1)
    zero = jnp.zeros_like(q)
    return jnp.concatenate([jnp.where(lane < LANES // 2, q, zero),
                            jnp.where(lane >= LANES // 2, q, zero)], axis=0)


def _merge_heads_t(o):
    t = o.shape[1] // 2
    feat = lax.broadcasted_iota(jnp.int32, (LANES, t), 0)
    return jnp.where(feat < LANES // 2, o[:, :t], o[:, t:]).T


def _pv_t(v, p):
    return lax.dot_general(v, p, (((0,), (0,)), ((), ())), preferred_element_type=F32)


def _softmax_stats(s, m, keep=None):
    m_tile = jnp.max(s, axis=0, keepdims=True)
    if keep is not None:
        m_tile = jnp.where(keep, m_tile, NEG)
    m_new = jnp.maximum(m, m_tile)
    alpha = jnp.exp2(m - m_new)
    p = jnp.exp2(s - (m_new if keep is None else jnp.where(keep, m_new, -NEG)))
    return m_new, p.astype(BF16), alpha


VT_ROWS = LANES + BF16_ROWS


def _store_values_t(v_ref, group, vt_s):
    for kj in range(vt_s.shape[0]):
        v = v_ref[kj * ATT_T:(kj + 1) * ATT_T, group * LANES:(group + 1) * LANES]
        vt_s[kj, 0:LANES, :] = v.astype(F32).T.astype(BF16)
        vt_s[kj, LANES:VT_ROWS, :] = jnp.ones((VT_ROWS - LANES, ATT_T), BF16)


def _flash(qi, streams):
    def stats(stream, kj, m):
        _, (s_s, p_s, _, _), keep_fn = stream
        m, p, alpha = _softmax_stats(s_s[...], m, None if keep_fn is None else keep_fn(kj))
        p_s[...] = p
        return m, alpha

    def value_product(stream, kj):
        _, (_, p_s, _, vt_s), _ = stream
        return jnp.dot(vt_s[kj], p_s[...], preferred_element_type=F32)

    for logits_fn, (s_s, _, _, _), _ in streams:
        s_s[...] = logits_fn(0)
    carry = [stats(stream, 0, jnp.full((1, stream[1][0].shape[1]), -jnp.inf, F32)) for stream in streams]
    for logits_fn, (s_s, _, acc_s, _), _ in streams:
        s_s[...] = logits_fn(jnp.minimum(1, qi))
        acc_s[...] = jnp.zeros(acc_s.shape, F32)

    def body(kj, carry):
        out = []
        for stream, (m, alpha_prev) in zip(streams, carry):
            logits_fn, (s_s, _, acc_s, _), _ = stream
            acc_s[...] = alpha_prev * acc_s[...] + value_product(stream, kj - 1)
            out.append(stats(stream, kj, m))
            s_s[...] = logits_fn(jnp.minimum(kj + 1, qi))
        return tuple(out)

    carry = lax.fori_loop(1, qi + 1, body, tuple(carry))
    outs = []
    for stream, (_, alpha) in zip(streams, carry):
        acc = alpha * stream[1][2][...] + value_product(stream, qi)
        outs.append(acc[0:LANES] / acc[LANES:LANES + 1])
    return outs


N_STREAMS = 4


SCRATCH_PER_STREAM = 4


def _flash_scratch(n_queries, n_tiles):
    return N_STREAMS * [pltpu.VMEM((ATT_T, n_queries), F32), pltpu.VMEM((ATT_T, n_queries), BF16),
                        pltpu.VMEM((VT_ROWS, n_queries), F32), pltpu.VMEM((n_tiles, VT_ROWS, ATT_T), BF16)]


def _kv_block(ref, kj, group):
    return ref[pl.ds(pl.multiple_of(kj * ATT_T, ATT_T), ATT_T), group * LANES:(group + 1) * LANES]


def _diff_attn_kernel(lam_ref, g_ref, q_ref, k_ref, v_ref, bias_ref, o_ref, *scratch, lam_init):
    qi = pl.program_id(2)
    t = ATT_T
    scratch = [scratch[SCRATCH_PER_STREAM * g:SCRATCH_PER_STREAM * (g + 1)] for g in range(N_STREAMS)]

    @pl.when(qi == 0)
    def _():
        for g in range(N_STREAMS):
            _store_values_t(v_ref, g, scratch[g][3])

    def stream(g):
        q2 = _split_heads(q_ref[:, g * LANES:(g + 1) * LANES])

        def logits(kj):
            return _dot_t(_kv_block(k_ref, kj, g), q2) + bias_ref[g, _tile_kind(kj, qi)]

        return logits, scratch[g], None

    outs = _flash(qi, [stream(g) for g in range(N_STREAMS)])
    lv = lam_ref[...]
    lam = (jnp.exp(jnp.sum(lv[0:1] * lv[1:2], axis=1, keepdims=True))
           - jnp.exp(jnp.sum(lv[2:3] * lv[3:4], axis=1, keepdims=True)) + lam_init)
    for g, o in enumerate(outs):
        o = o[:, :t] - lam * o[:, t:]
        o = o * lax.rsqrt(jnp.mean(o * o, axis=0, keepdims=True) + EPS) * (g_ref[...] * (1.0 - lam_init))
        o_ref[:, g * LANES:(g + 1) * LANES] = o.T.astype(o_ref.dtype)


def _diff_attention(lam_vecs, subln_g, qkv, tiles, batch, seq, lam_init):
    m = batch * seq
    nq = seq // ATT_T
    w = N_STREAMS * LANES
    return pl.pallas_call(
        functools.partial(_diff_attn_kernel, lam_init=lam_init),
        grid=(batch, DA_HEADS // N_STREAMS, nq),
        in_specs=[pl.BlockSpec((4, DA_HD), lambda b, h, i: (0, 0)),
                  pl.BlockSpec((2 * DA_HD, 1), lambda b, h, i: (0, 0)),
                  pl.BlockSpec((ATT_T, w), lambda b, h, i: (b * nq + i, _QA // N_STREAMS + h)),
                  pl.BlockSpec((seq, w), lambda b, h, i: (b, _KA // N_STREAMS + h)),
                  pl.BlockSpec((seq, w), lambda b, h, i: (b, _VA // N_STREAMS + h)),
                  pl.BlockSpec((N_STREAMS, 3, ATT_T, 2 * ATT_T), lambda b, h, i: (h, 0, 0, 0))],
        out_specs=pl.BlockSpec((ATT_T, w), lambda b, h, i: (b * nq + i, h)),
        out_shape=jax.ShapeDtypeStruct((m, DA_W), BF16),
        scratch_shapes=_flash_scratch(2 * ATT_T, nq),
        compiler_params=_params("arbitrary", "arbitrary", "arbitrary"),
        name="diff_attention",
    )(lam_vecs, subln_g.astype(F32).reshape(2 * DA_HD, 1), qkv, qkv, qkv, tiles)


def _swap_halves(x):
    return pltpu.roll(x.astype(F32), LANES // 2, 1).astype(x.dtype)


def _swa_kernel(sink_ref, q_ref, kp_ref, ko_ref, vp_ref, vo_ref, bias_ref, o_ref):
    qi = pl.program_id(1)
    t = SW_T
    half = LANES // 2
    lane = lax.broadcasted_iota(jnp.int32, (t, LANES), 1)
    no_prev = jnp.where(qi > 0, 0.0, NEG)
    out = {}
    for s, heads in enumerate((_SW_DIRECT, _SW_SWAPPED)):
        kp, ko, vp, vo = (r[...] if s == 0 else _swap_halves(r[...]) for r in (kp_ref, ko_ref, vp_ref, vo_ref))
        qs = []
        for h in heads:
            q = q_ref[:, (h // 2) * LANES:(h // 2 + 1) * LANES]
            qs.append(jnp.where(lane >= half if h % 2 else lane < half, q, jnp.zeros_like(q)))
        qs = jnp.concatenate(qs, axis=0)
        cols = slice(s * len(heads) * t, (s + 1) * len(heads) * t)
        s_prev = _dot_t(kp, qs) + bias_ref[0:t, cols] + no_prev
        s_own = _dot_t(ko, qs) + bias_ref[t:2 * t, cols]
        sink = sink_ref[:, cols]
        m = jnp.maximum(jnp.maximum(jnp.max(s_prev, axis=0, keepdims=True),
                                    jnp.max(s_own, axis=0, keepdims=True)), sink)
        p_prev = jnp.exp2(s_prev - m)
        p_own = jnp.exp2(s_own - m)
        l = (jnp.sum(p_prev, axis=0, keepdims=True) + jnp.sum(p_own, axis=0, keepdims=True)
             + jnp.exp2(sink - m))
        o = (_pv_t(vp, p_prev.astype(BF16)) + _pv_t(vo, p_own.astype(BF16))) / l
        for c, h in enumerate(heads):
            out[h] = o[:, c * t:(c + 1) * t]
    feat = lax.broadcasted_iota(jnp.int32, (LANES, t), 0)
    for g in range(SW_QW // LANES):
        o_ref[:, g * LANES:(g + 1) * LANES] = jnp.where(feat < half, out[2 * g], out[2 * g + 1]).T.astype(o_ref.dtype)


def _swa_attention(sink_row, qkv, win, batch, seq):
    m = batch * seq
    nq = seq // SW_T
    cols = SW_HEADS * SW_T
    prev = lambda b, i: (b * nq + jnp.maximum(i - 1, 0), _KB)
    own = lambda b, i: (b * nq + i, _KB)
    prev_v = lambda b, i: (b * nq + jnp.maximum(i - 1, 0), _VB)
    own_v = lambda b, i: (b * nq + i, _VB)
    return pl.pallas_call(
        _swa_kernel,
        grid=(batch, nq),
        in_specs=[pl.BlockSpec((1, cols), lambda b, i: (0, 0)),
                  pl.BlockSpec((SW_T, SW_QW), lambda b, i: (b * nq + i, _QB // SW_QW)),
                  pl.BlockSpec((SW_T, LANES), prev),
                  pl.BlockSpec((SW_T, LANES), own),
                  pl.BlockSpec((SW_T, LANES), prev_v),
                  pl.BlockSpec((SW_T, LANES), own_v),
                  pl.BlockSpec((2 * SW_T, cols), lambda b, i: (0, 0))],
        out_specs=pl.BlockSpec((SW_T, SW_QW), lambda b, i: (b * nq + i, 0)),
        out_shape=jax.ShapeDtypeStruct((m, SW_QW), BF16),
        compiler_params=_params("arbitrary", "arbitrary"),
        name="swa_attention",
    )(sink_row, qkv, qkv, qkv, qkv, qkv, win)


def _moba_kernel(q_ref, k_ref, v_ref, bias_ref, o_ref, kmean_s, *scratch, n_blocks):
    qi = pl.program_id(2)
    stride = kmean_s.shape[0] // 3
    scratch = [scratch[SCRATCH_PER_STREAM * g:SCRATCH_PER_STREAM * (g + 1)] for g in range(N_STREAMS)]

    @pl.when(qi == 0)
    def _():
        for g in range(N_STREAMS):
            _store_values_t(v_ref, g, scratch[g][3])
        kf = k_ref[...].astype(F32).reshape(n_blocks, MB_BLOCK, N_STREAMS * LANES)
        rest = jnp.sum(kf, axis=1) * (1.0 / MB_BLOCK)
        kmean_s[...] = jnp.zeros(kmean_s.shape, BF16)
        for part in range(3):
            term = rest.astype(BF16)
            kmean_s[part * stride:part * stride + n_blocks, :] = term
            rest = rest - term.astype(F32)

    def stream(g):
        q2 = _split_heads(q_ref[:, g * LANES:(g + 1) * LANES])
        parts = _dot_t(kmean_s[:, g * LANES:(g + 1) * LANES], q2)
        gate = parts[0:n_blocks] + parts[stride:stride + n_blocks] + parts[2 * stride:2 * stride + n_blocks]
        blk = lax.broadcasted_iota(jnp.int32, gate.shape, 0)

        def selected(kj):
            g_kj = jnp.sum(jnp.where(blk == kj, gate, 0.0), axis=0, keepdims=True)
            beats = ((gate > g_kj) | ((gate == g_kj) & (blk < kj))) & (blk < qi)
            rank = jnp.sum(beats.astype(F32), axis=0, keepdims=True)
            return (rank < MB_TOPK) | (kj >= qi)

        def logits(kj):
            return _dot_t(_kv_block(k_ref, kj, g), q2) + bias_ref[g, _tile_kind(kj, qi)]

        return logits, scratch[g], selected

    outs = _flash(qi, [stream(g) for g in range(N_STREAMS)])
    for g, o in enumerate(outs):
        o_ref[:, g * LANES:(g + 1) * LANES] = _merge_heads_t(o).astype(o_ref.dtype)


def _moba_attention(qkv, tiles, batch, seq):
    m = batch * seq
    nq = seq // ATT_T
    n_blocks = seq // MB_BLOCK
    w = N_STREAMS * LANES
    tile0 = DA_HEADS // N_STREAMS
    return pl.pallas_call(
        functools.partial(_moba_kernel, n_blocks=n_blocks),
        grid=(batch, MB_HEADS // 2 // N_STREAMS, nq),
        in_specs=[pl.BlockSpec((ATT_T, w), lambda b, h, i: (b * nq + i, _QC // N_STREAMS + h)),
                  pl.BlockSpec((seq, w), lambda b, h, i: (b, _KC // N_STREAMS + h)),
                  pl.BlockSpec((seq, w), lambda b, h, i: (b, _VC // N_STREAMS + h)),
                  pl.BlockSpec((N_STREAMS, 3, ATT_T, 2 * ATT_T), lambda b, h, i: (tile0 + h, 0, 0, 0))],
        out_specs=pl.BlockSpec((ATT_T, w), lambda b, h, i: (b * nq + i, h)),
        out_shape=jax.ShapeDtypeStruct((m, MB_W), BF16),
        scratch_shapes=[pltpu.VMEM((3 * pl.cdiv(n_blocks, BF16_ROWS) * BF16_ROWS, w), BF16)]
        + _flash_scratch(2 * ATT_T, nq),
        compiler_params=_params("arbitrary", "arbitrary", "arbitrary"),
        name="moba_attention",
    )(qkv, qkv, qkv, tiles)


def _mix_out_kernel(ya_ref, yb_ref, yc_ref, ga_ref, gb_ref, gc_ref, x_ref, woa_ref, wob_ref, woc_ref,
                    wout_ref, pg_ref, ng_ref, xo_ref, ho_ref):
    def branch(y_ref, w_ref, g_ref, rows):
        return g_ref[rows, :].astype(F32) * jnp.dot(y_ref[rows, :], w_ref[...], preferred_element_type=F32)

    tm = x_ref.shape[0]
    sub = MXU_N if tm % MXU_N == 0 else tm
    for r0 in range(0, tm, sub):
        rows = slice(r0, r0 + sub)
        mix = (branch(ya_ref, woa_ref, ga_ref, rows) + branch(yb_ref, wob_ref, gb_ref, rows)
               + branch(yc_ref, woc_ref, gc_ref, rows))
        z = jnp.dot(mix.astype(BF16), wout_ref[...], preferred_element_type=F32)
        xn = x_ref[rows, :] + _rms(z, pg_ref[...])
        xo_ref[rows, :] = xn
        ho_ref[rows, :] = _rms(xn, ng_ref[...]).astype(ho_ref.dtype)


def _mix_out(ya, yb, yc, gates, x2, woa, wob, woc, wout, post_g, next_g):
    m, d = x2.shape
    tm = _pick(m, (512, 256, 128))
    row = lambda i: (i, 0)
    const = lambda i: (0, 0)
    once = pl.Buffered(1)
    return pl.pallas_call(
        _mix_out_kernel,
        grid=(m // tm,),
        in_specs=[pl.BlockSpec((tm, DA_W), row), pl.BlockSpec((tm, SW_QW), row), pl.BlockSpec((tm, MB_W), row),
                  pl.BlockSpec((tm, d), lambda i: (i, 0)), pl.BlockSpec((tm, d), lambda i: (i, 1)),
                  pl.BlockSpec((tm, d), lambda i: (i, 2)),
                  pl.BlockSpec((tm, d), row),
                  pl.BlockSpec((DA_W, d), const, pipeline_mode=once),
                  pl.BlockSpec((SW_QW, d), const, pipeline_mode=once),
                  pl.BlockSpec((MB_W, d), const, pipeline_mode=once),
                  pl.BlockSpec((d, d), const, pipeline_mode=once),
                  pl.BlockSpec((1, d), const), pl.BlockSpec((1, d), const)],
        out_specs=[pl.BlockSpec((tm, d), row), pl.BlockSpec((tm, d), row)],
        out_shape=[jax.ShapeDtypeStruct((m, d), F32), jax.ShapeDtypeStruct((m, d), BF16)],
        compiler_params=_params("arbitrary"),
        name="mix_out",
    )(ya, yb, yc, gates, gates, gates, x2, woa, wob, woc, wout, post_g.reshape(1, d), next_g.reshape(1, d))


CONV_W = 3
HALO = 8


def _ffn_up_kernel(h_ref, wgf_ref, wvf_ref, cwg_ref, cwv_ref, cbg_ref, cbv_ref, o_ref, halo_g, halo_v, wg_ref, wv_ref,
                   *, tiles_per_seq):
    i = pl.program_id(1)
    tm = h_ref.shape[0]

    @pl.when(i == 0)
    def _():
        wg_ref[...] = wgf_ref[...].astype(BF16)
        wv_ref[...] = wvf_ref[...].astype(BF16)

    @pl.when(i % tiles_per_seq == 0)
    def _():
        halo_g[...] = jnp.zeros(halo_g.shape, F32)
        halo_v[...] = jnp.zeros(halo_v.shape, F32)

    h = h_ref[...]

    def conv(w_ref, cw_ref, cb_ref, halo_s):
        u = jnp.dot(h, w_ref[...], preferred_element_type=F32)
        cw, cb = cw_ref[...], cb_ref[...]

        def taps(u2, u1, u0):
            return cb + u2 * cw[0:1] + u1 * cw[1:2] + u0 * cw[2:3]

        head = jnp.concatenate([halo_s[...], u[0:HALO]], axis=0)
        first = taps(head[HALO - 2:2 * HALO - 2], head[HALO - 1:2 * HALO - 1], head[HALO:])
        body = taps(pltpu.roll(u, 2, 0), pltpu.roll(u, 1, 0), u)
        halo_s[...] = u[tm - HALO:tm]
        return jnp.concatenate([first, body[HALO:]], axis=0)

    gate = conv(wg_ref, cwg_ref, cbg_ref, halo_g)
    val = conv(wv_ref, cwv_ref, cbv_ref, halo_v)
    o_ref[...] = (jax.nn.gelu(gate, approximate=True) * val).astype(o_ref.dtype)


def _ffn_up(h, w_up, conv_w, conv_b, layer, seq):
    m, d = h.shape
    f = w_up.shape[2] // 2
    tm = _pick(seq, (1024, 512, 256, 128))
    tn = _pick(f, (512, 384, 256, 128))
    nj = f // tn
    gate_col = lambda j, i: (layer, 0, j)
    val_col = lambda j, i: (layer, 0, nj + j)
    conv_b = conv_b.reshape(conv_b.shape[0], 1, 2 * f)
    return pl.pallas_call(
        functools.partial(_ffn_up_kernel, tiles_per_seq=seq // tm),
        grid=(nj, m // tm),
        in_specs=[pl.BlockSpec((tm, d), lambda j, i: (i, 0)),
                  pl.BlockSpec((None, d, tn), gate_col), pl.BlockSpec((None, d, tn), val_col),
                  pl.BlockSpec((None, CONV_W, tn), gate_col), pl.BlockSpec((None, CONV_W, tn), val_col),
                  pl.BlockSpec((None, 1, tn), gate_col), pl.BlockSpec((None, 1, tn), val_col)],
        out_specs=pl.BlockSpec((tm, tn), lambda j, i: (i, j)),
        out_shape=jax.ShapeDtypeStruct((m, f), BF16),
        scratch_shapes=[pltpu.VMEM((HALO, tn), F32), pltpu.VMEM((HALO, tn), F32),
                        pltpu.VMEM((d, tn), BF16), pltpu.VMEM((d, tn), BF16)],
        compiler_params=_params("arbitrary", "arbitrary"),
        name="ffn_up",
    )(h, w_up, w_up, conv_w, conv_w, conv_b, conv_b)


def _ffn_down_kernel(a_ref, w_ref, x_ref, pg_ref, ng_ref, xo_ref, *ho_ref):
    k = pl.program_id(1)
    last = pl.num_programs(1) - 1

    @pl.when(k == 0)
    def _():
        xo_ref[...] = jnp.zeros(xo_ref.shape, F32)

    @pl.when(k < last)
    def _():
        xo_ref[...] += jnp.dot(a_ref[...], w_ref[...], preferred_element_type=F32)

    @pl.when(k == last)
    def _():
        tm = x_ref.shape[0]
        sub = MXU_N if tm % MXU_N == 0 else tm
        for r0 in range(0, tm, sub):
            rows = slice(r0, r0 + sub)
            z = xo_ref[rows, :] + jnp.dot(a_ref[rows, :], w_ref[...], preferred_element_type=F32)
            xn = x_ref[rows, :] + _rms(z, pg_ref[...])
            xo_ref[rows, :] = xn
            if ho_ref:
                ho_ref[0][rows, :] = _rms(xn, ng_ref[...]).astype(ho_ref[0].dtype)


def _ffn_down(a, w_down, x2, post_g, next_g):
    m, d = x2.shape
    f = a.shape[1]
    tm = _pick(m, (1024, 512, 256, 128))
    tk = _pick(f, (1408, 1024, 512, 384, 256, 128))
    emit_next = next_g is not None
    row = lambda i, k: (i, 0)
    const = lambda i, k: (0, 0)
    once = pl.Buffered(1)
    out_specs = [pl.BlockSpec((tm, d), row)]
    out_shape = [jax.ShapeDtypeStruct((m, d), F32)]
    if emit_next:
        out_specs.append(pl.BlockSpec((tm, d), row))
        out_shape.append(jax.ShapeDtypeStruct((m, d), BF16))
    ng = next_g if emit_next else post_g
    res = pl.pallas_call(
        _ffn_down_kernel,
        grid=(m // tm, f // tk),
        in_specs=[pl.BlockSpec((tm, tk), lambda i, k: (i, k)), pl.BlockSpec((tk, d), lambda i, k: (k, 0)),
                  pl.BlockSpec((tm, d), row, pipeline_mode=once),
                  pl.BlockSpec((1, d), const), pl.BlockSpec((1, d), const)],
        out_specs=out_specs,
        out_shape=out_shape,
        compiler_params=_params("arbitrary", "arbitrary"),
        name="ffn_down",
    )(a, w_down, x2, post_g.reshape(1, d), ng.reshape(1, d))
    return (res[0], res[1]) if emit_next else (res[0], None)


@jax.jit
def _trunk(x, rel_bias_table, w_in, b_gate, lam_q1, lam_k1, lam_q2, lam_k2, diff_subln_g, sinks, w_oa, w_ob,
           w_oc, w_out, pre_mix_g, post_mix_g, pre_ffn_g, post_ffn_g, w_up, conv_w, conv_b, w_down):
    batch, seq, d = x.shape
    depth = w_in.shape[0]
    assert seq % ATT_T == 0 and seq % MB_BLOCK == 0 and d % LANES == 0
    assert w_in.shape[2] == QKV_W + N_BRANCH * d
    m = batch * seq

    tab_flat = rel_bias_table.astype(F32).reshape(-1)
    tiles, win = _bias_tiles(tab_flat)
    sw_order = jnp.array(_SW_ORDER)

    x2 = x.reshape(m, d)
    h = _prenorm(x2, pre_mix_g[0])
    for l in range(depth):
        qkv = _in_proj(h, w_in, l)
        gates = _in_gates(h, w_in, b_gate, l)

        lam_init = 0.8 - 0.6 * math.exp(-0.3 * l)
        lam_vecs = jnp.stack([lam_q1[l], lam_k1[l], lam_q2[l], lam_k2[l]]).astype(F32)
        ya = _diff_attention(lam_vecs, diff_subln_g[l], qkv, tiles, batch, seq, lam_init)
        sink_row = jnp.repeat(sinks[l].astype(F32)[sw_order] * LOG2E, SW_T).reshape(1, SW_HEADS * SW_T)
        yb = _swa_attention(sink_row, qkv, win, batch, seq)
        yc = _moba_attention(qkv, tiles, batch, seq)

        x2, h = _mix_out(ya, yb, yc, gates, x2, w_oa[l].astype(BF16), w_ob[l].astype(BF16),
                         w_oc[l].astype(BF16), w_out[l].astype(BF16), post_mix_g[l], pre_ffn_g[l])

        a = _ffn_up(h, w_up, conv_w, conv_b, l, seq)
        next_g = pre_mix_g[l + 1] if l + 1 < depth else None
        x2, h = _ffn_down(a, w_down[l].astype(BF16), x2, post_ffn_g[l], next_g)
    return x2.reshape(batch, seq, d)


def kernel(x, rel_bias_table, w_in, b_gate, lam_q1, lam_k1, lam_q2, lam_k2, diff_subln_g, sinks, w_oa, w_ob, w_oc, w_out, pre_mix_g, post_mix_g, pre_ffn_g, post_ffn_g, w_up, conv_w, conv_b, w_down):
    return _trunk(x, rel_bias_table, w_in, b_gate, lam_q1, lam_k1, lam_q2, lam_k2, diff_subln_g, sinks, w_oa, w_ob,
                  w_oc, w_out, pre_mix_g, post_mix_g, pre_ffn_g, post_ffn_g, w_up, conv_w, conv_b, w_down)
```

```python
import functools
import math

import numpy as np
import jax
import jax.numpy as jnp
from jax import lax
from jax.experimental import pallas as pl
from jax.experimental.pallas import tpu as pltpu

DA_HEADS = 4
DA_HD = 64
DA_W = DA_HEADS * 2 * DA_HD
SW_HEADS = 8
SW_KV = 2
SW_HD = 64
WINDOW = 128
SW_QW = SW_HEADS * SW_HD
SW_KW = SW_KV * SW_HD
MB_HEADS = 8
MB_HD = 64
MB_W = MB_HEADS * MB_HD
MB_BLOCK = 256
MB_TOPK = 3
N_BUCKETS = 32
MAX_DIST = 128
N_ATT_HEADS = DA_HEADS + SW_HEADS + MB_HEADS
N_BRANCH = 3
QKV_W = 3 * DA_W + SW_QW + 2 * SW_KW + 3 * MB_W
EPS = 1e-6

LANES = 128
MXU_N = 256
BF16_ROWS = 16
ATT_T = 256
SW_T = WINDOW
NEG = -1e30
LOG2E = math.log2(math.e)
Q_SCALE = DA_HD ** -0.5 * LOG2E
VMEM_LIMIT = 56 * 1024 * 1024

_QA, _KA, _VA = 0, DA_W // LANES, 2 * DA_W // LANES
_QC = 3 * DA_W // LANES
_KC = _QC + MB_W // LANES
_VC = _KC + MB_W // LANES
_QB = 3 * DA_W + 3 * MB_W
_KB = (_QB + SW_QW) // LANES
_VB = _KB + 1
_SW_DIRECT = tuple(h for h in range(SW_HEADS) if h % 2 == h // (SW_HEADS // SW_KV))
_SW_SWAPPED = tuple(h for h in range(SW_HEADS) if h % 2 != h // (SW_HEADS // SW_KV))
_SW_ORDER = _SW_DIRECT + _SW_SWAPPED

F32 = jnp.float32
BF16 = jnp.bfloat16


def _pick(n, candidates):
    for c in candidates:
        if n % c == 0:
            return c
    raise ValueError(f"no tile in {candidates} divides {n}")


def _params(*sem):
    return pltpu.CompilerParams(dimension_semantics=sem, vmem_limit_bytes=VMEM_LIMIT)


def _rms(x, g):
    return x * lax.rsqrt(jnp.mean(x * x, axis=-1, keepdims=True) + EPS) * g


def _dot_t(a, b, **kw):
    return lax.dot_general(a, b, (((1,), (1,)), ((), ())), preferred_element_type=F32, **kw)


def _rel_bucket(dist):
    n = jnp.maximum(dist, 0)
    max_exact = N_BUCKETS // 2
    nf = jnp.maximum(n, 1).astype(F32)
    large = max_exact + (jnp.log(nf / max_exact) / math.log(MAX_DIST / max_exact)
                         * (N_BUCKETS - max_exact)).astype(jnp.int32)
    large = jnp.minimum(large, N_BUCKETS - 1)
    return jnp.where(n < max_exact, n, large)


def _bias_lookup(tab_ref, head, dist):
    bucket = _rel_bucket(dist)
    acc = jnp.zeros(dist.shape, F32)
    for b in range(N_BUCKETS):
        acc = jnp.where(bucket == b, tab_ref[b * N_ATT_HEADS + head], acc)
    return acc * LOG2E


FAR, NEAR, DIAG = 0, 1, 2


def _tile_kind(kj, qi):
    return jnp.clip(kj - qi + DIAG, FAR, DIAG)


def _bias_kernel(tab_ref, o_ref):
    e = pl.program_id(0)
    moba0 = DA_HEADS + SW_HEADS + 2 * (e - DA_HEADS)
    heads = (jnp.where(e < DA_HEADS, e, moba0), jnp.where(e < DA_HEADS, e, moba0 + 1))
    d = (lax.broadcasted_iota(jnp.int32, (ATT_T, ATT_T), 1)
         - lax.broadcasted_iota(jnp.int32, (ATT_T, ATT_T), 0))
    half = ATT_T // 2
    for c, head in enumerate(heads):
        cols = slice(c * ATT_T, (c + 1) * ATT_T)
        last = tab_ref[(N_BUCKETS - 1) * N_ATT_HEADS + head] * LOG2E
        o_ref[0, FAR, :, cols] = jnp.full((ATT_T, ATT_T), last, F32)
        o_ref[0, NEAR, 0:half, cols] = jnp.full((half, ATT_T), last, F32)
        o_ref[0, NEAR, half:ATT_T, cols] = _bias_lookup(tab_ref, head, d[half:] + ATT_T)
        o_ref[0, DIAG, :, cols] = jnp.where(d >= 0, _bias_lookup(tab_ref, head, d), NEG)


def _window_kernel(tab_ref, win_ref):
    c = pl.program_id(0)
    head = jnp.int32(DA_HEADS + _SW_ORDER[-1])
    for pos, h in enumerate(_SW_ORDER[:-1]):
        head = jnp.where(c == pos, DA_HEADS + h, head)
    d = (lax.broadcasted_iota(jnp.int32, (2 * SW_T, SW_T), 1) + SW_T
         - lax.broadcasted_iota(jnp.int32, (2 * SW_T, SW_T), 0))
    win_ref[...] = jnp.where((d >= 0) & (d < WINDOW), _bias_lookup(tab_ref, head, d), NEG)


def _bias_tiles(tab_flat):
    assert ATT_T // 2 + 1 >= MAX_DIST
    n = DA_HEADS + MB_HEADS // 2
    tiles = pl.pallas_call(
        _bias_kernel,
        grid=(n,),
        in_specs=[pl.BlockSpec(memory_space=pltpu.SMEM)],
        out_specs=pl.BlockSpec((1, 3, ATT_T, 2 * ATT_T), lambda e: (e, 0, 0, 0)),
        out_shape=jax.ShapeDtypeStruct((n, 3, ATT_T, 2 * ATT_T), F32),
        compiler_params=_params("arbitrary"),
        name="bias_tiles",
    )(tab_flat)
    win = pl.pallas_call(
        _window_kernel,
        grid=(SW_HEADS,),
        in_specs=[pl.BlockSpec(memory_space=pltpu.SMEM)],
        out_specs=pl.BlockSpec((2 * SW_T, SW_T), lambda c: (0, c)),
        out_shape=jax.ShapeDtypeStruct((2 * SW_T, SW_HEADS * SW_T), F32),
        compiler_params=_params("arbitrary"),
        name="window_tiles",
    )(tab_flat)
    return tiles, win


def _norm_kernel(x_ref, g_ref, o_ref):
    o_ref[...] = _rms(x_ref[...], g_ref[...]).astype(o_ref.dtype)


def _prenorm(x2, g):
    m, d = x2.shape
    tm = _pick(m, (512, 256, 128))
    return pl.pallas_call(
        _norm_kernel,
        grid=(m // tm,),
        in_specs=[pl.BlockSpec((tm, d), lambda i: (i, 0)), pl.BlockSpec((1, d), lambda i: (0, 0))],
        out_specs=pl.BlockSpec((tm, d), lambda i: (i, 0)),
        out_shape=jax.ShapeDtypeStruct((m, d), BF16),
        compiler_params=_params("arbitrary"),
        name="prenorm",
    )(x2, g.reshape(1, d))


def _proj_kernel(h_ref, *refs):
    w_refs, scale_ref, o_ref, wb_s = refs[:-3], refs[-3], refs[-2], refs[-1]

    @pl.when(pl.program_id(1) == 0)
    def _():
        for c, w_ref in enumerate(w_refs):
            cols = slice(c * PROJ_BLOCK, (c + 1) * PROJ_BLOCK)
            wb_s[:, cols] = (w_ref[...] * scale_ref[:, cols]).astype(BF16)

    o_ref[...] = jnp.dot(h_ref[...], wb_s[...], preferred_element_type=F32).astype(o_ref.dtype)


PROJ_BLOCK = 256
PROJ_BLOCKS_PER_STEP = 5


def _proj_source_block(jb):
    n_a, n_b, n_c = (3 * DA_W // PROJ_BLOCK, (SW_QW + 2 * SW_KW) // PROJ_BLOCK, 3 * MB_W // PROJ_BLOCK)
    return jnp.where(jb < n_a, jb, jnp.where(jb < n_a + n_c, jb + n_b, jb - n_c))


def _in_proj(h, w_in, layer):
    m, d = h.shape
    tm = _pick(m, (1024, 512, 256, 128))
    tn = PROJ_BLOCKS_PER_STEP * PROJ_BLOCK
    assert QKV_W % tn == 0 and (SW_QW + 2 * SW_KW) % PROJ_BLOCK == 0 and DA_W % PROJ_BLOCK == 0
    scale = np.ones((1, QKV_W), np.float32)
    for q0 in (_QA * LANES, _QC * LANES, _QB):
        scale[:, q0:q0 + DA_W] = Q_SCALE
    w_specs = [pl.BlockSpec((None, d, PROJ_BLOCK),
                            lambda j, i, c=c: (layer, 0, _proj_source_block(j * PROJ_BLOCKS_PER_STEP + c)))
               for c in range(PROJ_BLOCKS_PER_STEP)]
    return pl.pallas_call(
        _proj_kernel,
        grid=(QKV_W // tn, m // tm),
        in_specs=[pl.BlockSpec((tm, d), lambda j, i: (i, 0))] + w_specs + [pl.BlockSpec((1, tn), lambda j, i: (0, j))],
        out_specs=pl.BlockSpec((tm, tn), lambda j, i: (i, j)),
        out_shape=jax.ShapeDtypeStruct((m, QKV_W), BF16),
        scratch_shapes=[pltpu.VMEM((d, tn), BF16)],
        compiler_params=_params("arbitrary", "arbitrary"),
        name="in_proj",
    )(h, *([w_in] * PROJ_BLOCKS_PER_STEP), jnp.asarray(scale))


def _gate_kernel(h_ref, w_ref, b_ref, o_ref, wb_s):
    @pl.when(pl.program_id(1) == 0)
    def _():
        wb_s[...] = w_ref[...].astype(BF16)

    acc = jnp.dot(h_ref[...], wb_s[...], preferred_element_type=F32)
    o_ref[...] = (0.5 * jnp.tanh(0.5 * (acc + b_ref[...])) + 0.5).astype(o_ref.dtype)


def _in_gates(h, w_in, b_gate, layer):
    m, d = h.shape
    n = w_in.shape[2] - QKV_W
    tm = _pick(m, (1024, 512, 256, 128))
    tn = _pick(math.gcd(n, QKV_W), (768, 512, 256, 128))
    col0 = QKV_W // tn
    return pl.pallas_call(
        _gate_kernel,
        grid=(n // tn, m // tm),
        in_specs=[pl.BlockSpec((tm, d), lambda j, i: (i, 0)),
                  pl.BlockSpec((None, d, tn), lambda j, i: (layer, 0, col0 + j)),
                  pl.BlockSpec((None, 1, tn), lambda j, i: (layer, 0, j))],
        out_specs=pl.BlockSpec((tm, tn), lambda j, i: (i, j)),
        out_shape=jax.ShapeDtypeStruct((m, n), BF16),
        scratch_shapes=[pltpu.VMEM((d, tn), BF16)],
        compiler_params=_params("arbitrary", "arbitrary"),
        name="in_gates",
    )(h, w_in, b_gate.reshape(b_gate.shape[0], 1, n))


def _split_heads(q):
    lane = lax.broadcasted_iota(jnp.int32, q.shape, 1)
    zero = jnp.zeros_like(q)
    return jnp.concatenate([jnp.where(lane < LANES // 2, q, zero),
                            jnp.where(lane >= LANES // 2, q, zero)], axis=0)


def _merge_heads_t(o):
    t = o.shape[1] // 2
    feat = lax.broadcasted_iota(jnp.int32, (LANES, t), 0)
    return jnp.where(feat < LANES // 2, o[:, :t], o[:, t:]).T


def _pv_t(v, p):
    return lax.dot_general(v, p, (((0,), (0,)), ((), ())), preferred_element_type=F32)


def _softmax_stats(s, m, keep=None):
    m_tile = jnp.max(s, axis=0, keepdims=True)
    if keep is not None:
        m_tile = jnp.where(keep, m_tile, NEG)
    m_new = jnp.maximum(m, m_tile)
    alpha = jnp.exp2(m - m_new)
    p = jnp.exp2(s - (m_new if keep is None else jnp.where(keep, m_new, -NEG)))
    return m_new, p.astype(BF16), alpha


VT_ROWS = LANES + BF16_ROWS


def _store_values_t(v_ref, group, vt_s):
    for kj in range(vt_s.shape[0]):
        v = v_ref[kj * ATT_T:(kj + 1) * ATT_T, group * LANES:(group + 1) * LANES]
        vt_s[kj, 0:LANES, :] = v.astype(F32).T.astype(BF16)
        vt_s[kj, LANES:VT_ROWS, :] = jnp.ones((VT_ROWS - LANES, ATT_T), BF16)


def _flash(qi, streams):
    def stats(stream, kj, m):
        _, (s_s, p_s, _, _), keep_fn = stream
        m, p, alpha = _softmax_stats(s_s[...], m, None if keep_fn is None else keep_fn(kj))
        p_s[...] = p
        return m, alpha

    def value_product(stream, kj):
        _, (_, p_s, _, vt_s), _ = stream
        return jnp.dot(vt_s[kj], p_s[...], preferred_element_type=F32)

    for logits_fn, (s_s, _, _, _), _ in streams:
        s_s[...] = logits_fn(0)
    carry = [stats(stream, 0, jnp.full((1, stream[1][0].shape[1]), -jnp.inf, F32)) for stream in streams]
    for logits_fn, (s_s, _, acc_s, _), _ in streams:
        s_s[...] = logits_fn(jnp.minimum(1, qi))
        acc_s[...] = jnp.zeros(acc_s.shape, F32)

    def body(kj, carry):
        out = []
        for stream, (m, alpha_prev) in zip(streams, carry):
            logits_fn, (s_s, _, acc_s, _), _ = stream
            acc_s[...] = alpha_prev * acc_s[...] + value_product(stream, kj - 1)
            out.append(stats(stream, kj, m))
            s_s[...] = logits_fn(jnp.minimum(kj + 1, qi))
        return tuple(out)

    carry = lax.fori_loop(1, qi + 1, body, tuple(carry))
    outs = []
    for stream, (_, alpha) in zip(streams, carry):
        acc = alpha * stream[1][2][...] + value_product(stream, qi)
        outs.append(acc[0:LANES] / acc[LANES:LANES + 1])
    return outs


N_STREAMS = 4


SCRATCH_PER_STREAM = 4


def _flash_scratch(n_queries, n_tiles):
    return N_STREAMS * [pltpu.VMEM((ATT_T, n_queries), F32), pltpu.VMEM((ATT_T, n_queries), BF16),
                        pltpu.VMEM((VT_ROWS, n_queries), F32), pltpu.VMEM((n_tiles, VT_ROWS, ATT_T), BF16)]


def _kv_block(ref, kj, group):
    return ref[pl.ds(pl.multiple_of(kj * ATT_T, ATT_T), ATT_T), group * LANES:(group + 1) * LANES]


def _diff_attn_kernel(lam_ref, g_ref, q_ref, k_ref, v_ref, bias_ref, o_ref, *scratch, lam_init):
    qi = pl.program_id(2)
    t = ATT_T
    scratch = [scratch[SCRATCH_PER_STREAM * g:SCRATCH_PER_STREAM * (g + 1)] for g in range(N_STREAMS)]

    @pl.when(qi == 0)
    def _():
        for g in range(N_STREAMS):
            _store_values_t(v_ref, g, scratch[g][3])

    def stream(g):
        q2 = _split_heads(q_ref[:, g * LANES:(g + 1) * LANES])

        def logits(kj):
            return _dot_t(_kv_block(k_ref, kj, g), q2) + bias_ref[g, _tile_kind(kj, qi)]

        return logits, scratch[g], None

    outs = _flash(qi, [stream(g) for g in range(N_STREAMS)])
    lv = lam_ref[...]
    lam = (jnp.exp(jnp.sum(lv[0:1] * lv[1:2], axis=1, keepdims=True))
           - jnp.exp(jnp.sum(lv[2:3] * lv[3:4], axis=1, keepdims=True)) + lam_init)
    for g, o in enumerate(outs):
        o = o[:, :t] - lam * o[:, t:]
        o = o * lax.rsqrt(jnp.mean(o * o, axis=0, keepdims=True) + EPS) * (g_ref[...] * (1.0 - lam_init))
        o_ref[:, g * LANES:(g + 1) * LANES] = o.T.astype(o_ref.dtype)


def _diff_attention(lam_vecs, subln_g, qkv, tiles, batch, seq, lam_init):
    m = batch * seq
    nq = seq // ATT_T
    w = N_STREAMS * LANES
    return pl.pallas_call(
        functools.partial(_diff_attn_kernel, lam_init=lam_init),
        grid=(batch, DA_HEADS // N_STREAMS, nq),
        in_specs=[pl.BlockSpec((4, DA_HD), lambda b, h, i: (0, 0)),
                  pl.BlockSpec((2 * DA_HD, 1), lambda b, h, i: (0, 0)),
                  pl.BlockSpec((ATT_T, w), lambda b, h, i: (b * nq + i, _QA // N_STREAMS + h)),
                  pl.BlockSpec((seq, w), lambda b, h, i: (b, _KA // N_STREAMS + h)),
                  pl.BlockSpec((seq, w), lambda b, h, i: (b, _VA // N_STREAMS + h)),
                  pl.BlockSpec((N_STREAMS, 3, ATT_T, 2 * ATT_T), lambda b, h, i: (h, 0, 0, 0))],
        out_specs=pl.BlockSpec((ATT_T, w), lambda b, h, i: (b * nq + i, h)),
        out_shape=jax.ShapeDtypeStruct((m, DA_W), BF16),
        scratch_shapes=_flash_scratch(2 * ATT_T, nq),
        compiler_params=_params("arbitrary", "arbitrary", "arbitrary"),
        name="diff_attention",
    )(lam_vecs, subln_g.astype(F32).reshape(2 * DA_HD, 1), qkv, qkv, qkv, tiles)


def _swap_halves(x):
    return pltpu.roll(x.astype(F32), LANES // 2, 1).astype(x.dtype)


def _swa_kernel(sink_ref, q_ref, kp_ref, ko_ref, vp_ref, vo_ref, bias_ref, o_ref):
    qi = pl.program_id(1)
    t = SW_T
    half = LANES // 2
    lane = lax.broadcasted_iota(jnp.int32, (t, LANES), 1)
    feat = lax.broadcasted_iota(jnp.int32, (LANES, t), 0)
    first_of_seq = jnp.where(qi > 0, 0.0, NEG)
    for j in range(SW_TILES):
        rows = slice(j * t, (j + 1) * t)
        before = slice((j - 1) * t, j * t)
        out = {}
        for s, heads in enumerate((_SW_DIRECT, _SW_SWAPPED)):
            kp, vp = (kp_ref[...], vp_ref[...]) if j == 0 else (ko_ref[before, :], vo_ref[before, :])
            ko, vo = ko_ref[rows, :], vo_ref[rows, :]
            if s == 1:
                kp, ko, vp, vo = (_swap_halves(x) for x in (kp, ko, vp, vo))
            qs = []
            for h in heads:
                q = q_ref[rows, (h // 2) * LANES:(h // 2 + 1) * LANES]
                qs.append(jnp.where(lane >= half if h % 2 else lane < half, q, jnp.zeros_like(q)))
            qs = jnp.concatenate(qs, axis=0)
            cols = slice(s * len(heads) * t, (s + 1) * len(heads) * t)
            s_prev = _dot_t(kp, qs) + bias_ref[0:t, cols]
            if j == 0:
                s_prev = s_prev + first_of_seq
            s_own = _dot_t(ko, qs) + bias_ref[t:2 * t, cols]
            sink = sink_ref[:, cols]
            m = jnp.maximum(jnp.maximum(jnp.max(s_prev, axis=0, keepdims=True),
                                        jnp.max(s_own, axis=0, keepdims=True)), sink)
            p_prev = jnp.exp2(s_prev - m)
            p_own = jnp.exp2(s_own - m)
            l = (jnp.sum(p_prev, axis=0, keepdims=True) + jnp.sum(p_own, axis=0, keepdims=True)
                 + jnp.exp2(sink - m))
            o = (_pv_t(vp, p_prev.astype(BF16)) + _pv_t(vo, p_own.astype(BF16))) / l
            for c, h in enumerate(heads):
                out[h] = o[:, c * t:(c + 1) * t]
        for g in range(SW_QW // LANES):
            o_ref[rows, g * LANES:(g + 1) * LANES] = jnp.where(feat < half, out[2 * g], out[2 * g + 1]).T.astype(o_ref.dtype)


SW_TILES = 4


def _swa_attention(sink_row, qkv, win, batch, seq):
    m = batch * seq
    step = SW_TILES * SW_T
    assert seq % step == 0
    nq = seq // step
    cols = SW_HEADS * SW_T
    prev = lambda b, i: (jnp.maximum((b * nq + i) * SW_TILES - 1, 0), _KB)
    prev_v = lambda b, i: (jnp.maximum((b * nq + i) * SW_TILES - 1, 0), _VB)
    own = lambda b, i: (b * nq + i, _KB)
    own_v = lambda b, i: (b * nq + i, _VB)
    return pl.pallas_call(
        _swa_kernel,
        grid=(batch, nq),
        in_specs=[pl.BlockSpec((1, cols), lambda b, i: (0, 0)),
                  pl.BlockSpec((step, SW_QW), lambda b, i: (b * nq + i, _QB // SW_QW)),
                  pl.BlockSpec((SW_T, LANES), prev),
                  pl.BlockSpec((step, LANES), own),
                  pl.BlockSpec((SW_T, LANES), prev_v),
                  pl.BlockSpec((step, LANES), own_v),
                  pl.BlockSpec((2 * SW_T, cols), lambda b, i: (0, 0))],
        out_specs=pl.BlockSpec((step, SW_QW), lambda b, i: (b * nq + i, 0)),
        out_shape=jax.ShapeDtypeStruct((m, SW_QW), BF16),
        compiler_params=_params("arbitrary", "arbitrary"),
        name="swa_attention",
    )(sink_row, qkv, qkv, qkv, qkv, qkv, win)


def _moba_kernel(q_ref, k_ref, v_ref, bias_ref, o_ref, kmean_s, *scratch, n_blocks):
    qi = pl.program_id(2)
    stride = kmean_s.shape[0] // 3
    scratch = [scratch[SCRATCH_PER_STREAM * g:SCRATCH_PER_STREAM * (g + 1)] for g in range(N_STREAMS)]

    @pl.when(qi == 0)
    def _():
        for g in range(N_STREAMS):
            _store_values_t(v_ref, g, scratch[g][3])
        kf = k_ref[...].astype(F32).reshape(n_blocks, MB_BLOCK, N_STREAMS * LANES)
        rest = jnp.sum(kf, axis=1) * (1.0 / MB_BLOCK)
        kmean_s[...] = jnp.zeros(kmean_s.shape, BF16)
        for part in range(3):
            term = rest.astype(BF16)
            kmean_s[part * stride:part * stride + n_blocks, :] = term
            rest = rest - term.astype(F32)

    def stream(g):
        q2 = _split_heads(q_ref[:, g * LANES:(g + 1) * LANES])
        parts = _dot_t(kmean_s[:, g * LANES:(g + 1) * LANES], q2)
        gate = parts[0:n_blocks] + parts[stride:stride + n_blocks] + parts[2 * stride:2 * stride + n_blocks]
        blk = lax.broadcasted_iota(jnp.int32, gate.shape, 0)

        def selected(kj):
            g_kj = jnp.sum(jnp.where(blk == kj, gate, 0.0), axis=0, keepdims=True)
            beats = ((gate > g_kj) | ((gate == g_kj) & (blk < kj))) & (blk < qi)
            rank = jnp.sum(beats.astype(F32), axis=0, keepdims=True)
            return (rank < MB_TOPK) | (kj >= qi)

        def logits(kj):
            return _dot_t(_kv_block(k_ref, kj, g), q2) + bias_ref[g, _tile_kind(kj, qi)]

        return logits, scratch[g], selected

    outs = _flash(qi, [stream(g) for g in range(N_STREAMS)])
    for g, o in enumerate(outs):
        o_ref[:, g * LANES:(g + 1) * LANES] = _merge_heads_t(o).astype(o_ref.dtype)


def _moba_attention(qkv, tiles, batch, seq):
    m = batch * seq
    nq = seq // ATT_T
    n_blocks = seq // MB_BLOCK
    w = N_STREAMS * LANES
    tile0 = DA_HEADS // N_STREAMS
    return pl.pallas_call(
        functools.partial(_moba_kernel, n_blocks=n_blocks),
        grid=(batch, MB_HEADS // 2 // N_STREAMS, nq),
        in_specs=[pl.BlockSpec((ATT_T, w), lambda b, h, i: (b * nq + i, _QC // N_STREAMS + h)),
                  pl.BlockSpec((seq, w), lambda b, h, i: (b, _KC // N_STREAMS + h)),
                  pl.BlockSpec((seq, w), lambda b, h, i: (b, _VC // N_STREAMS + h)),
                  pl.BlockSpec((N_STREAMS, 3, ATT_T, 2 * ATT_T), lambda b, h, i: (tile0 + h, 0, 0, 0))],
        out_specs=pl.BlockSpec((ATT_T, w), lambda b, h, i: (b * nq + i, h)),
        out_shape=jax.ShapeDtypeStruct((m, MB_W), BF16),
        scratch_shapes=[pltpu.VMEM((3 * pl.cdiv(n_blocks, BF16_ROWS) * BF16_ROWS, w), BF16)]
        + _flash_scratch(2 * ATT_T, nq),
        compiler_params=_params("arbitrary", "arbitrary", "arbitrary"),
        name="moba_attention",
    )(qkv, qkv, qkv, tiles)


def _mix_out_kernel(ya_ref, yb_ref, yc_ref, ga_ref, gb_ref, gc_ref, x_ref, woa_ref, wob_ref, woc_ref,
                    wout_ref, pg_ref, ng_ref, xo_ref, ho_ref):
    def branch(y_ref, w_ref, g_ref, rows):
        return g_ref[rows, :].astype(F32) * jnp.dot(y_ref[rows, :], w_ref[...], preferred_element_type=F32)

    tm = x_ref.shape[0]
    sub = MXU_N if tm % MXU_N == 0 else tm
    for r0 in range(0, tm, sub):
        rows = slice(r0, r0 + sub)
        mix = (branch(ya_ref, woa_ref, ga_ref, rows) + branch(yb_ref, wob_ref, gb_ref, rows)
               + branch(yc_ref, woc_ref, gc_ref, rows))
        z = jnp.dot(mix.astype(BF16), wout_ref[...], preferred_element_type=F32)
        xn = x_ref[rows, :] + _rms(z, pg_ref[...])
        xo_ref[rows, :] = xn
        ho_ref[rows, :] = _rms(xn, ng_ref[...]).astype(ho_ref.dtype)


def _mix_out(ya, yb, yc, gates, x2, woa, wob, woc, wout, post_g, next_g):
    m, d = x2.shape
    tm = _pick(m, (512, 256, 128))
    row = lambda i: (i, 0)
    const = lambda i: (0, 0)
    once = pl.Buffered(1)
    return pl.pallas_call(
        _mix_out_kernel,
        grid=(m // tm,),
        in_specs=[pl.BlockSpec((tm, DA_W), row), pl.BlockSpec((tm, SW_QW), row), pl.BlockSpec((tm, MB_W), row),
                  pl.BlockSpec((tm, d), lambda i: (i, 0)), pl.BlockSpec((tm, d), lambda i: (i, 1)),
                  pl.BlockSpec((tm, d), lambda i: (i, 2)),
                  pl.BlockSpec((tm, d), row),
                  pl.BlockSpec((DA_W, d), const, pipeline_mode=once),
                  pl.BlockSpec((SW_QW, d), const, pipeline_mode=once),
                  pl.BlockSpec((MB_W, d), const, pipeline_mode=once),
                  pl.BlockSpec((d, d), const, pipeline_mode=once),
                  pl.BlockSpec((1, d), const), pl.BlockSpec((1, d), const)],
        out_specs=[pl.BlockSpec((tm, d), row), pl.BlockSpec((tm, d), row)],
        out_shape=[jax.ShapeDtypeStruct((m, d), F32), jax.ShapeDtypeStruct((m, d), BF16)],
        compiler_params=_params("arbitrary"),
        name="mix_out",
    )(ya, yb, yc, gates, gates, gates, x2, woa, wob, woc, wout, post_g.reshape(1, d), next_g.reshape(1, d))


CONV_W = 3
HALO = 8


def _ffn_up_kernel(h_ref, wgf_ref, wvf_ref, cwg_ref, cwv_ref, cbg_ref, cbv_ref, o_ref, halo_g, halo_v, wg_ref, wv_ref,
                   *, tiles_per_seq):
    i = pl.program_id(1)
    tm = h_ref.shape[0]

    @pl.when(i == 0)
    def _():
        wg_ref[...] = wgf_ref[...].astype(BF16)
        wv_ref[...] = wvf_ref[...].astype(BF16)

    @pl.when(i % tiles_per_seq == 0)
    def _():
        halo_g[...] = jnp.zeros(halo_g.shape, F32)
        halo_v[...] = jnp.zeros(halo_v.shape, F32)

    h = h_ref[...]

    def conv(w_ref, cw_ref, cb_ref, halo_s):
        u = jnp.dot(h, w_ref[...], preferred_element_type=F32)
        cw, cb = cw_ref[...], cb_ref[...]

        def taps(u2, u1, u0):
            return cb + u2 * cw[0:1] + u1 * cw[1:2] + u0 * cw[2:3]

        head = jnp.concatenate([halo_s[...], u[0:HALO]], axis=0)
        first = taps(head[HALO - 2:2 * HALO - 2], head[HALO - 1:2 * HALO - 1], head[HALO:])
        body = taps(pltpu.roll(u, 2, 0), pltpu.roll(u, 1, 0), u)
        halo_s[...] = u[tm - HALO:tm]
        return jnp.concatenate([first, body[HALO:]], axis=0)

    gate = conv(wg_ref, cwg_ref, cbg_ref, halo_g)
    val = conv(wv_ref, cwv_ref, cbv_ref, halo_v)
    o_ref[...] = (jax.nn.gelu(gate, approximate=True) * val).astype(o_ref.dtype)


def _ffn_up(h, w_up, conv_w, conv_b, layer, seq):
    m, d = h.shape
    f = w_up.shape[2] // 2
    tm = _pick(seq, (1024, 512, 256, 128))
    tn = _pick(f, (512, 384, 256, 128))
    nj = f // tn
    gate_col = lambda j, i: (layer, 0, j)
    val_col = lambda j, i: (layer, 0, nj + j)
    conv_b = conv_b.reshape(conv_b.shape[0], 1, 2 * f)
    return pl.pallas_call(
        functools.partial(_ffn_up_kernel, tiles_per_seq=seq // tm),
        grid=(nj, m // tm),
        in_specs=[pl.BlockSpec((tm, d), lambda j, i: (i, 0)),
                  pl.BlockSpec((None, d, tn), gate_col), pl.BlockSpec((None, d, tn), val_col),
                  pl.BlockSpec((None, CONV_W, tn), gate_col), pl.BlockSpec((None, CONV_W, tn), val_col),
                  pl.BlockSpec((None, 1, tn), gate_col), pl.BlockSpec((None, 1, tn), val_col)],
        out_specs=pl.BlockSpec((tm, tn), lambda j, i: (i, j)),
        out_shape=jax.ShapeDtypeStruct((m, f), BF16),
        scratch_shapes=[pltpu.VMEM((HALO, tn), F32), pltpu.VMEM((HALO, tn), F32),
                        pltpu.VMEM((d, tn), BF16), pltpu.VMEM((d, tn), BF16)],
        compiler_params=_params("arbitrary", "arbitrary"),
        name="ffn_up",
    )(h, w_up, w_up, conv_w, conv_w, conv_b, conv_b)


def _ffn_down_kernel(a_ref, w_ref, x_ref, pg_ref, ng_ref, xo_ref, *ho_ref):
    z = jnp.dot(a_ref[...], w_ref[...], preferred_element_type=F32)
    xn = x_ref[...] + _rms(z, pg_ref[...])
    xo_ref[...] = xn
    if ho_ref:
        ho_ref[0][...] = _rms(xn, ng_ref[...]).astype(ho_ref[0].dtype)


def _ffn_down(a, w_down, x2, post_g, next_g):
    m, d = x2.shape
    f = a.shape[1]
    tm = _pick(m, (256, 128))
    emit_next = next_g is not None
    row = lambda i: (i, 0)
    const = lambda i: (0, 0)
    out_specs = [pl.BlockSpec((tm, d), row)]
    out_shape = [jax.ShapeDtypeStruct((m, d), F32)]
    if emit_next:
        out_specs.append(pl.BlockSpec((tm, d), row))
        out_shape.append(jax.ShapeDtypeStruct((m, d), BF16))
    ng = next_g if emit_next else post_g
    res = pl.pallas_call(
        _ffn_down_kernel,
        grid=(m // tm,),
        in_specs=[pl.BlockSpec((tm, f), row), pl.BlockSpec((f, d), const, pipeline_mode=pl.Buffered(1)),
                  pl.BlockSpec((tm, d), row), pl.BlockSpec((1, d), const), pl.BlockSpec((1, d), const)],
        out_specs=out_specs,
        out_shape=out_shape,
        compiler_params=_params("arbitrary"),
        name="ffn_down",
    )(a, w_down, x2, post_g.reshape(1, d), ng.reshape(1, d))
    return (res[0], res[1]) if emit_next else (res[0], None)


@jax.jit
def _trunk(x, rel_bias_table, w_in, b_gate, lam_q1, lam_k1, lam_q2, lam_k2, diff_subln_g, sinks, w_oa, w_ob,
           w_oc, w_out, pre_mix_g, post_mix_g, pre_ffn_g, post_ffn_g, w_up, conv_w, conv_b, w_down):
    batch, seq, d = x.shape
    depth = w_in.shape[0]
    assert seq % ATT_T == 0 and seq % MB_BLOCK == 0 and d % LANES == 0
    assert w_in.shape[2] == QKV_W + N_BRANCH * d
    m = batch * seq

    tab_flat = rel_bias_table.astype(F32).reshape(-1)
    tiles, win = _bias_tiles(tab_flat)
    sw_order = jnp.array(_SW_ORDER)

    x2 = x.reshape(m, d)
    h = _prenorm(x2, pre_mix_g[0])
    for l in range(depth):
        qkv = _in_proj(h, w_in, l)
        gates = _in_gates(h, w_in, b_gate, l)

        lam_init = 0.8 - 0.6 * math.exp(-0.3 * l)
        lam_vecs = jnp.stack([lam_q1[l], lam_k1[l], lam_q2[l], lam_k2[l]]).astype(F32)
        ya = _diff_attention(lam_vecs, diff_subln_g[l], qkv, tiles, batch, seq, lam_init)
        sink_row = jnp.repeat(sinks[l].astype(F32)[sw_order] * LOG2E, SW_T).reshape(1, SW_HEADS * SW_T)
        yb = _swa_attention(sink_row, qkv, win, batch, seq)
        yc = _moba_attention(qkv, tiles, batch, seq)

        x2, h = _mix_out(ya, yb, yc, gates, x2, w_oa[l].astype(BF16), w_ob[l].astype(BF16),
                         w_oc[l].astype(BF16), w_out[l].astype(BF16), post_mix_g[l], pre_ffn_g[l])

        a = _ffn_up(h, w_up, conv_w, conv_b, l, seq)
        next_g = pre_mix_g[l + 1] if l + 1 < depth else None
        x2, h = _ffn_down(a, w_down[l].astype(BF16), x2, post_ffn_g[l], next_g)
    return x2.reshape(batch, seq, d)


def kernel(x, rel_bias_table, w_in, b_gate, lam_q1, lam_k1, lam_q2, lam_k2, diff_subln_g, sinks, w_oa, w_ob, w_oc, w_out, pre_mix_g, post_mix_g, pre_ffn_g, post_ffn_g, w_up, conv_w, conv_b, w_down):
    return _trunk(x, rel_bias_table, w_in, b_gate, lam_q1, lam_k1, lam_q2, lam_k2, diff_subln_g, sinks, w_oa, w_ob,
                  w_oc, w_out, pre_mix_g, post_mix_g, pre_ffn_g, post_ffn_g, w_up, conv_w, conv_b, w_down)
```

```python
import functools
import math

import numpy as np
import jax
import jax.numpy as jnp
from jax import lax
from jax.experimental import pallas as pl
from jax.experimental.pallas import tpu as pltpu

DA_HEADS = 4
DA_HD = 64
DA_W = DA_HEADS * 2 * DA_HD
SW_HEADS = 8
SW_KV = 2
SW_HD = 64
WINDOW = 128
SW_QW = SW_HEADS * SW_HD
SW_KW = SW_KV * SW_HD
MB_HEADS = 8
MB_HD = 64
MB_W = MB_HEADS * MB_HD
MB_BLOCK = 256
MB_TOPK = 3
N_BUCKETS = 32
MAX_DIST = 128
N_ATT_HEADS = DA_HEADS + SW_HEADS + MB_HEADS
N_BRANCH = 3
QKV_W = 3 * DA_W + SW_QW + 2 * SW_KW + 3 * MB_W
EPS = 1e-6

LANES = 128
MXU_N = 256
BF16_ROWS = 16
ATT_T = 256
SW_T = WINDOW
NEG = -1e30
LOG2E = math.log2(math.e)
Q_SCALE = DA_HD ** -0.5 * LOG2E
VMEM_LIMIT = 56 * 1024 * 1024

_QA, _KA, _VA = 0, DA_W // LANES, 2 * DA_W // LANES
_QC = 3 * DA_W // LANES
_KC = _QC + MB_W // LANES
_VC = _KC + MB_W // LANES
_QB = 3 * DA_W + 3 * MB_W
_KB = (_QB + SW_QW) // LANES
_VB = _KB + 1
_SW_DIRECT = tuple(h for h in range(SW_HEADS) if h % 2 == h // (SW_HEADS // SW_KV))
_SW_SWAPPED = tuple(h for h in range(SW_HEADS) if h % 2 != h // (SW_HEADS // SW_KV))
_SW_ORDER = _SW_DIRECT + _SW_SWAPPED

F32 = jnp.float32
BF16 = jnp.bfloat16


def _pick(n, candidates):
    for c in candidates:
        if n % c == 0:
            return c
    raise ValueError(f"no tile in {candidates} divides {n}")


def _params(*sem):
    return pltpu.CompilerParams(dimension_semantics=sem, vmem_limit_bytes=VMEM_LIMIT)


def _rms(x, g):
    return x * lax.rsqrt(jnp.mean(x * x, axis=-1, keepdims=True) + EPS) * g


def _dot_t(a, b, **kw):
    return lax.dot_general(a, b, (((1,), (1,)), ((), ())), preferred_element_type=F32, **kw)


def _rel_bucket(dist):
    n = jnp.maximum(dist, 0)
    max_exact = N_BUCKETS // 2
    nf = jnp.maximum(n, 1).astype(F32)
    large = max_exact + (jnp.log(nf / max_exact) / math.log(MAX_DIST / max_exact)
                         * (N_BUCKETS - max_exact)).astype(jnp.int32)
    large = jnp.minimum(large, N_BUCKETS - 1)
    return jnp.where(n < max_exact, n, large)


def _bias_lookup(tab_ref, head, dist):
    bucket = _rel_bucket(dist)
    acc = jnp.zeros(dist.shape, F32)
    for b in range(N_BUCKETS):
        acc = jnp.where(bucket == b, tab_ref[b * N_ATT_HEADS + head], acc)
    return acc * LOG2E


FAR, NEAR, DIAG = 0, 1, 2


def _tile_kind(kj, qi):
    return jnp.clip(kj - qi + DIAG, FAR, DIAG)


def _bias_kernel(tab_ref, o_ref):
    e = pl.program_id(0)
    moba0 = DA_HEADS + SW_HEADS + 2 * (e - DA_HEADS)
    heads = (jnp.where(e < DA_HEADS, e, moba0), jnp.where(e < DA_HEADS, e, moba0 + 1))
    d = (lax.broadcasted_iota(jnp.int32, (ATT_T, ATT_T), 1)
         - lax.broadcasted_iota(jnp.int32, (ATT_T, ATT_T), 0))
    half = ATT_T // 2
    for c, head in enumerate(heads):
        cols = slice(c * ATT_T, (c + 1) * ATT_T)
        last = tab_ref[(N_BUCKETS - 1) * N_ATT_HEADS + head] * LOG2E
        o_ref[0, FAR, :, cols] = jnp.full((ATT_T, ATT_T), last, F32)
        o_ref[0, NEAR, 0:half, cols] = jnp.full((half, ATT_T), last, F32)
        o_ref[0, NEAR, half:ATT_T, cols] = _bias_lookup(tab_ref, head, d[half:] + ATT_T)
        o_ref[0, DIAG, :, cols] = jnp.where(d >= 0, _bias_lookup(tab_ref, head, d), NEG)


def _window_kernel(tab_ref, win_ref):
    c = pl.program_id(0)
    head = jnp.int32(DA_HEADS + _SW_ORDER[-1])
    for pos, h in enumerate(_SW_ORDER[:-1]):
        head = jnp.where(c == pos, DA_HEADS + h, head)
    d = (lax.broadcasted_iota(jnp.int32, (2 * SW_T, SW_T), 1) + SW_T
         - lax.broadcasted_iota(jnp.int32, (2 * SW_T, SW_T), 0))
    win_ref[...] = jnp.where((d >= 0) & (d < WINDOW), _bias_lookup(tab_ref, head, d), NEG)


def _bias_tiles(tab_flat):
    assert ATT_T // 2 + 1 >= MAX_DIST
    n = DA_HEADS + MB_HEADS // 2
    tiles = pl.pallas_call(
        _bias_kernel,
        grid=(n,),
        in_specs=[pl.BlockSpec(memory_space=pltpu.SMEM)],
        out_specs=pl.BlockSpec((1, 3, ATT_T, 2 * ATT_T), lambda e: (e, 0, 0, 0)),
        out_shape=jax.ShapeDtypeStruct((n, 3, ATT_T, 2 * ATT_T), F32),
        compiler_params=_params("arbitrary"),
        name="bias_tiles",
    )(tab_flat)
    win = pl.pallas_call(
        _window_kernel,
        grid=(SW_HEADS,),
        in_specs=[pl.BlockSpec(memory_space=pltpu.SMEM)],
        out_specs=pl.BlockSpec((2 * SW_T, SW_T), lambda c: (0, c)),
        out_shape=jax.ShapeDtypeStruct((2 * SW_T, SW_HEADS * SW_T), F32),
        compiler_params=_params("arbitrary"),
        name="window_tiles",
    )(tab_flat)
    return tiles, win


def _norm_kernel(x_ref, g_ref, o_ref):
    o_ref[...] = _rms(x_ref[...], g_ref[...]).astype(o_ref.dtype)


def _prenorm(x2, g):
    m, d = x2.shape
    tm = _pick(m, (512, 256, 128))
    return pl.pallas_call(
        _norm_kernel,
        grid=(m // tm,),
        in_specs=[pl.BlockSpec((tm, d), lambda i: (i, 0)), pl.BlockSpec((1, d), lambda i: (0, 0))],
        out_specs=pl.BlockSpec((tm, d), lambda i: (i, 0)),
        out_shape=jax.ShapeDtypeStruct((m, d), BF16),
        compiler_params=_params("arbitrary"),
        name="prenorm",
    )(x2, g.reshape(1, d))


def _proj_kernel(h_ref, *refs):
    w_refs, scale_ref, o_ref, wb_s = refs[:-3], refs[-3], refs[-2], refs[-1]

    @pl.when(pl.program_id(1) == 0)
    def _():
        for c, w_ref in enumerate(w_refs):
            cols = slice(c * PROJ_BLOCK, (c + 1) * PROJ_BLOCK)
            wb_s[:, cols] = (w_ref[...] * scale_ref[:, cols]).astype(BF16)

    o_ref[...] = jnp.dot(h_ref[...], wb_s[...], preferred_element_type=F32).astype(o_ref.dtype)


PROJ_BLOCK = 256
PROJ_BLOCKS_PER_STEP = 5


def _proj_source_block(jb):
    n_a, n_b, n_c = (3 * DA_W // PROJ_BLOCK, (SW_QW + 2 * SW_KW) // PROJ_BLOCK, 3 * MB_W // PROJ_BLOCK)
    return jnp.where(jb < n_a, jb, jnp.where(jb < n_a + n_c, jb + n_b, jb - n_c))


def _in_proj(h, w_in, layer):
    m, d = h.shape
    tm = _pick(m, (1024, 512, 256, 128))
    tn = PROJ_BLOCKS_PER_STEP * PROJ_BLOCK
    assert QKV_W % tn == 0 and (SW_QW + 2 * SW_KW) % PROJ_BLOCK == 0 and DA_W % PROJ_BLOCK == 0
    scale = np.ones((1, QKV_W), np.float32)
    for q0 in (_QA * LANES, _QC * LANES, _QB):
        scale[:, q0:q0 + DA_W] = Q_SCALE
    w_specs = [pl.BlockSpec((None, d, PROJ_BLOCK),
                            lambda j, i, c=c: (layer, 0, _proj_source_block(j * PROJ_BLOCKS_PER_STEP + c)))
               for c in range(PROJ_BLOCKS_PER_STEP)]
    return pl.pallas_call(
        _proj_kernel,
        grid=(QKV_W // tn, m // tm),
        in_specs=[pl.BlockSpec((tm, d), lambda j, i: (i, 0))] + w_specs + [pl.BlockSpec((1, tn), lambda j, i: (0, j))],
        out_specs=pl.BlockSpec((tm, tn), lambda j, i: (i, j)),
        out_shape=jax.ShapeDtypeStruct((m, QKV_W), BF16),
        scratch_shapes=[pltpu.VMEM((d, tn), BF16)],
        compiler_params=_params("arbitrary", "arbitrary"),
        name="in_proj",
    )(h, *([w_in] * PROJ_BLOCKS_PER_STEP), jnp.asarray(scale))


def _gate_kernel(h_ref, w_ref, b_ref, o_ref, wb_s):
    @pl.when(pl.program_id(1) == 0)
    def _():
        wb_s[...] = w_ref[...].astype(BF16)

    acc = jnp.dot(h_ref[...], wb_s[...], preferred_element_type=F32)
    o_ref[...] = (0.5 * jnp.tanh(0.5 * (acc + b_ref[...])) + 0.5).astype(o_ref.dtype)


def _in_gates(h, w_in, b_gate, layer):
    m, d = h.shape
    n = w_in.shape[2] - QKV_W
    tm = _pick(m, (1024, 512, 256, 128))
    tn = _pick(math.gcd(n, QKV_W), (768, 512, 256, 128))
    col0 = QKV_W // tn
    return pl.pallas_call(
        _gate_kernel,
        grid=(n // tn, m // tm),
        in_specs=[pl.BlockSpec((tm, d), lambda j, i: (i, 0)),
                  pl.BlockSpec((None, d, tn), lambda j, i: (layer, 0, col0 + j)),
                  pl.BlockSpec((None, 1, tn), lambda j, i: (layer, 0, j))],
        out_specs=pl.BlockSpec((tm, tn), lambda j, i: (i, j)),
        out_shape=jax.ShapeDtypeStruct((m, n), BF16),
        scratch_shapes=[pltpu.VMEM((d, tn), BF16)],
        compiler_params=_params("arbitrary", "arbitrary"),
        name="in_gates",
    )(h, w_in, b_gate.reshape(b_gate.shape[0], 1, n))


def _split_heads(q):
    lane = lax.broadcasted_iota(jnp.int32, q.shape, 1)
    zero = jnp.zeros_like(q)
    return jnp.concatenate([jnp.where(lane < LANES // 2, q, zero),
                            jnp.where(lane >= LANES // 2, q, zero)], axis=0)


def _merge_heads_t(o):
    t = o.shape[1] // 2
    feat = lax.broadcasted_iota(jnp.int32, (LANES, t), 0)
    return jnp.where(feat < LANES // 2, o[:, :t], o[:, t:]).T


def _pv_t(v, p):
    return lax.dot_general(v, p, (((0,), (0,)), ((), ())), preferred_element_type=F32)


def _softmax_stats(s, m, keep=None):
    m_tile = jnp.max(s, axis=0, keepdims=True)
    if keep is not None:
        m_tile = jnp.where(keep, m_tile, NEG)
    m_new = jnp.maximum(m, m_tile)
    alpha = jnp.exp2(m - m_new)
    p = jnp.exp2(s - (m_new if keep is None else jnp.where(keep, m_new, -NEG)))
    return m_new, p.astype(BF16), alpha


VT_ROWS = LANES + BF16_ROWS


def _store_values_t(v_ref, group, vt_s):
    for kj in range(vt_s.shape[0]):
        v = v_ref[kj * ATT_T:(kj + 1) * ATT_T, group * LANES:(group + 1) * LANES]
        vt_s[kj, 0:LANES, :] = v.astype(F32).T.astype(BF16)
        vt_s[kj, LANES:VT_ROWS, :] = jnp.ones((VT_ROWS - LANES, ATT_T), BF16)


def _flash(qi, streams):
    def stats(stream, kj, m):
        _, (s_s, p_s, _, _), keep_fn = stream
        m, p, alpha = _softmax_stats(s_s[...], m, None if keep_fn is None else keep_fn(kj))
        p_s[...] = p
        return m, alpha

    def value_product(stream, kj):
        _, (_, p_s, _, vt_s), _ = stream
        return jnp.dot(vt_s[kj], p_s[...], preferred_element_type=F32)

    for logits_fn, (s_s, _, _, _), _ in streams:
        s_s[...] = logits_fn(0)
    carry = [stats(stream, 0, jnp.full((1, stream[1][0].shape[1]), -jnp.inf, F32)) for stream in streams]
    for logits_fn, (s_s, _, acc_s, _), _ in streams:
        s_s[...] = logits_fn(jnp.minimum(1, qi))
        acc_s[...] = jnp.zeros(acc_s.shape, F32)

    def body(kj, carry):
        out = []
        for stream, (m, alpha_prev) in zip(streams, carry):
            logits_fn, (s_s, _, acc_s, _), _ = stream
            acc_s[...] = alpha_prev * acc_s[...] + value_product(stream, kj - 1)
            out.append(stats(stream, kj, m))
            s_s[...] = logits_fn(jnp.minimum(kj + 1, qi))
        return tuple(out)

    carry = lax.fori_loop(1, qi + 1, body, tuple(carry))
    outs = []
    for stream, (_, alpha) in zip(streams, carry):
        acc = alpha * stream[1][2][...] + value_product(stream, qi)
        outs.append(acc[0:LANES] / acc[LANES:LANES + 1])
    return outs


N_STREAMS = 4


SCRATCH_PER_STREAM = 4


def _flash_scratch(n_queries, n_tiles):
    return N_STREAMS * [pltpu.VMEM((ATT_T, n_queries), F32), pltpu.VMEM((ATT_T, n_queries), BF16),
                        pltpu.VMEM((VT_ROWS, n_queries), F32), pltpu.VMEM((n_tiles, VT_ROWS, ATT_T), BF16)]


def _kv_block(ref, kj, group):
    return ref[pl.ds(pl.multiple_of(kj * ATT_T, ATT_T), ATT_T), group * LANES:(group + 1) * LANES]


def _dense_attn_kernel(lam_ref, g_ref, qa_ref, ka_ref, va_ref, qc_ref, kc_ref, vc_ref, bias_ref, oa_ref, oc_ref,
                       kmean_s, *scratch, lam_init, n_blocks):
    qi = pl.program_id(1)
    t = ATT_T
    stride = kmean_s.shape[0] // 3
    scratch = [scratch[SCRATCH_PER_STREAM * g:SCRATCH_PER_STREAM * (g + 1)] for g in range(2 * N_STREAMS)]
    scratch_a, scratch_c = scratch[:N_STREAMS], scratch[N_STREAMS:]

    @pl.when(qi == 0)
    def _():
        for g in range(N_STREAMS):
            _store_values_t(va_ref, g, scratch_a[g][3])
            _store_values_t(vc_ref, g, scratch_c[g][3])
        kf = kc_ref[...].astype(F32).reshape(n_blocks, MB_BLOCK, N_STREAMS * LANES)
        rest = jnp.sum(kf, axis=1) * (1.0 / MB_BLOCK)
        kmean_s[...] = jnp.zeros(kmean_s.shape, BF16)
        for part in range(3):
            term = rest.astype(BF16)
            kmean_s[part * stride:part * stride + n_blocks, :] = term
            rest = rest - term.astype(F32)

    def diff_stream(g):
        q2 = _split_heads(qa_ref[:, g * LANES:(g + 1) * LANES])

        def logits(kj):
            return _dot_t(_kv_block(ka_ref, kj, g), q2) + bias_ref[g, _tile_kind(kj, qi)]

        return logits, scratch_a[g], None

    def moba_stream(g):
        q2 = _split_heads(qc_ref[:, g * LANES:(g + 1) * LANES])
        parts = _dot_t(kmean_s[:, g * LANES:(g + 1) * LANES], q2)
        gate = parts[0:n_blocks] + parts[stride:stride + n_blocks] + parts[2 * stride:2 * stride + n_blocks]
        blk = lax.broadcasted_iota(jnp.int32, gate.shape, 0)

        def selected(kj):
            g_kj = jnp.sum(jnp.where(blk == kj, gate, 0.0), axis=0, keepdims=True)
            beats = ((gate > g_kj) | ((gate == g_kj) & (blk < kj))) & (blk < qi)
            rank = jnp.sum(beats.astype(F32), axis=0, keepdims=True)
            return (rank < MB_TOPK) | (kj >= qi)

        def logits(kj):
            return _dot_t(_kv_block(kc_ref, kj, g), q2) + bias_ref[N_STREAMS + g, _tile_kind(kj, qi)]

        return logits, scratch_c[g], selected

    streams = []
    for g in range(N_STREAMS):
        streams += [diff_stream(g), moba_stream(g)]
    outs = _flash(qi, streams)
    lv = lam_ref[...]
    lam = (jnp.exp(jnp.sum(lv[0:1] * lv[1:2], axis=1, keepdims=True))
           - jnp.exp(jnp.sum(lv[2:3] * lv[3:4], axis=1, keepdims=True)) + lam_init)
    for g in range(N_STREAMS):
        o = outs[2 * g]
        o = o[:, :t] - lam * o[:, t:]
        o = o * lax.rsqrt(jnp.mean(o * o, axis=0, keepdims=True) + EPS) * (g_ref[...] * (1.0 - lam_init))
        oa_ref[:, g * LANES:(g + 1) * LANES] = o.T.astype(oa_ref.dtype)
        oc_ref[:, g * LANES:(g + 1) * LANES] = _merge_heads_t(outs[2 * g + 1]).astype(oc_ref.dtype)


def _dense_attention(lam_vecs, subln_g, qkv, tiles, batch, seq, lam_init):
    assert DA_HEADS == N_STREAMS and MB_HEADS == 2 * N_STREAMS
    m = batch * seq
    nq = seq // ATT_T
    n_blocks = seq // MB_BLOCK
    w = N_STREAMS * LANES
    q_spec = lambda col: pl.BlockSpec((ATT_T, w), lambda b, i: (b * nq + i, col // N_STREAMS))
    kv_spec = lambda col: pl.BlockSpec((seq, w), lambda b, i: (b, col // N_STREAMS))
    out_spec = pl.BlockSpec((ATT_T, w), lambda b, i: (b * nq + i, 0))
    return pl.pallas_call(
        functools.partial(_dense_attn_kernel, lam_init=lam_init, n_blocks=n_blocks),
        grid=(batch, nq),
        in_specs=[pl.BlockSpec((4, DA_HD), lambda b, i: (0, 0)),
                  pl.BlockSpec((2 * DA_HD, 1), lambda b, i: (0, 0)),
                  q_spec(_QA), kv_spec(_KA), kv_spec(_VA), q_spec(_QC), kv_spec(_KC), kv_spec(_VC),
                  pl.BlockSpec((2 * N_STREAMS, 3, ATT_T, 2 * ATT_T), lambda b, i: (0, 0, 0, 0),
                               pipeline_mode=pl.Buffered(1))],
        out_specs=[out_spec, out_spec],
        out_shape=[jax.ShapeDtypeStruct((m, DA_W), BF16), jax.ShapeDtypeStruct((m, MB_W), BF16)],
        scratch_shapes=[pltpu.VMEM((3 * pl.cdiv(n_blocks, BF16_ROWS) * BF16_ROWS, w), BF16)]
        + 2 * _flash_scratch(2 * ATT_T, nq),
        compiler_params=_params("arbitrary", "arbitrary"),
        name="dense_attention",
    )(lam_vecs, subln_g.astype(F32).reshape(2 * DA_HD, 1), qkv, qkv, qkv, qkv, qkv, qkv, tiles)


def _swap_halves(x):
    return pltpu.roll(x.astype(F32), LANES // 2, 1).astype(x.dtype)


def _swa_kernel(sink_ref, q_ref, kp_ref, ko_ref, vp_ref, vo_ref, bias_ref, o_ref):
    qi = pl.program_id(1)
    t = SW_T
    half = LANES // 2
    lane = lax.broadcasted_iota(jnp.int32, (t, LANES), 1)
    feat = lax.broadcasted_iota(jnp.int32, (LANES, t), 0)
    first_of_seq = jnp.where(qi > 0, 0.0, NEG)
    for j in range(SW_TILES):
        rows = slice(j * t, (j + 1) * t)
        before = slice((j - 1) * t, j * t)
        out = {}
        for s, heads in enumerate((_SW_DIRECT, _SW_SWAPPED)):
            kp, vp = (kp_ref[...], vp_ref[...]) if j == 0 else (ko_ref[before, :], vo_ref[before, :])
            ko, vo = ko_ref[rows, :], vo_ref[rows, :]
            if s == 1:
                kp, ko, vp, vo = (_swap_halves(x) for x in (kp, ko, vp, vo))
            qs = []
            for h in heads:
                q = q_ref[rows, (h // 2) * LANES:(h // 2 + 1) * LANES]
                qs.append(jnp.where(lane >= half if h % 2 else lane < half, q, jnp.zeros_like(q)))
            qs = jnp.concatenate(qs, axis=0)
            cols = slice(s * len(heads) * t, (s + 1) * len(heads) * t)
            s_prev = _dot_t(kp, qs) + bias_ref[0:t, cols]
            if j == 0:
                s_prev = s_prev + first_of_seq
            s_own = _dot_t(ko, qs) + bias_ref[t:2 * t, cols]
            sink = sink_ref[:, cols]
            m = jnp.maximum(jnp.maximum(jnp.max(s_prev, axis=0, keepdims=True),
                                        jnp.max(s_own, axis=0, keepdims=True)), sink)
            p_prev = jnp.exp2(s_prev - m)
            p_own = jnp.exp2(s_own - m)
            l = (jnp.sum(p_prev, axis=0, keepdims=True) + jnp.sum(p_own, axis=0, keepdims=True)
                 + jnp.exp2(sink - m))
            o = (_pv_t(vp, p_prev.astype(BF16)) + _pv_t(vo, p_own.astype(BF16))) / l
            for c, h in enumerate(heads):
                out[h] = o[:, c * t:(c + 1) * t]
        for g in range(SW_QW // LANES):
            o_ref[rows, g * LANES:(g + 1) * LANES] = jnp.where(feat < half, out[2 * g], out[2 * g + 1]).T.astype(o_ref.dtype)


SW_TILES = 4


def _swa_attention(sink_row, qkv, win, batch, seq):
    m = batch * seq
    step = SW_TILES * SW_T
    assert seq % step == 0
    nq = seq // step
    cols = SW_HEADS * SW_T
    prev = lambda b, i: (jnp.maximum((b * nq + i) * SW_TILES - 1, 0), _KB)
    prev_v = lambda b, i: (jnp.maximum((b * nq + i) * SW_TILES - 1, 0), _VB)
    own = lambda b, i: (b * nq + i, _KB)
    own_v = lambda b, i: (b * nq + i, _VB)
    return pl.pallas_call(
        _swa_kernel,
        grid=(batch, nq),
        in_specs=[pl.BlockSpec((1, cols), lambda b, i: (0, 0)),
                  pl.BlockSpec((step, SW_QW), lambda b, i: (b * nq + i, _QB // SW_QW)),
                  pl.BlockSpec((SW_T, LANES), prev),
                  pl.BlockSpec((step, LANES), own),
                  pl.BlockSpec((SW_T, LANES), prev_v),
                  pl.BlockSpec((step, LANES), own_v),
                  pl.BlockSpec((2 * SW_T, cols), lambda b, i: (0, 0))],
        out_specs=pl.BlockSpec((step, SW_QW), lambda b, i: (b * nq + i, 0)),
        out_shape=jax.ShapeDtypeStruct((m, SW_QW), BF16),
        compiler_params=_params("arbitrary", "arbitrary"),
        name="swa_attention",
    )(sink_row, qkv, qkv, qkv, qkv, qkv, win)


def _mix_out_kernel(ya_ref, yb_ref, yc_ref, ga_ref, gb_ref, gc_ref, x_ref, woa_ref, wob_ref, woc_ref,
                    wout_ref, pg_ref, ng_ref, xo_ref, ho_ref):
    def branch(y_ref, w_ref, g_ref, rows):
        return g_ref[rows, :].astype(F32) * jnp.dot(y_ref[rows, :], w_ref[...], preferred_element_type=F32)

    tm = x_ref.shape[0]
    sub = MXU_N if tm % MXU_N == 0 else tm
    for r0 in range(0, tm, sub):
        rows = slice(r0, r0 + sub)
        mix = (branch(ya_ref, woa_ref, ga_ref, rows) + branch(yb_ref, wob_ref, gb_ref, rows)
               + branch(yc_ref, woc_ref, gc_ref, rows))
        z = jnp.dot(mix.astype(BF16), wout_ref[...], preferred_element_type=F32)
        xn = x_ref[rows, :] + _rms(z, pg_ref[...])
        xo_ref[rows, :] = xn
        ho_ref[rows, :] = _rms(xn, ng_ref[...]).astype(ho_ref.dtype)


def _mix_out(ya, yb, yc, gates, x2, woa, wob, woc, wout, post_g, next_g):
    m, d = x2.shape
    tm = _pick(m, (512, 256, 128))
    row = lambda i: (i, 0)
    const = lambda i: (0, 0)
    once = pl.Buffered(1)
    return pl.pallas_call(
        _mix_out_kernel,
        grid=(m // tm,),
        in_specs=[pl.BlockSpec((tm, DA_W), row), pl.BlockSpec((tm, SW_QW), row), pl.BlockSpec((tm, MB_W), row),
                  pl.BlockSpec((tm, d), lambda i: (i, 0)), pl.BlockSpec((tm, d), lambda i: (i, 1)),
                  pl.BlockSpec((tm, d), lambda i: (i, 2)),
                  pl.BlockSpec((tm, d), row),
                  pl.BlockSpec((DA_W, d), const, pipeline_mode=once),
                  pl.BlockSpec((SW_QW, d), const, pipeline_mode=once),
                  pl.BlockSpec((MB_W, d), const, pipeline_mode=once),
                  pl.BlockSpec((d, d), const, pipeline_mode=once),
                  pl.BlockSpec((1, d), const), pl.BlockSpec((1, d), const)],
        out_specs=[pl.BlockSpec((tm, d), row), pl.BlockSpec((tm, d), row)],
        out_shape=[jax.ShapeDtypeStruct((m, d), F32), jax.ShapeDtypeStruct((m, d), BF16)],
        compiler_params=_params("arbitrary"),
        name="mix_out",
    )(ya, yb, yc, gates, gates, gates, x2, woa, wob, woc, wout, post_g.reshape(1, d), next_g.reshape(1, d))


CONV_W = 3
HALO = 8


def _ffn_up_kernel(h_ref, wgf_ref, wvf_ref, cwg_ref, cwv_ref, cbg_ref, cbv_ref, o_ref, halo_g, halo_v, wg_ref, wv_ref,
                   *, tiles_per_seq):
    i = pl.program_id(1)
    tm = h_ref.shape[0]

    @pl.when(i == 0)
    def _():
        wg_ref[...] = wgf_ref[...].astype(BF16)
        wv_ref[...] = wvf_ref[...].astype(BF16)

    @pl.when(i % tiles_per_seq == 0)
    def _():
        halo_g[...] = jnp.zeros(halo_g.shape, F32)
        halo_v[...] = jnp.zeros(halo_v.shape, F32)

    h = h_ref[...]

    def conv(w_ref, cw_ref, cb_ref, halo_s):
        u = jnp.dot(h, w_ref[...], preferred_element_type=F32)
        cw, cb = cw_ref[...], cb_ref[...]

        def taps(u2, u1, u0):
            return cb + u2 * cw[0:1] + u1 * cw[1:2] + u0 * cw[2:3]

        head = jnp.concatenate([halo_s[...], u[0:HALO]], axis=0)
        first = taps(head[HALO - 2:2 * HALO - 2], head[HALO - 1:2 * HALO - 1], head[HALO:])
        body = taps(pltpu.roll(u, 2, 0), pltpu.roll(u, 1, 0), u)
        halo_s[...] = u[tm - HALO:tm]
        return jnp.concatenate([first, body[HALO:]], axis=0)

    gate = conv(wg_ref, cwg_ref, cbg_ref, halo_g)
    val = conv(wv_ref, cwv_ref, cbv_ref, halo_v)
    o_ref[...] = (jax.nn.gelu(gate, approximate=True) * val).astype(o_ref.dtype)


def _ffn_up(h, w_up, conv_w, conv_b, layer, seq):
    m, d = h.shape
    f = w_up.shape[2] // 2
    tm = _pick(seq, (1024, 512, 256, 128))
    tn = _pick(f, (512, 384, 256, 128))
    nj = f // tn
    gate_col = lambda j, i: (layer, 0, j)
    val_col = lambda j, i: (layer, 0, nj + j)
    conv_b = conv_b.reshape(conv_b.shape[0], 1, 2 * f)
    return pl.pallas_call(
        functools.partial(_ffn_up_kernel, tiles_per_seq=seq // tm),
        grid=(nj, m // tm),
        in_specs=[pl.BlockSpec((tm, d), lambda j, i: (i, 0)),
                  pl.BlockSpec((None, d, tn), gate_col), pl.BlockSpec((None, d, tn), val_col),
                  pl.BlockSpec((None, CONV_W, tn), gate_col), pl.BlockSpec((None, CONV_W, tn), val_col),
                  pl.BlockSpec((None, 1, tn), gate_col), pl.BlockSpec((None, 1, tn), val_col)],
        out_specs=pl.BlockSpec((tm, tn), lambda j, i: (i, j)),
        out_shape=jax.ShapeDtypeStruct((m, f), BF16),
        scratch_shapes=[pltpu.VMEM((HALO, tn), F32), pltpu.VMEM((HALO, tn), F32),
                        pltpu.VMEM((d, tn), BF16), pltpu.VMEM((d, tn), BF16)],
        compiler_params=_params("arbitrary", "arbitrary"),
        name="ffn_up",
    )(h, w_up, w_up, conv_w, conv_w, conv_b, conv_b)


def _ffn_down_kernel(a_ref, w_ref, x_ref, pg_ref, ng_ref, xo_ref, *ho_ref):
    z = jnp.dot(a_ref[...], w_ref[...], preferred_element_type=F32)
    xn = x_ref[...] + _rms(z, pg_ref[...])
    xo_ref[...] = xn
    if ho_ref:
        ho_ref[0][...] = _rms(xn, ng_ref[...]).astype(ho_ref[0].dtype)


def _ffn_down(a, w_down, x2, post_g, next_g):
    m, d = x2.shape
    f = a.shape[1]
    tm = _pick(m, (256, 128))
    emit_next = next_g is not None
    row = lambda i: (i, 0)
    const = lambda i: (0, 0)
    out_specs = [pl.BlockSpec((tm, d), row)]
    out_shape = [jax.ShapeDtypeStruct((m, d), F32)]
    if emit_next:
        out_specs.append(pl.BlockSpec((tm, d), row))
        out_shape.append(jax.ShapeDtypeStruct((m, d), BF16))
    ng = next_g if emit_next else post_g
    res = pl.pallas_call(
        _ffn_down_kernel,
        grid=(m // tm,),
        in_specs=[pl.BlockSpec((tm, f), row), pl.BlockSpec((f, d), const, pipeline_mode=pl.Buffered(1)),
                  pl.BlockSpec((tm, d), row), pl.BlockSpec((1, d), const), pl.BlockSpec((1, d), const)],
        out_specs=out_specs,
        out_shape=out_shape,
        compiler_params=_params("arbitrary"),
        name="ffn_down",
    )(a, w_down, x2, post_g.reshape(1, d), ng.reshape(1, d))
    return (res[0], res[1]) if emit_next else (res[0], None)


@jax.jit
def _trunk(x, rel_bias_table, w_in, b_gate, lam_q1, lam_k1, lam_q2, lam_k2, diff_subln_g, sinks, w_oa, w_ob,
           w_oc, w_out, pre_mix_g, post_mix_g, pre_ffn_g, post_ffn_g, w_up, conv_w, conv_b, w_down):
    batch, seq, d = x.shape
    depth = w_in.shape[0]
    assert seq % ATT_T == 0 and seq % MB_BLOCK == 0 and d % LANES == 0
    assert w_in.shape[2] == QKV_W + N_BRANCH * d
    m = batch * seq

    tab_flat = rel_bias_table.astype(F32).reshape(-1)
    tiles, win = _bias_tiles(tab_flat)
    sw_order = jnp.array(_SW_ORDER)

    x2 = x.reshape(m, d)
    h = _prenorm(x2, pre_mix_g[0])
    for l in range(depth):
        qkv = _in_proj(h, w_in, l)
        gates = _in_gates(h, w_in, b_gate, l)

        lam_init = 0.8 - 0.6 * math.exp(-0.3 * l)
        lam_vecs = jnp.stack([lam_q1[l], lam_k1[l], lam_q2[l], lam_k2[l]]).astype(F32)
        ya, yc = _dense_attention(lam_vecs, diff_subln_g[l], qkv, tiles, batch, seq, lam_init)
        sink_row = jnp.repeat(sinks[l].astype(F32)[sw_order] * LOG2E, SW_T).reshape(1, SW_HEADS * SW_T)
        yb = _swa_attention(sink_row, qkv, win, batch, seq)

        x2, h = _mix_out(ya, yb, yc, gates, x2, w_oa[l].astype(BF16), w_ob[l].astype(BF16),
                         w_oc[l].astype(BF16), w_out[l].astype(BF16), post_mix_g[l], pre_ffn_g[l])

        a = _ffn_up(h, w_up, conv_w, conv_b, l, seq)
        next_g = pre_mix_g[l + 1] if l + 1 < depth else None
        x2, h = _ffn_down(a, w_down[l].astype(BF16), x2, post_ffn_g[l], next_g)
    return x2.reshape(batch, seq, d)


def kernel(x, rel_bias_table, w_in, b_gate, lam_q1, lam_k1, lam_q2, lam_k2, diff_subln_g, sinks, w_oa, w_ob, w_oc, w_out, pre_mix_g, post_mix_g, pre_ffn_g, post_ffn_g, w_up, conv_w, conv_b, w_down):
    return _trunk(x, rel_bias_table, w_in, b_gate, lam_q1, lam_k1, lam_q2, lam_k2, diff_subln_g, sinks, w_oa, w_ob,
                  w_oc, w_out, pre_mix_g, post_mix_g, pre_ffn_g, post_ffn_g, w_up, conv_w, conv_b, w_down)
```

```python
import functools
import math

import numpy as np
import jax
import jax.numpy as jnp
from jax import lax
from jax.experimental import pallas as pl
from jax.experimental.pallas import tpu as pltpu

DA_HEADS = 4
DA_HD = 64
DA_W = DA_HEADS * 2 * DA_HD
SW_HEADS = 8
SW_KV = 2
SW_HD = 64
WINDOW = 128
SW_QW = SW_HEADS * SW_HD
SW_KW = SW_KV * SW_HD
MB_HEADS = 8
MB_HD = 64
MB_W = MB_HEADS * MB_HD
MB_BLOCK = 256
MB_TOPK = 3
N_BUCKETS = 32
MAX_DIST = 128
N_ATT_HEADS = DA_HEADS + SW_HEADS + MB_HEADS
N_BRANCH = 3
QKV_W = 3 * DA_W + SW_QW + 2 * SW_KW + 3 * MB_W
EPS = 1e-6

LANES = 128
MXU_N = 256
BF16_ROWS = 16
ATT_T = 256
SW_T = WINDOW
NEG = -1e30
LOG2E = math.log2(math.e)
Q_SCALE = DA_HD ** -0.5 * LOG2E
VMEM_LIMIT = 56 * 1024 * 1024

_QA, _KA, _VA = 0, DA_W // LANES, 2 * DA_W // LANES
_QC = 3 * DA_W // LANES
_KC = _QC + MB_W // LANES
_VC = _KC + MB_W // LANES
_QB = 3 * DA_W + 3 * MB_W
_KB = (_QB + SW_QW) // LANES
_VB = _KB + 1
_SW_DIRECT = tuple(h for h in range(SW_HEADS) if h % 2 == h // (SW_HEADS // SW_KV))
_SW_SWAPPED = tuple(h for h in range(SW_HEADS) if h % 2 != h // (SW_HEADS // SW_KV))
_SW_ORDER = _SW_DIRECT + _SW_SWAPPED

F32 = jnp.float32
BF16 = jnp.bfloat16


def _pick(n, candidates):
    for c in candidates:
        if n % c == 0:
            return c
    raise ValueError(f"no tile in {candidates} divides {n}")


def _params(*sem):
    return pltpu.CompilerParams(dimension_semantics=sem, vmem_limit_bytes=VMEM_LIMIT)


def _rms(x, g):
    return x * lax.rsqrt(jnp.mean(x * x, axis=-1, keepdims=True) + EPS) * g


def _dot_t(a, b, **kw):
    return lax.dot_general(a, b, (((1,), (1,)), ((), ())), preferred_element_type=F32, **kw)


def _rel_bucket(dist):
    n = jnp.maximum(dist, 0)
    max_exact = N_BUCKETS // 2
    nf = jnp.maximum(n, 1).astype(F32)
    large = max_exact + (jnp.log(nf / max_exact) / math.log(MAX_DIST / max_exact)
                         * (N_BUCKETS - max_exact)).astype(jnp.int32)
    large = jnp.minimum(large, N_BUCKETS - 1)
    return jnp.where(n < max_exact, n, large)


def _bias_lookup(tab_ref, head, dist):
    bucket = _rel_bucket(dist)
    acc = jnp.zeros(dist.shape, F32)
    for b in range(N_BUCKETS):
        acc = jnp.where(bucket == b, tab_ref[b * N_ATT_HEADS + head], acc)
    return acc * LOG2E


FAR, NEAR, DIAG = 0, 1, 2


def _tile_kind(kj, qi):
    return jnp.clip(kj - qi + DIAG, FAR, DIAG)


def _bias_kernel(tab_ref, o_ref):
    e = pl.program_id(0)
    moba0 = DA_HEADS + SW_HEADS + 2 * (e - DA_HEADS)
    heads = (jnp.where(e < DA_HEADS, e, moba0), jnp.where(e < DA_HEADS, e, moba0 + 1))
    d = (lax.broadcasted_iota(jnp.int32, (ATT_T, ATT_T), 1)
         - lax.broadcasted_iota(jnp.int32, (ATT_T, ATT_T), 0))
    half = ATT_T // 2
    for c, head in enumerate(heads):
        cols = slice(c * ATT_T, (c + 1) * ATT_T)
        last = tab_ref[(N_BUCKETS - 1) * N_ATT_HEADS + head] * LOG2E
        o_ref[0, FAR, :, cols] = jnp.full((ATT_T, ATT_T), last, F32)
        o_ref[0, NEAR, 0:half, cols] = jnp.full((half, ATT_T), last, F32)
        o_ref[0, NEAR, half:ATT_T, cols] = _bias_lookup(tab_ref, head, d[half:] + ATT_T)
        o_ref[0, DIAG, :, cols] = jnp.where(d >= 0, _bias_lookup(tab_ref, head, d), NEG)


def _window_kernel(tab_ref, win_ref):
    c = pl.program_id(0)
    head = jnp.int32(DA_HEADS + _SW_ORDER[-1])
    for pos, h in enumerate(_SW_ORDER[:-1]):
        head = jnp.where(c == pos, DA_HEADS + h, head)
    d = (lax.broadcasted_iota(jnp.int32, (2 * SW_T, SW_T), 1) + SW_T
         - lax.broadcasted_iota(jnp.int32, (2 * SW_T, SW_T), 0))
    win_ref[...] = jnp.where((d >= 0) & (d < WINDOW), _bias_lookup(tab_ref, head, d), NEG)


def _bias_tiles(tab_flat):
    assert ATT_T // 2 + 1 >= MAX_DIST
    n = DA_HEADS + MB_HEADS // 2
    tiles = pl.pallas_call(
        _bias_kernel,
        grid=(n,),
        in_specs=[pl.BlockSpec(memory_space=pltpu.SMEM)],
        out_specs=pl.BlockSpec((1, 3, ATT_T, 2 * ATT_T), lambda e: (e, 0, 0, 0)),
        out_shape=jax.ShapeDtypeStruct((n, 3, ATT_T, 2 * ATT_T), F32),
        compiler_params=_params("arbitrary"),
        name="bias_tiles",
    )(tab_flat)
    win = pl.pallas_call(
        _window_kernel,
        grid=(SW_HEADS,),
        in_specs=[pl.BlockSpec(memory_space=pltpu.SMEM)],
        out_specs=pl.BlockSpec((2 * SW_T, SW_T), lambda c: (0, c)),
        out_shape=jax.ShapeDtypeStruct((2 * SW_T, SW_HEADS * SW_T), F32),
        compiler_params=_params("arbitrary"),
        name="window_tiles",
    )(tab_flat)
    return tiles, win


def _norm_kernel(x_ref, g_ref, o_ref):
    o_ref[...] = _rms(x_ref[...], g_ref[...]).astype(o_ref.dtype)


def _prenorm(x2, g):
    m, d = x2.shape
    tm = _pick(m, (512, 256, 128))
    return pl.pallas_call(
        _norm_kernel,
        grid=(m // tm,),
        in_specs=[pl.BlockSpec((tm, d), lambda i: (i, 0)), pl.BlockSpec((1, d), lambda i: (0, 0))],
        out_specs=pl.BlockSpec((tm, d), lambda i: (i, 0)),
        out_shape=jax.ShapeDtypeStruct((m, d), BF16),
        compiler_params=_params("arbitrary"),
        name="prenorm",
    )(x2, g.reshape(1, d))


def _proj_kernel(h_ref, *refs):
    w_refs, scale_ref, o_ref, wb_s = refs[:-3], refs[-3], refs[-2], refs[-1]

    @pl.when(pl.program_id(1) == 0)
    def _():
        for c, w_ref in enumerate(w_refs):
            cols = slice(c * PROJ_BLOCK, (c + 1) * PROJ_BLOCK)
            wb_s[:, cols] = (w_ref[...] * scale_ref[:, cols]).astype(BF16)

    o_ref[...] = jnp.dot(h_ref[...], wb_s[...], preferred_element_type=F32).astype(o_ref.dtype)


PROJ_BLOCK = 256
PROJ_BLOCKS_PER_STEP = 5


def _proj_source_block(jb):
    n_a, n_b, n_c = (3 * DA_W // PROJ_BLOCK, (SW_QW + 2 * SW_KW) // PROJ_BLOCK, 3 * MB_W // PROJ_BLOCK)
    return jnp.where(jb < n_a, jb, jnp.where(jb < n_a + n_c, jb + n_b, jb - n_c))


def _in_proj(h, w_in, layer):
    m, d = h.shape
    tm = _pick(m, (1024, 512, 256, 128))
    tn = PROJ_BLOCKS_PER_STEP * PROJ_BLOCK
    assert QKV_W % tn == 0 and (SW_QW + 2 * SW_KW) % PROJ_BLOCK == 0 and DA_W % PROJ_BLOCK == 0
    scale = np.ones((1, QKV_W), np.float32)
    for q0 in (_QA * LANES, _QC * LANES, _QB):
        scale[:, q0:q0 + DA_W] = Q_SCALE
    w_specs = [pl.BlockSpec((None, d, PROJ_BLOCK),
                            lambda j, i, c=c: (layer, 0, _proj_source_block(j * PROJ_BLOCKS_PER_STEP + c)))
               for c in range(PROJ_BLOCKS_PER_STEP)]
    return pl.pallas_call(
        _proj_kernel,
        grid=(QKV_W // tn, m // tm),
        in_specs=[pl.BlockSpec((tm, d), lambda j, i: (i, 0))] + w_specs + [pl.BlockSpec((1, tn), lambda j, i: (0, j))],
        out_specs=pl.BlockSpec((tm, tn), lambda j, i: (i, j)),
        out_shape=jax.ShapeDtypeStruct((m, QKV_W), BF16),
        scratch_shapes=[pltpu.VMEM((d, tn), BF16)],
        compiler_params=_params("arbitrary", "arbitrary"),
        name="in_proj",
    )(h, *([w_in] * PROJ_BLOCKS_PER_STEP), jnp.asarray(scale))


def _gate_kernel(h_ref, w_ref, b_ref, o_ref, wb_s):
    @pl.when(pl.program_id(1) == 0)
    def _():
        wb_s[...] = w_ref[...].astype(BF16)

    acc = jnp.dot(h_ref[...], wb_s[...], preferred_element_type=F32)
    o_ref[...] = (0.5 * jnp.tanh(0.5 * (acc + b_ref[...])) + 0.5).astype(o_ref.dtype)


def _in_gates(h, w_in, b_gate, layer):
    m, d = h.shape
    n = w_in.shape[2] - QKV_W
    tm = _pick(m, (1024, 512, 256, 128))
    tn = _pick(math.gcd(n, QKV_W), (768, 512, 256, 128))
    col0 = QKV_W // tn
    return pl.pallas_call(
        _gate_kernel,
        grid=(n // tn, m // tm),
        in_specs=[pl.BlockSpec((tm, d), lambda j, i: (i, 0)),
                  pl.BlockSpec((None, d, tn), lambda j, i: (layer, 0, col0 + j)),
                  pl.BlockSpec((None, 1, tn), lambda j, i: (layer, 0, j))],
        out_specs=pl.BlockSpec((tm, tn), lambda j, i: (i, j)),
        out_shape=jax.ShapeDtypeStruct((m, n), BF16),
        scratch_shapes=[pltpu.VMEM((d, tn), BF16)],
        compiler_params=_params("arbitrary", "arbitrary"),
        name="in_gates",
    )(h, w_in, b_gate.reshape(b_gate.shape[0], 1, n))


def _split_heads(q):
    lane = lax.broadcasted_iota(jnp.int32, q.shape, 1)
    zero = jnp.zeros_like(q)
    return jnp.concatenate([jnp.where(lane < LANES // 2, q, zero),
                            jnp.where(lane >= LANES // 2, q, zero)], axis=0)


def _merge_heads_t(o):
    t = o.shape[1] // 2
    feat = lax.broadcasted_iota(jnp.int32, (LANES, t), 0)
    return jnp.where(feat < LANES // 2, o[:, :t], o[:, t:]).T


def _pv_t(v, p):
    return lax.dot_general(v, p, (((0,), (0,)), ((), ())), preferred_element_type=F32)


def _softmax_stats(s, m_tile, m, keep=None):
    if keep is not None:
        m_tile = jnp.where(keep, m_tile, NEG)
    m_new = jnp.maximum(m, m_tile)
    alpha = jnp.exp2(m - m_new)
    p = jnp.exp2(s - (m_new if keep is None else jnp.where(keep, m_new, -NEG)))
    return m_new, p.astype(BF16), alpha


VT_ROWS = LANES + BF16_ROWS


def _store_values_t(v_ref, group, vt_s):
    for kj in range(vt_s.shape[0]):
        v = v_ref[kj * ATT_T:(kj + 1) * ATT_T, group * LANES:(group + 1) * LANES]
        vt_s[kj, 0:LANES, :] = v.astype(F32).T.astype(BF16)
        vt_s[kj, LANES:VT_ROWS, :] = jnp.ones((VT_ROWS - LANES, ATT_T), BF16)


def _flash(qi, streams):
    def issue_logits(stream, kj):
        logits_fn, (s_s, _, _), _ = stream
        s = logits_fn(kj)
        s_s[...] = s
        return jnp.max(s, axis=0, keepdims=True)

    def accumulate(stream, kj, m_tile, m):
        _, (s_s, acc_s, vt_s), keep_fn = stream
        m, p, alpha = _softmax_stats(s_s[...], m_tile, m, None if keep_fn is None else keep_fn(kj))
        return m, alpha * acc_s[...] + jnp.dot(vt_s[kj], p, preferred_element_type=F32)

    carry = []
    for stream in streams:
        acc_s = stream[1][1]
        acc_s[...] = jnp.zeros(acc_s.shape, F32)
        carry.append((jnp.full((1, acc_s.shape[1]), -jnp.inf, F32), issue_logits(stream, 0)))

    def body(kj, carry):
        out = []
        for stream, (m, m_tile) in zip(streams, carry):
            m, acc = accumulate(stream, kj, m_tile, m)
            stream[1][1][...] = acc
            out.append((m, issue_logits(stream, kj + 1)))
        return tuple(out)

    carry = lax.fori_loop(0, qi, body, tuple(carry))
    outs = []
    for stream, (m, m_tile) in zip(streams, carry):
        _, acc = accumulate(stream, qi, m_tile, m)
        outs.append(acc[0:LANES] / acc[LANES:LANES + 1])
    return outs


N_STREAMS = 4


SCRATCH_PER_STREAM = 3


def _flash_scratch(n_queries, n_tiles):
    return N_STREAMS * [pltpu.VMEM((ATT_T, n_queries), F32), pltpu.VMEM((VT_ROWS, n_queries), F32),
                        pltpu.VMEM((n_tiles, VT_ROWS, ATT_T), BF16)]


def _kv_block(ref, kj, group):
    return ref[pl.ds(pl.multiple_of(kj * ATT_T, ATT_T), ATT_T), group * LANES:(group + 1) * LANES]


def _dense_attn_kernel(lam_ref, g_ref, qa_ref, ka_ref, va_ref, qc_ref, kc_ref, vc_ref, bias_ref, oa_ref, oc_ref,
                       kmean_s, *scratch, lam_init, n_blocks):
    qi = pl.program_id(1)
    t = ATT_T
    stride = kmean_s.shape[0] // 3
    scratch = [scratch[SCRATCH_PER_STREAM * g:SCRATCH_PER_STREAM * (g + 1)] for g in range(2 * N_STREAMS)]
    scratch_a, scratch_c = scratch[:N_STREAMS], scratch[N_STREAMS:]

    @pl.when(qi == 0)
    def _():
        for g in range(N_STREAMS):
            _store_values_t(va_ref, g, scratch_a[g][2])
            _store_values_t(vc_ref, g, scratch_c[g][2])
        kf = kc_ref[...].astype(F32).reshape(n_blocks, MB_BLOCK, N_STREAMS * LANES)
        rest = jnp.sum(kf, axis=1) * (1.0 / MB_BLOCK)
        kmean_s[...] = jnp.zeros(kmean_s.shape, BF16)
        for part in range(3):
            term = rest.astype(BF16)
            kmean_s[part * stride:part * stride + n_blocks, :] = term
            rest = rest - term.astype(F32)

    def diff_stream(g):
        q2 = _split_heads(qa_ref[:, g * LANES:(g + 1) * LANES])

        def logits(kj):
            return _dot_t(_kv_block(ka_ref, kj, g), q2) + bias_ref[g, _tile_kind(kj, qi)]

        return logits, scratch_a[g], None

    def moba_stream(g):
        q2 = _split_heads(qc_ref[:, g * LANES:(g + 1) * LANES])
        parts = _dot_t(kmean_s[:, g * LANES:(g + 1) * LANES], q2)
        gate = parts[0:n_blocks] + parts[stride:stride + n_blocks] + parts[2 * stride:2 * stride + n_blocks]
        blk = lax.broadcasted_iota(jnp.int32, gate.shape, 0)

        def selected(kj):
            g_kj = jnp.sum(jnp.where(blk == kj, gate, 0.0), axis=0, keepdims=True)
            beats = ((gate > g_kj) | ((gate == g_kj) & (blk < kj))) & (blk < qi)
            rank = jnp.sum(beats.astype(F32), axis=0, keepdims=True)
            return (rank < MB_TOPK) | (kj >= qi)

        def logits(kj):
            return _dot_t(_kv_block(kc_ref, kj, g), q2) + bias_ref[N_STREAMS + g, _tile_kind(kj, qi)]

        return logits, scratch_c[g], selected

    streams = []
    for g in range(N_STREAMS):
        streams += [diff_stream(g), moba_stream(g)]
    outs = _flash(qi, streams)
    lv = lam_ref[...]
    lam = (jnp.exp(jnp.sum(lv[0:1] * lv[1:2], axis=1, keepdims=True))
           - jnp.exp(jnp.sum(lv[2:3] * lv[3:4], axis=1, keepdims=True)) + lam_init)
    for g in range(N_STREAMS):
        o = outs[2 * g]
        o = o[:, :t] - lam * o[:, t:]
        o = o * lax.rsqrt(jnp.mean(o * o, axis=0, keepdims=True) + EPS) * (g_ref[...] * (1.0 - lam_init))
        oa_ref[:, g * LANES:(g + 1) * LANES] = o.T.astype(oa_ref.dtype)
        oc_ref[:, g * LANES:(g + 1) * LANES] = _merge_heads_t(outs[2 * g + 1]).astype(oc_ref.dtype)


def _dense_attention(lam_vecs, subln_g, qkv, tiles, batch, seq, lam_init):
    assert DA_HEADS == N_STREAMS and MB_HEADS == 2 * N_STREAMS
    m = batch * seq
    nq = seq // ATT_T
    n_blocks = seq // MB_BLOCK
    w = N_STREAMS * LANES
    q_spec = lambda col: pl.BlockSpec((ATT_T, w), lambda b, i: (b * nq + i, col // N_STREAMS))
    kv_spec = lambda col: pl.BlockSpec((seq, w), lambda b, i: (b, col // N_STREAMS))
    out_spec = pl.BlockSpec((ATT_T, w), lambda b, i: (b * nq + i, 0))
    return pl.pallas_call(
        functools.partial(_dense_attn_kernel, lam_init=lam_init, n_blocks=n_blocks),
        grid=(batch, nq),
        in_specs=[pl.BlockSpec((4, DA_HD), lambda b, i: (0, 0)),
                  pl.BlockSpec((2 * DA_HD, 1), lambda b, i: (0, 0)),
                  q_spec(_QA), kv_spec(_KA), kv_spec(_VA), q_spec(_QC), kv_spec(_KC), kv_spec(_VC),
                  pl.BlockSpec((2 * N_STREAMS, 3, ATT_T, 2 * ATT_T), lambda b, i: (0, 0, 0, 0),
                               pipeline_mode=pl.Buffered(1))],
        out_specs=[out_spec, out_spec],
        out_shape=[jax.ShapeDtypeStruct((m, DA_W), BF16), jax.ShapeDtypeStruct((m, MB_W), BF16)],
        scratch_shapes=[pltpu.VMEM((3 * pl.cdiv(n_blocks, BF16_ROWS) * BF16_ROWS, w), BF16)]
        + 2 * _flash_scratch(2 * ATT_T, nq),
        compiler_params=_params("arbitrary", "arbitrary"),
        name="dense_attention",
    )(lam_vecs, subln_g.astype(F32).reshape(2 * DA_HD, 1), qkv, qkv, qkv, qkv, qkv, qkv, tiles)


def _swap_halves(x):
    return pltpu.roll(x.astype(F32), LANES // 2, 1).astype(x.dtype)


def _swa_kernel(sink_ref, q_ref, kp_ref, ko_ref, vp_ref, vo_ref, bias_ref, o_ref):
    qi = pl.program_id(1)
    t = SW_T
    half = LANES // 2
    lane = lax.broadcasted_iota(jnp.int32, (t, LANES), 1)
    feat = lax.broadcasted_iota(jnp.int32, (LANES, t), 0)
    first_of_seq = jnp.where(qi > 0, 0.0, NEG)
    for j in range(SW_TILES):
        rows = slice(j * t, (j + 1) * t)
        before = slice((j - 1) * t, j * t)
        out = {}
        for s, heads in enumerate((_SW_DIRECT, _SW_SWAPPED)):
            kp, vp = (kp_ref[...], vp_ref[...]) if j == 0 else (ko_ref[before, :], vo_ref[before, :])
            ko, vo = ko_ref[rows, :], vo_ref[rows, :]
            if s == 1:
                kp, ko, vp, vo = (_swap_halves(x) for x in (kp, ko, vp, vo))
            qs = []
            for h in heads:
                q = q_ref[rows, (h // 2) * LANES:(h // 2 + 1) * LANES]
                qs.append(jnp.where(lane >= half if h % 2 else lane < half, q, jnp.zeros_like(q)))
            qs = jnp.concatenate(qs, axis=0)
            cols = slice(s * len(heads) * t, (s + 1) * len(heads) * t)
            s_prev = _dot_t(kp, qs) + bias_ref[0:t, cols]
            if j == 0:
                s_prev = s_prev + first_of_seq
            s_own = _dot_t(ko, qs) + bias_ref[t:2 * t, cols]
            sink = sink_ref[:, cols]
            m = jnp.maximum(jnp.maximum(jnp.max(s_prev, axis=0, keepdims=True),
                                        jnp.max(s_own, axis=0, keepdims=True)), sink)
            p_prev = jnp.exp2(s_prev - m)
            p_own = jnp.exp2(s_own - m)
            l = (jnp.sum(p_prev, axis=0, keepdims=True) + jnp.sum(p_own, axis=0, keepdims=True)
                 + jnp.exp2(sink - m))
            o = (_pv_t(vp, p_prev.astype(BF16)) + _pv_t(vo, p_own.astype(BF16))) / l
            for c, h in enumerate(heads):
                out[h] = o[:, c * t:(c + 1) * t]
        for g in range(SW_QW // LANES):
            o_ref[rows, g * LANES:(g + 1) * LANES] = jnp.where(feat < half, out[2 * g], out[2 * g + 1]).T.astype(o_ref.dtype)


SW_TILES = 4


def _swa_attention(sink_row, qkv, win, batch, seq):
    m = batch * seq
    step = SW_TILES * SW_T
    assert seq % step == 0
    nq = seq // step
    cols = SW_HEADS * SW_T
    prev = lambda b, i: (jnp.maximum((b * nq + i) * SW_TILES - 1, 0), _KB)
    prev_v = lambda b, i: (jnp.maximum((b * nq + i) * SW_TILES - 1, 0), _VB)
    own = lambda b, i: (b * nq + i, _KB)
    own_v = lambda b, i: (b * nq + i, _VB)
    return pl.pallas_call(
        _swa_kernel,
        grid=(batch, nq),
        in_specs=[pl.BlockSpec((1, cols), lambda b, i: (0, 0)),
                  pl.BlockSpec((step, SW_QW), lambda b, i: (b * nq + i, _QB // SW_QW)),
                  pl.BlockSpec((SW_T, LANES), prev),
                  pl.BlockSpec((step, LANES), own),
                  pl.BlockSpec((SW_T, LANES), prev_v),
                  pl.BlockSpec((step, LANES), own_v),
                  pl.BlockSpec((2 * SW_T, cols), lambda b, i: (0, 0))],
        out_specs=pl.BlockSpec((step, SW_QW), lambda b, i: (b * nq + i, 0)),
        out_shape=jax.ShapeDtypeStruct((m, SW_QW), BF16),
        compiler_params=_params("arbitrary", "arbitrary"),
        name="swa_attention",
    )(sink_row, qkv, qkv, qkv, qkv, qkv, win)


def _mix_out_kernel(ya_ref, yb_ref, yc_ref, ga_ref, gb_ref, gc_ref, x_ref, woa_ref, wob_ref, woc_ref,
                    wout_ref, pg_ref, ng_ref, xo_ref, ho_ref):
    def branch(y_ref, w_ref, g_ref, rows):
        return g_ref[rows, :].astype(F32) * jnp.dot(y_ref[rows, :], w_ref[...], preferred_element_type=F32)

    tm = x_ref.shape[0]
    sub = MXU_N if tm % MXU_N == 0 else tm
    for r0 in range(0, tm, sub):
        rows = slice(r0, r0 + sub)
        mix = (branch(ya_ref, woa_ref, ga_ref, rows) + branch(yb_ref, wob_ref, gb_ref, rows)
               + branch(yc_ref, woc_ref, gc_ref, rows))
        z = jnp.dot(mix.astype(BF16), wout_ref[...], preferred_element_type=F32)
        xn = x_ref[rows, :] + _rms(z, pg_ref[...])
        xo_ref[rows, :] = xn
        ho_ref[rows, :] = _rms(xn, ng_ref[...]).astype(ho_ref.dtype)


def _mix_out(ya, yb, yc, gates, x2, woa, wob, woc, wout, post_g, next_g):
    m, d = x2.shape
    tm = _pick(m, (512, 256, 128))
    row = lambda i: (i, 0)
    const = lambda i: (0, 0)
    once = pl.Buffered(1)
    return pl.pallas_call(
        _mix_out_kernel,
        grid=(m // tm,),
        in_specs=[pl.BlockSpec((tm, DA_W), row), pl.BlockSpec((tm, SW_QW), row), pl.BlockSpec((tm, MB_W), row),
                  pl.BlockSpec((tm, d), lambda i: (i, 0)), pl.BlockSpec((tm, d), lambda i: (i, 1)),
                  pl.BlockSpec((tm, d), lambda i: (i, 2)),
                  pl.BlockSpec((tm, d), row),
                  pl.BlockSpec((DA_W, d), const, pipeline_mode=once),
                  pl.BlockSpec((SW_QW, d), const, pipeline_mode=once),
                  pl.BlockSpec((MB_W, d), const, pipeline_mode=once),
                  pl.BlockSpec((d, d), const, pipeline_mode=once),
                  pl.BlockSpec((1, d), const), pl.BlockSpec((1, d), const)],
        out_specs=[pl.BlockSpec((tm, d), row), pl.BlockSpec((tm, d), row)],
        out_shape=[jax.ShapeDtypeStruct((m, d), F32), jax.ShapeDtypeStruct((m, d), BF16)],
        compiler_params=_params("arbitrary"),
        name="mix_out",
    )(ya, yb, yc, gates, gates, gates, x2, woa, wob, woc, wout, post_g.reshape(1, d), next_g.reshape(1, d))


CONV_W = 3
HALO = 8


def _ffn_up_kernel(h_ref, wgf_ref, wvf_ref, cwg_ref, cwv_ref, cbg_ref, cbv_ref, o_ref, halo_g, halo_v, wg_ref, wv_ref,
                   *, tiles_per_seq):
    i = pl.program_id(1)
    tm = h_ref.shape[0]

    @pl.when(i == 0)
    def _():
        wg_ref[...] = wgf_ref[...].astype(BF16)
        wv_ref[...] = wvf_ref[...].astype(BF16)

    @pl.when(i % tiles_per_seq == 0)
    def _():
        halo_g[...] = jnp.zeros(halo_g.shape, F32)
        halo_v[...] = jnp.zeros(halo_v.shape, F32)

    h = h_ref[...]

    def conv(w_ref, cw_ref, cb_ref, halo_s):
        u = jnp.dot(h, w_ref[...], preferred_element_type=F32)
        cw, cb = cw_ref[...], cb_ref[...]

        def taps(u2, u1, u0):
            return cb + u2 * cw[0:1] + u1 * cw[1:2] + u0 * cw[2:3]

        head = jnp.concatenate([halo_s[...], u[0:HALO]], axis=0)
        first = taps(head[HALO - 2:2 * HALO - 2], head[HALO - 1:2 * HALO - 1], head[HALO:])
        body = taps(pltpu.roll(u, 2, 0), pltpu.roll(u, 1, 0), u)
        halo_s[...] = u[tm - HALO:tm]
        return jnp.concatenate([first, body[HALO:]], axis=0)

    gate = conv(wg_ref, cwg_ref, cbg_ref, halo_g)
    val = conv(wv_ref, cwv_ref, cbv_ref, halo_v)
    o_ref[...] = (jax.nn.gelu(gate, approximate=True) * val).astype(o_ref.dtype)


def _ffn_up(h, w_up, conv_w, conv_b, layer, seq):
    m, d = h.shape
    f = w_up.shape[2] // 2
    tm = _pick(seq, (1024, 512, 256, 128))
    tn = _pick(f, (512, 384, 256, 128))
    nj = f // tn
    gate_col = lambda j, i: (layer, 0, j)
    val_col = lambda j, i: (layer, 0, nj + j)
    conv_b = conv_b.reshape(conv_b.shape[0], 1, 2 * f)
    return pl.pallas_call(
        functools.partial(_ffn_up_kernel, tiles_per_seq=seq // tm),
        grid=(nj, m // tm),
        in_specs=[pl.BlockSpec((tm, d), lambda j, i: (i, 0)),
                  pl.BlockSpec((None, d, tn), gate_col), pl.BlockSpec((None, d, tn), val_col),
                  pl.BlockSpec((None, CONV_W, tn), gate_col), pl.BlockSpec((None, CONV_W, tn), val_col),
                  pl.BlockSpec((None, 1, tn), gate_col), pl.BlockSpec((None, 1, tn), val_col)],
        out_specs=pl.BlockSpec((tm, tn), lambda j, i: (i, j)),
        out_shape=jax.ShapeDtypeStruct((m, f), BF16),
        scratch_shapes=[pltpu.VMEM((HALO, tn), F32), pltpu.VMEM((HALO, tn), F32),
                        pltpu.VMEM((d, tn), BF16), pltpu.VMEM((d, tn), BF16)],
        compiler_params=_params("arbitrary", "arbitrary"),
        name="ffn_up",
    )(h, w_up, w_up, conv_w, conv_w, conv_b, conv_b)


def _ffn_down_kernel(a_ref, w_ref, x_ref, pg_ref, ng_ref, xo_ref, *ho_ref):
    z = jnp.dot(a_ref[...], w_ref[...], preferred_element_type=F32)
    xn = x_ref[...] + _rms(z, pg_ref[...])
    xo_ref[...] = xn
    if ho_ref:
        ho_ref[0][...] = _rms(xn, ng_ref[...]).astype(ho_ref[0].dtype)


def _ffn_down(a, w_down, x2, post_g, next_g):
    m, d = x2.shape
    f = a.shape[1]
    tm = _pick(m, (256, 128))
    emit_next = next_g is not None
    row = lambda i: (i, 0)
    const = lambda i: (0, 0)
    out_specs = [pl.BlockSpec((tm, d), row)]
    out_shape = [jax.ShapeDtypeStruct((m, d), F32)]
    if emit_next:
        out_specs.append(pl.BlockSpec((tm, d), row))
        out_shape.append(jax.ShapeDtypeStruct((m, d), BF16))
    ng = next_g if emit_next else post_g
    res = pl.pallas_call(
        _ffn_down_kernel,
        grid=(m // tm,),
        in_specs=[pl.BlockSpec((tm, f), row), pl.BlockSpec((f, d), const, pipeline_mode=pl.Buffered(1)),
                  pl.BlockSpec((tm, d), row), pl.BlockSpec((1, d), const), pl.BlockSpec((1, d), const)],
        out_specs=out_specs,
        out_shape=out_shape,
        compiler_params=_params("arbitrary"),
        name="ffn_down",
    )(a, w_down, x2, post_g.reshape(1, d), ng.reshape(1, d))
    return (res[0], res[1]) if emit_next else (res[0], None)


@jax.jit
def _trunk(x, rel_bias_table, w_in, b_gate, lam_q1, lam_k1, lam_q2, lam_k2, diff_subln_g, sinks, w_oa, w_ob,
           w_oc, w_out, pre_mix_g, post_mix_g, pre_ffn_g, post_ffn_g, w_up, conv_w, conv_b, w_down):
    batch, seq, d = x.shape
    depth = w_in.shape[0]
    assert seq % ATT_T == 0 and seq % MB_BLOCK == 0 and d % LANES == 0
    assert w_in.shape[2] == QKV_W + N_BRANCH * d
    m = batch * seq

    tab_flat = rel_bias_table.astype(F32).reshape(-1)
    tiles, win = _bias_tiles(tab_flat)
    sw_order = jnp.array(_SW_ORDER)

    x2 = x.reshape(m, d)
    h = _prenorm(x2, pre_mix_g[0])
    for l in range(depth):
        qkv = _in_proj(h, w_in, l)
        gates = _in_gates(h, w_in, b_gate, l)

        lam_init = 0.8 - 0.6 * math.exp(-0.3 * l)
        lam_vecs = jnp.stack([lam_q1[l], lam_k1[l], lam_q2[l], lam_k2[l]]).astype(F32)
        ya, yc = _dense_attention(lam_vecs, diff_subln_g[l], qkv, tiles, batch, seq, lam_init)
        sink_row = jnp.repeat(sinks[l].astype(F32)[sw_order] * LOG2E, SW_T).reshape(1, SW_HEADS * SW_T)
        yb = _swa_attention(sink_row, qkv, win, batch, seq)

        x2, h = _mix_out(ya, yb, yc, gates, x2, w_oa[l].astype(BF16), w_ob[l].astype(BF16),
                         w_oc[l].astype(BF16), w_out[l].astype(BF16), post_mix_g[l], pre_ffn_g[l])

        a = _ffn_up(h, w_up, conv_w, conv_b, l, seq)
        next_g = pre_mix_g[l + 1] if l + 1 < depth else None
        x2, h = _ffn_down(a, w_down[l].astype(BF16), x2, post_ffn_g[l], next_g)
    return x2.reshape(batch, seq, d)


def kernel(x, rel_bias_table, w_in, b_gate, lam_q1, lam_k1, lam_q2, lam_k2, diff_subln_g, sinks, w_oa, w_ob, w_oc, w_out, pre_mix_g, post_mix_g, pre_ffn_g, post_ffn_g, w_up, conv_w, conv_b, w_down):
    return _trunk(x, rel_bias_table, w_in, b_gate, lam_q1, lam_k1, lam_q2, lam_k2, diff_subln_g, sinks, w_oa, w_ob,
                  w_oc, w_out, pre_mix_g, post_mix_g, pre_ffn_g, post_ffn_g, w_up, conv_w, conv_b, w_down)
```

```python
import functools
import math

import numpy as np
import jax
import jax.numpy as jnp
from jax import lax
from jax.experimental import pallas as pl
from jax.experimental.pallas import tpu as pltpu

DA_HEADS = 4
DA_HD = 64
DA_W = DA_HEADS * 2 * DA_HD
SW_HEADS = 8
SW_KV = 2
SW_HD = 64
WINDOW = 128
SW_QW = SW_HEADS * SW_HD
SW_KW = SW_KV * SW_HD
MB_HEADS = 8
MB_HD = 64
MB_W = MB_HEADS * MB_HD
MB_BLOCK = 256
MB_TOPK = 3
N_BUCKETS = 32
MAX_DIST = 128
N_ATT_HEADS = DA_HEADS + SW_HEADS + MB_HEADS
N_BRANCH = 3
QKV_W = 3 * DA_W + SW_QW + 2 * SW_KW + 3 * MB_W
EPS = 1e-6

LANES = 128
MXU_N = 256
BF16_ROWS = 16
ATT_T = 256
SW_T = WINDOW
NEG = -1e30
LOG2E = math.log2(math.e)
Q_SCALE = DA_HD ** -0.5 * LOG2E
VMEM_LIMIT = 56 * 1024 * 1024

_QA, _KA, _VA = 0, DA_W // LANES, 2 * DA_W // LANES
_QC = 3 * DA_W // LANES
_KC = _QC + MB_W // LANES
_VC = _KC + MB_W // LANES
_QB = 3 * DA_W + 3 * MB_W
_KB = (_QB + SW_QW) // LANES
_VB = _KB + 1
_SW_DIRECT = tuple(h for h in range(SW_HEADS) if h % 2 == h // (SW_HEADS // SW_KV))
_SW_SWAPPED = tuple(h for h in range(SW_HEADS) if h % 2 != h // (SW_HEADS // SW_KV))
_SW_ORDER = _SW_DIRECT + _SW_SWAPPED

F32 = jnp.float32
BF16 = jnp.bfloat16


def _pick(n, candidates):
    for c in candidates:
        if n % c == 0:
            return c
    raise ValueError(f"no tile in {candidates} divides {n}")


def _params(*sem):
    return pltpu.CompilerParams(dimension_semantics=sem, vmem_limit_bytes=VMEM_LIMIT)


def _rms(x, g):
    return x * lax.rsqrt(jnp.mean(x * x, axis=-1, keepdims=True) + EPS) * g


def _dot_t(a, b, **kw):
    return lax.dot_general(a, b, (((1,), (1,)), ((), ())), preferred_element_type=F32, **kw)


def _rel_bucket(dist):
    n = jnp.maximum(dist, 0)
    max_exact = N_BUCKETS // 2
    nf = jnp.maximum(n, 1).astype(F32)
    large = max_exact + (jnp.log(nf / max_exact) / math.log(MAX_DIST / max_exact)
                         * (N_BUCKETS - max_exact)).astype(jnp.int32)
    large = jnp.minimum(large, N_BUCKETS - 1)
    return jnp.where(n < max_exact, n, large)


def _bias_lookup(tab_ref, head, dist):
    bucket = _rel_bucket(dist)
    acc = jnp.zeros(dist.shape, F32)
    for b in range(N_BUCKETS):
        acc = jnp.where(bucket == b, tab_ref[b * N_ATT_HEADS + head], acc)
    return acc * LOG2E


FAR, NEAR, DIAG = 0, 1, 2


def _tile_kind(kj, qi):
    return jnp.clip(kj - qi + DIAG, FAR, DIAG)


def _bias_kernel(tab_ref, o_ref):
    e = pl.program_id(0)
    moba0 = DA_HEADS + SW_HEADS + 2 * (e - DA_HEADS)
    heads = (jnp.where(e < DA_HEADS, e, moba0), jnp.where(e < DA_HEADS, e, moba0 + 1))
    d = (lax.broadcasted_iota(jnp.int32, (ATT_T, ATT_T), 1)
         - lax.broadcasted_iota(jnp.int32, (ATT_T, ATT_T), 0))
    half = ATT_T // 2
    for c, head in enumerate(heads):
        cols = slice(c * ATT_T, (c + 1) * ATT_T)
        last = tab_ref[(N_BUCKETS - 1) * N_ATT_HEADS + head] * LOG2E
        o_ref[0, FAR, :, cols] = jnp.full((ATT_T, ATT_T), last, F32)
        o_ref[0, NEAR, 0:half, cols] = jnp.full((half, ATT_T), last, F32)
        o_ref[0, NEAR, half:ATT_T, cols] = _bias_lookup(tab_ref, head, d[half:] + ATT_T)
        o_ref[0, DIAG, :, cols] = jnp.where(d >= 0, _bias_lookup(tab_ref, head, d), NEG)


def _window_kernel(tab_ref, win_ref):
    c = pl.program_id(0)
    head = jnp.int32(DA_HEADS + _SW_ORDER[-1])
    for pos, h in enumerate(_SW_ORDER[:-1]):
        head = jnp.where(c == pos, DA_HEADS + h, head)
    d = (lax.broadcasted_iota(jnp.int32, (2 * SW_T, SW_T), 1) + SW_T
         - lax.broadcasted_iota(jnp.int32, (2 * SW_T, SW_T), 0))
    win_ref[...] = jnp.where((d >= 0) & (d < WINDOW), _bias_lookup(tab_ref, head, d), NEG)


def _bias_tiles(tab_flat):
    assert ATT_T // 2 + 1 >= MAX_DIST
    n = DA_HEADS + MB_HEADS // 2
    tiles = pl.pallas_call(
        _bias_kernel,
        grid=(n,),
        in_specs=[pl.BlockSpec(memory_space=pltpu.SMEM)],
        out_specs=pl.BlockSpec((1, 3, ATT_T, 2 * ATT_T), lambda e: (e, 0, 0, 0)),
        out_shape=jax.ShapeDtypeStruct((n, 3, ATT_T, 2 * ATT_T), F32),
        compiler_params=_params("arbitrary"),
        name="bias_tiles",
    )(tab_flat)
    win = pl.pallas_call(
        _window_kernel,
        grid=(SW_HEADS,),
        in_specs=[pl.BlockSpec(memory_space=pltpu.SMEM)],
        out_specs=pl.BlockSpec((2 * SW_T, SW_T), lambda c: (0, c)),
        out_shape=jax.ShapeDtypeStruct((2 * SW_T, SW_HEADS * SW_T), F32),
        compiler_params=_params("arbitrary"),
        name="window_tiles",
    )(tab_flat)
    return tiles, win


def _norm_kernel(x_ref, g_ref, o_ref):
    o_ref[...] = _rms(x_ref[...], g_ref[...]).astype(o_ref.dtype)


def _prenorm(x2, g):
    m, d = x2.shape
    tm = _pick(m, (512, 256, 128))
    return pl.pallas_call(
        _norm_kernel,
        grid=(m // tm,),
        in_specs=[pl.BlockSpec((tm, d), lambda i: (i, 0)), pl.BlockSpec((1, d), lambda i: (0, 0))],
        out_specs=pl.BlockSpec((tm, d), lambda i: (i, 0)),
        out_shape=jax.ShapeDtypeStruct((m, d), BF16),
        compiler_params=_params("arbitrary"),
        name="prenorm",
    )(x2, g.reshape(1, d))


def _proj_kernel(h_ref, *refs):
    w_refs, scale_ref, o_ref, wb_s = refs[:-3], refs[-3], refs[-2], refs[-1]

    @pl.when(pl.program_id(1) == 0)
    def _():
        for c, w_ref in enumerate(w_refs):
            cols = slice(c * PROJ_BLOCK, (c + 1) * PROJ_BLOCK)
            wb_s[:, cols] = (w_ref[...] * scale_ref[:, cols]).astype(BF16)

    o_ref[...] = jnp.dot(h_ref[...], wb_s[...], preferred_element_type=F32).astype(o_ref.dtype)


PROJ_BLOCK = 256
PROJ_BLOCKS_PER_STEP = 5


def _proj_source_block(jb):
    n_a, n_b, n_c = (3 * DA_W // PROJ_BLOCK, (SW_QW + 2 * SW_KW) // PROJ_BLOCK, 3 * MB_W // PROJ_BLOCK)
    return jnp.where(jb < n_a, jb, jnp.where(jb < n_a + n_c, jb + n_b, jb - n_c))


def _in_proj(h, w_in, layer):
    m, d = h.shape
    tm = _pick(m, (1024, 512, 256, 128))
    tn = PROJ_BLOCKS_PER_STEP * PROJ_BLOCK
    assert QKV_W % tn == 0 and (SW_QW + 2 * SW_KW) % PROJ_BLOCK == 0 and DA_W % PROJ_BLOCK == 0
    scale = np.ones((1, QKV_W), np.float32)
    for q0 in (_QA * LANES, _QC * LANES, _QB):
        scale[:, q0:q0 + DA_W] = Q_SCALE
    w_specs = [pl.BlockSpec((None, d, PROJ_BLOCK),
                            lambda j, i, c=c: (layer, 0, _proj_source_block(j * PROJ_BLOCKS_PER_STEP + c)))
               for c in range(PROJ_BLOCKS_PER_STEP)]
    return pl.pallas_call(
        _proj_kernel,
        grid=(QKV_W // tn, m // tm),
        in_specs=[pl.BlockSpec((tm, d), lambda j, i: (i, 0))] + w_specs + [pl.BlockSpec((1, tn), lambda j, i: (0, j))],
        out_specs=pl.BlockSpec((tm, tn), lambda j, i: (i, j)),
        out_shape=jax.ShapeDtypeStruct((m, QKV_W), BF16),
        scratch_shapes=[pltpu.VMEM((d, tn), BF16)],
        compiler_params=_params("arbitrary", "arbitrary"),
        name="in_proj",
    )(h, *([w_in] * PROJ_BLOCKS_PER_STEP), jnp.asarray(scale))


def _gate_kernel(h_ref, w_ref, b_ref, o_ref, wb_s):
    @pl.when(pl.program_id(1) == 0)
    def _():
        wb_s[...] = w_ref[...].astype(BF16)

    acc = jnp.dot(h_ref[...], wb_s[...], preferred_element_type=F32)
    o_ref[...] = (0.5 * jnp.tanh(0.5 * (acc + b_ref[...])) + 0.5).astype(o_ref.dtype)


def _in_gates(h, w_in, b_gate, layer):
    m, d = h.shape
    n = w_in.shape[2] - QKV_W
    tm = _pick(m, (1024, 512, 256, 128))
    tn = _pick(math.gcd(n, QKV_W), (768, 512, 256, 128))
    col0 = QKV_W // tn
    return pl.pallas_call(
        _gate_kernel,
        grid=(n // tn, m // tm),
        in_specs=[pl.BlockSpec((tm, d), lambda j, i: (i, 0)),
                  pl.BlockSpec((None, d, tn), lambda j, i: (layer, 0, col0 + j)),
                  pl.BlockSpec((None, 1, tn), lambda j, i: (layer, 0, j))],
        out_specs=pl.BlockSpec((tm, tn), lambda j, i: (i, j)),
        out_shape=jax.ShapeDtypeStruct((m, n), BF16),
        scratch_shapes=[pltpu.VMEM((d, tn), BF16)],
        compiler_params=_params("arbitrary", "arbitrary"),
        name="in_gates",
    )(h, w_in, b_gate.reshape(b_gate.shape[0], 1, n))


def _split_heads(q):
    lane = lax.broadcasted_iota(jnp.int32, q.shape, 1)
    zero = jnp.zeros_like(q)
    return jnp.concatenate([jnp.where(lane < LANES // 2, q, zero),
                            jnp.where(lane >= LANES // 2, q, zero)], axis=0)


def _merge_heads_t(o):
    t = o.shape[1] // 2
    feat = lax.broadcasted_iota(jnp.int32, (LANES, t), 0)
    return jnp.where(feat < LANES // 2, o[:, :t], o[:, t:]).T


def _pv_t(v, p):
    return lax.dot_general(v, p, (((0,), (0,)), ((), ())), preferred_element_type=F32)


def _softmax_stats(s, m_tile, m, keep=None):
    if keep is not None:
        m_tile = jnp.where(keep, m_tile, NEG)
    m_new = jnp.maximum(m, m_tile)
    alpha = jnp.exp2(m - m_new)
    p = jnp.exp2(s - (m_new if keep is None else jnp.where(keep, m_new, -NEG)))
    return m_new, p.astype(BF16), alpha


VT_ROWS = LANES + BF16_ROWS


def _store_values_t(v_ref, group, vt_s):
    for kj in range(vt_s.shape[0]):
        v = v_ref[kj * ATT_T:(kj + 1) * ATT_T, group * LANES:(group + 1) * LANES]
        vt_s[kj, 0:LANES, :] = v.astype(F32).T.astype(BF16)
        vt_s[kj, LANES:VT_ROWS, :] = jnp.ones((VT_ROWS - LANES, ATT_T), BF16)


def _flash(qi, streams):
    def issue_logits(stream, kj):
        logits_fn, (s_s, _, _), _ = stream
        s = logits_fn(kj)
        s_s[...] = s
        return jnp.max(s, axis=0, keepdims=True)

    def accumulate(stream, kj, m_tile, m):
        _, (s_s, acc_s, vt_s), keep_fn = stream
        m, p, alpha = _softmax_stats(s_s[...], m_tile, m, None if keep_fn is None else keep_fn(kj))
        return m, alpha * acc_s[...] + jnp.dot(vt_s[kj], p, preferred_element_type=F32)

    carry = []
    for stream in streams:
        acc_s = stream[1][1]
        acc_s[...] = jnp.zeros(acc_s.shape, F32)
        carry.append((jnp.full((1, acc_s.shape[1]), -jnp.inf, F32), issue_logits(stream, 0)))

    def body(kj, carry):
        out = []
        for stream, (m, m_tile) in zip(streams, carry):
            m, acc = accumulate(stream, kj, m_tile, m)
            stream[1][1][...] = acc
            out.append((m, issue_logits(stream, kj + 1)))
        return tuple(out)

    carry = lax.fori_loop(0, qi, body, tuple(carry))
    outs = []
    for stream, (m, m_tile) in zip(streams, carry):
        _, acc = accumulate(stream, qi, m_tile, m)
        outs.append(acc[0:LANES] / acc[LANES:LANES + 1])
    return outs


N_STREAMS = 4


SCRATCH_PER_STREAM = 3


def _flash_scratch(n_queries, n_tiles):
    return N_STREAMS * [pltpu.VMEM((ATT_T, n_queries), F32), pltpu.VMEM((VT_ROWS, n_queries), F32),
                        pltpu.VMEM((n_tiles, VT_ROWS, ATT_T), BF16)]


def _kv_block(ref, kj, group):
    return ref[pl.ds(pl.multiple_of(kj * ATT_T, ATT_T), ATT_T), group * LANES:(group + 1) * LANES]


def _dense_attn_kernel(lam_ref, g_ref, qa_ref, ka_ref, va_ref, qc_ref, kc_ref, vc_ref, bias_ref, oa_ref, oc_ref,
                       kmean_s, *scratch, lam_init, n_blocks):
    qi = pl.program_id(1)
    t = ATT_T
    stride = kmean_s.shape[0] // 3
    scratch = [scratch[SCRATCH_PER_STREAM * g:SCRATCH_PER_STREAM * (g + 1)] for g in range(2 * N_STREAMS)]
    scratch_a, scratch_c = scratch[:N_STREAMS], scratch[N_STREAMS:]

    @pl.when(qi == 0)
    def _():
        for g in range(N_STREAMS):
            _store_values_t(va_ref, g, scratch_a[g][2])
            _store_values_t(vc_ref, g, scratch_c[g][2])
        kf = kc_ref[...].astype(F32).reshape(n_blocks, MB_BLOCK, N_STREAMS * LANES)
        rest = jnp.sum(kf, axis=1) * (1.0 / MB_BLOCK)
        kmean_s[...] = jnp.zeros(kmean_s.shape, BF16)
        for part in range(3):
            term = rest.astype(BF16)
            kmean_s[part * stride:part * stride + n_blocks, :] = term
            rest = rest - term.astype(F32)

    def diff_stream(g):
        q2 = _split_heads(qa_ref[:, g * LANES:(g + 1) * LANES])

        def logits(kj):
            return _dot_t(_kv_block(ka_ref, kj, g), q2) + bias_ref[g, _tile_kind(kj, qi)]

        return logits, scratch_a[g], None

    def moba_stream(g):
        q2 = _split_heads(qc_ref[:, g * LANES:(g + 1) * LANES])
        parts = _dot_t(kmean_s[:, g * LANES:(g + 1) * LANES], q2)
        gate = parts[0:n_blocks] + parts[stride:stride + n_blocks] + parts[2 * stride:2 * stride + n_blocks]
        blk = lax.broadcasted_iota(jnp.int32, gate.shape, 0)

        def selected(kj):
            g_kj = jnp.sum(jnp.where(blk == kj, gate, 0.0), axis=0, keepdims=True)
            beats = ((gate > g_kj) | ((gate == g_kj) & (blk < kj))) & (blk < qi)
            rank = jnp.sum(beats.astype(F32), axis=0, keepdims=True)
            return (rank < MB_TOPK) | (kj >= qi)

        def logits(kj):
            return _dot_t(_kv_block(kc_ref, kj, g), q2) + bias_ref[N_STREAMS + g, _tile_kind(kj, qi)]

        return logits, scratch_c[g], selected

    streams = []
    for g in range(N_STREAMS):
        streams += [diff_stream(g), moba_stream(g)]
    outs = _flash(qi, streams)
    lv = lam_ref[...]
    lam = (jnp.exp(jnp.sum(lv[0:1] * lv[1:2], axis=1, keepdims=True))
           - jnp.exp(jnp.sum(lv[2:3] * lv[3:4], axis=1, keepdims=True)) + lam_init)
    for g in range(N_STREAMS):
        o = outs[2 * g]
        o = o[:, :t] - lam * o[:, t:]
        o = o * lax.rsqrt(jnp.mean(o * o, axis=0, keepdims=True) + EPS) * (g_ref[...] * (1.0 - lam_init))
        oa_ref[:, g * LANES:(g + 1) * LANES] = o.T.astype(oa_ref.dtype)
        oc_ref[:, g * LANES:(g + 1) * LANES] = _merge_heads_t(outs[2 * g + 1]).astype(oc_ref.dtype)


def _dense_attention(lam_vecs, subln_g, qkv, tiles, batch, seq, lam_init):
    assert DA_HEADS == N_STREAMS and MB_HEADS == 2 * N_STREAMS
    m = batch * seq
    nq = seq // ATT_T
    n_blocks = seq // MB_BLOCK
    w = N_STREAMS * LANES
    q_spec = lambda col: pl.BlockSpec((ATT_T, w), lambda b, i: (b * nq + i, col // N_STREAMS))
    kv_spec = lambda col: pl.BlockSpec((seq, w), lambda b, i: (b, col // N_STREAMS))
    out_spec = pl.BlockSpec((ATT_T, w), lambda b, i: (b * nq + i, 0))
    return pl.pallas_call(
        functools.partial(_dense_attn_kernel, lam_init=lam_init, n_blocks=n_blocks),
        grid=(batch, nq),
        in_specs=[pl.BlockSpec((4, DA_HD), lambda b, i: (0, 0)),
                  pl.BlockSpec((2 * DA_HD, 1), lambda b, i: (0, 0)),
                  q_spec(_QA), kv_spec(_KA), kv_spec(_VA), q_spec(_QC), kv_spec(_KC), kv_spec(_VC),
                  pl.BlockSpec((2 * N_STREAMS, 3, ATT_T, 2 * ATT_T), lambda b, i: (0, 0, 0, 0),
                               pipeline_mode=pl.Buffered(1))],
        out_specs=[out_spec, out_spec],
        out_shape=[jax.ShapeDtypeStruct((m, DA_W), BF16), jax.ShapeDtypeStruct((m, MB_W), BF16)],
        scratch_shapes=[pltpu.VMEM((3 * pl.cdiv(n_blocks, BF16_ROWS) * BF16_ROWS, w), BF16)]
        + 2 * _flash_scratch(2 * ATT_T, nq),
        compiler_params=_params("arbitrary", "arbitrary"),
        name="dense_attention",
    )(lam_vecs, subln_g.astype(F32).reshape(2 * DA_HD, 1), qkv, qkv, qkv, qkv, qkv, qkv, tiles)


def _swap_halves(x):
    return pltpu.roll(x.astype(F32), LANES // 2, 1).astype(x.dtype)


def _swa_kernel(sink_ref, q_ref, kp_ref, ko_ref, vp_ref, vo_ref, bias_ref, o_ref):
    qi = pl.program_id(1)
    t = SW_T
    half = LANES // 2
    lane = lax.broadcasted_iota(jnp.int32, (t, LANES), 1)
    feat = lax.broadcasted_iota(jnp.int32, (LANES, t), 0)
    first_of_seq = jnp.where(qi > 0, 0.0, NEG)
    for j in range(SW_TILES):
        rows = slice(j * t, (j + 1) * t)
        before = slice((j - 1) * t, j * t)
        out = {}
        for s, heads in enumerate((_SW_DIRECT, _SW_SWAPPED)):
            kp, vp = (kp_ref[...], vp_ref[...]) if j == 0 else (ko_ref[before, :], vo_ref[before, :])
            ko, vo = ko_ref[rows, :], vo_ref[rows, :]
            if s == 1:
                kp, ko, vp, vo = (_swap_halves(x) for x in (kp, ko, vp, vo))
            qs = []
            for h in heads:
                q = q_ref[rows, (h // 2) * LANES:(h // 2 + 1) * LANES]
                qs.append(jnp.where(lane >= half if h % 2 else lane < half, q, jnp.zeros_like(q)))
            qs = jnp.concatenate(qs, axis=0)
            cols = slice(s * len(heads) * t, (s + 1) * len(heads) * t)
            s_prev = _dot_t(kp, qs) + bias_ref[0:t, cols]
            if j == 0:
                s_prev = s_prev + first_of_seq
            s_own = _dot_t(ko, qs) + bias_ref[t:2 * t, cols]
            sink = sink_ref[:, cols]
            m = jnp.maximum(jnp.maximum(jnp.max(s_prev, axis=0, keepdims=True),
                                        jnp.max(s_own, axis=0, keepdims=True)), sink)
            p_prev = jnp.exp2(s_prev - m)
            p_own = jnp.exp2(s_own - m)
            l = (jnp.sum(p_prev, axis=0, keepdims=True) + jnp.sum(p_own, axis=0, keepdims=True)
                 + jnp.exp2(sink - m))
            o = (_pv_t(vp, p_prev.astype(BF16)) + _pv_t(vo, p_own.astype(BF16))) / l
            for c, h in enumerate(heads):
                out[h] = o[:, c * t:(c + 1) * t]
        for g in range(SW_QW // LANES):
            o_ref[rows, g * LANES:(g + 1) * LANES] = jnp.where(feat < half, out[2 * g], out[2 * g + 1]).T.astype(o_ref.dtype)


SW_TILES = 4


def _swa_attention(sink_row, qkv, win, batch, seq):
    m = batch * seq
    step = SW_TILES * SW_T
    assert seq % step == 0
    nq = seq // step
    cols = SW_HEADS * SW_T
    prev = lambda b, i: (jnp.maximum((b * nq + i) * SW_TILES - 1, 0), _KB)
    prev_v = lambda b, i: (jnp.maximum((b * nq + i) * SW_TILES - 1, 0), _VB)
    own = lambda b, i: (b * nq + i, _KB)
    own_v = lambda b, i: (b * nq + i, _VB)
    return pl.pallas_call(
        _swa_kernel,
        grid=(batch, nq),
        in_specs=[pl.BlockSpec((1, cols), lambda b, i: (0, 0)),
                  pl.BlockSpec((step, SW_QW), lambda b, i: (b * nq + i, _QB // SW_QW)),
                  pl.BlockSpec((SW_T, LANES), prev),
                  pl.BlockSpec((step, LANES), own),
                  pl.BlockSpec((SW_T, LANES), prev_v),
                  pl.BlockSpec((step, LANES), own_v),
                  pl.BlockSpec((2 * SW_T, cols), lambda b, i: (0, 0))],
        out_specs=pl.BlockSpec((step, SW_QW), lambda b, i: (b * nq + i, 0)),
        out_shape=jax.ShapeDtypeStruct((m, SW_QW), BF16),
        compiler_params=_params("arbitrary", "arbitrary"),
        name="swa_attention",
    )(sink_row, qkv, qkv, qkv, qkv, qkv, win)


def _mix_out_kernel(ya_ref, yb_ref, yc_ref, ga_ref, gb_ref, gc_ref, x_ref, woa_ref, wob_ref, woc_ref,
                    wout_ref, pg_ref, ng_ref, xo_ref, ho_ref):
    def branch(y_ref, w_ref, g_ref, rows):
        return g_ref[rows, :].astype(F32) * jnp.dot(y_ref[rows, :], w_ref[...], preferred_element_type=F32)

    tm = x_ref.shape[0]
    sub = MXU_N if tm % MXU_N == 0 else tm
    for r0 in range(0, tm, sub):
        rows = slice(r0, r0 + sub)
        mix = (branch(ya_ref, woa_ref, ga_ref, rows) + branch(yb_ref, wob_ref, gb_ref, rows)
               + branch(yc_ref, woc_ref, gc_ref, rows))
        z = jnp.dot(mix.astype(BF16), wout_ref[...], preferred_element_type=F32)
        xn = x_ref[rows, :] + _rms(z, pg_ref[...])
        xo_ref[rows, :] = xn
        ho_ref[rows, :] = _rms(xn, ng_ref[...]).astype(ho_ref.dtype)


def _mix_out(ya, yb, yc, gates, x2, woa, wob, woc, wout, layer, post_g, next_g):
    m, d = x2.shape
    tm = _pick(m, (512, 256, 128))
    row = lambda i: (i, 0)
    const = lambda i: (0, 0)
    weight = lambda rows: pl.BlockSpec((None, rows, d), lambda i: (layer, 0, 0), pipeline_mode=pl.Buffered(1))
    return pl.pallas_call(
        _mix_out_kernel,
        grid=(m // tm,),
        in_specs=[pl.BlockSpec((tm, DA_W), row), pl.BlockSpec((tm, SW_QW), row), pl.BlockSpec((tm, MB_W), row),
                  pl.BlockSpec((tm, d), lambda i: (i, 0)), pl.BlockSpec((tm, d), lambda i: (i, 1)),
                  pl.BlockSpec((tm, d), lambda i: (i, 2)),
                  pl.BlockSpec((tm, d), row),
                  weight(DA_W), weight(SW_QW), weight(MB_W), weight(d),
                  pl.BlockSpec((1, d), const), pl.BlockSpec((1, d), const)],
        out_specs=[pl.BlockSpec((tm, d), row), pl.BlockSpec((tm, d), row)],
        out_shape=[jax.ShapeDtypeStruct((m, d), F32), jax.ShapeDtypeStruct((m, d), BF16)],
        compiler_params=_params("arbitrary"),
        name="mix_out",
    )(ya, yb, yc, gates, gates, gates, x2, woa, wob, woc, wout, post_g.reshape(1, d), next_g.reshape(1, d))


CONV_W = 3
HALO = 8


def _ffn_up_kernel(h_ref, wgf_ref, wvf_ref, cwg_ref, cwv_ref, cbg_ref, cbv_ref, o_ref, halo_g, halo_v, wg_ref, wv_ref,
                   *, tiles_per_seq):
    i = pl.program_id(1)
    tm = h_ref.shape[0]

    @pl.when(i == 0)
    def _():
        wg_ref[...] = wgf_ref[...].astype(BF16)
        wv_ref[...] = wvf_ref[...].astype(BF16)

    @pl.when(i % tiles_per_seq == 0)
    def _():
        halo_g[...] = jnp.zeros(halo_g.shape, F32)
        halo_v[...] = jnp.zeros(halo_v.shape, F32)

    h = h_ref[...]

    def conv(w_ref, cw_ref, cb_ref, halo_s):
        u = jnp.dot(h, w_ref[...], preferred_element_type=F32)
        cw, cb = cw_ref[...], cb_ref[...]

        def taps(u2, u1, u0):
            return cb + u2 * cw[0:1] + u1 * cw[1:2] + u0 * cw[2:3]

        head = jnp.concatenate([halo_s[...], u[0:HALO]], axis=0)
        first = taps(head[HALO - 2:2 * HALO - 2], head[HALO - 1:2 * HALO - 1], head[HALO:])
        body = taps(pltpu.roll(u, 2, 0), pltpu.roll(u, 1, 0), u)
        halo_s[...] = u[tm - HALO:tm]
        return jnp.concatenate([first, body[HALO:]], axis=0)

    gate = conv(wg_ref, cwg_ref, cbg_ref, halo_g)
    val = conv(wv_ref, cwv_ref, cbv_ref, halo_v)
    o_ref[...] = (jax.nn.gelu(gate, approximate=True) * val).astype(o_ref.dtype)


def _ffn_up(h, w_up, conv_w, conv_b, layer, seq):
    m, d = h.shape
    f = w_up.shape[2] // 2
    tm = _pick(seq, (1024, 512, 256, 128))
    tn = _pick(f, (512, 384, 256, 128))
    nj = f // tn
    gate_col = lambda j, i: (layer, 0, j)
    val_col = lambda j, i: (layer, 0, nj + j)
    conv_b = conv_b.reshape(conv_b.shape[0], 1, 2 * f)
    return pl.pallas_call(
        functools.partial(_ffn_up_kernel, tiles_per_seq=seq // tm),
        grid=(nj, m // tm),
        in_specs=[pl.BlockSpec((tm, d), lambda j, i: (i, 0)),
                  pl.BlockSpec((None, d, tn), gate_col), pl.BlockSpec((None, d, tn), val_col),
                  pl.BlockSpec((None, CONV_W, tn), gate_col), pl.BlockSpec((None, CONV_W, tn), val_col),
                  pl.BlockSpec((None, 1, tn), gate_col), pl.BlockSpec((None, 1, tn), val_col)],
        out_specs=pl.BlockSpec((tm, tn), lambda j, i: (i, j)),
        out_shape=jax.ShapeDtypeStruct((m, f), BF16),
        scratch_shapes=[pltpu.VMEM((HALO, tn), F32), pltpu.VMEM((HALO, tn), F32),
                        pltpu.VMEM((d, tn), BF16), pltpu.VMEM((d, tn), BF16)],
        compiler_params=_params("arbitrary", "arbitrary"),
        name="ffn_up",
    )(h, w_up, w_up, conv_w, conv_w, conv_b, conv_b)


def _ffn_down_kernel(a_ref, w_ref, x_ref, pg_ref, ng_ref, xo_ref, *ho_ref):
    z = jnp.dot(a_ref[...], w_ref[...], preferred_element_type=F32)
    xn = x_ref[...] + _rms(z, pg_ref[...])
    xo_ref[...] = xn
    if ho_ref:
        ho_ref[0][...] = _rms(xn, ng_ref[...]).astype(ho_ref[0].dtype)


def _ffn_down(a, w_down, layer, x2, post_g, next_g):
    m, d = x2.shape
    f = a.shape[1]
    tm = _pick(m, (256, 128))
    emit_next = next_g is not None
    row = lambda i: (i, 0)
    const = lambda i: (0, 0)
    out_specs = [pl.BlockSpec((tm, d), row)]
    out_shape = [jax.ShapeDtypeStruct((m, d), F32)]
    if emit_next:
        out_specs.append(pl.BlockSpec((tm, d), row))
        out_shape.append(jax.ShapeDtypeStruct((m, d), BF16))
    ng = next_g if emit_next else post_g
    res = pl.pallas_call(
        _ffn_down_kernel,
        grid=(m // tm,),
        in_specs=[pl.BlockSpec((tm, f), row),
                  pl.BlockSpec((None, f, d), lambda i: (layer, 0, 0), pipeline_mode=pl.Buffered(1)),
                  pl.BlockSpec((tm, d), row), pl.BlockSpec((1, d), const), pl.BlockSpec((1, d), const)],
        out_specs=out_specs,
        out_shape=out_shape,
        compiler_params=_params("arbitrary"),
        name="ffn_down",
    )(a, w_down, x2, post_g.reshape(1, d), ng.reshape(1, d))
    return (res[0], res[1]) if emit_next else (res[0], None)


@jax.jit
def _trunk(x, rel_bias_table, w_in, b_gate, lam_q1, lam_k1, lam_q2, lam_k2, diff_subln_g, sinks, w_oa, w_ob,
           w_oc, w_out, pre_mix_g, post_mix_g, pre_ffn_g, post_ffn_g, w_up, conv_w, conv_b, w_down):
    batch, seq, d = x.shape
    depth = w_in.shape[0]
    assert seq % ATT_T == 0 and seq % MB_BLOCK == 0 and d % LANES == 0
    assert w_in.shape[2] == QKV_W + N_BRANCH * d
    m = batch * seq

    tab_flat = rel_bias_table.astype(F32).reshape(-1)
    tiles, win = _bias_tiles(tab_flat)
    sw_order = jnp.array(_SW_ORDER)
    w_oa, w_ob, w_oc, w_out, w_down = (w.astype(BF16) for w in (w_oa, w_ob, w_oc, w_out, w_down))

    x2 = x.reshape(m, d)
    h = _prenorm(x2, pre_mix_g[0])
    for l in range(depth):
        qkv = _in_proj(h, w_in, l)
        gates = _in_gates(h, w_in, b_gate, l)

        lam_init = 0.8 - 0.6 * math.exp(-0.3 * l)
        lam_vecs = jnp.stack([lam_q1[l], lam_k1[l], lam_q2[l], lam_k2[l]]).astype(F32)
        ya, yc = _dense_attention(lam_vecs, diff_subln_g[l], qkv, tiles, batch, seq, lam_init)
        sink_row = jnp.repeat(sinks[l].astype(F32)[sw_order] * LOG2E, SW_T).reshape(1, SW_HEADS * SW_T)
        yb = _swa_attention(sink_row, qkv, win, batch, seq)

        x2, h = _mix_out(ya, yb, yc, gates, x2, w_oa, w_ob, w_oc, w_out, l, post_mix_g[l], pre_ffn_g[l])

        a = _ffn_up(h, w_up, conv_w, conv_b, l, seq)
        next_g = pre_mix_g[l + 1] if l + 1 < depth else None
        x2, h = _ffn_down(a, w_down, l, x2, post_ffn_g[l], next_g)
    return x2.reshape(batch, seq, d)


def kernel(x, rel_bias_table, w_in, b_gate, lam_q1, lam_k1, lam_q2, lam_k2, diff_subln_g, sinks, w_oa, w_ob, w_oc, w_out, pre_mix_g, post_mix_g, pre_ffn_g, post_ffn_g, w_up, conv_w, conv_b, w_down):
    return _trunk(x, rel_bias_table, w_in, b_gate, lam_q1, lam_k1, lam_q2, lam_k2, diff_subln_g, sinks, w_oa, w_ob,
                  w_oc, w_out, pre_mix_g, post_mix_g, pre_ffn_g, post_ffn_g, w_up, conv_w, conv_b, w_down)
```

```python
import functools
import math

import numpy as np
import jax
import jax.numpy as jnp
from jax import lax
from jax.experimental import pallas as pl
from jax.experimental.pallas import tpu as pltpu

DA_HEADS = 4
DA_HD = 64
DA_W = DA_HEADS * 2 * DA_HD
SW_HEADS = 8
SW_KV = 2
SW_HD = 64
WINDOW = 128
SW_QW = SW_HEADS * SW_HD
SW_KW = SW_KV * SW_HD
MB_HEADS = 8
MB_HD = 64
MB_W = MB_HEADS * MB_HD
MB_BLOCK = 256
MB_TOPK = 3
N_BUCKETS = 32
MAX_DIST = 128
N_ATT_HEADS = DA_HEADS + SW_HEADS + MB_HEADS
N_BRANCH = 3
QKV_W = 3 * DA_W + SW_QW + 2 * SW_KW + 3 * MB_W
EPS = 1e-6

LANES = 128
MXU_N = 256
SUBLANES = 8
BF16_ROWS = 16
ATT_T = 256
SW_T = WINDOW
NEG = -1e30
LOG2E = math.log2(math.e)
Q_SCALE = DA_HD ** -0.5 * LOG2E
VMEM_LIMIT = 56 * 1024 * 1024

_QA, _KA, _VA = 0, DA_W // LANES, 2 * DA_W // LANES
_QC = 3 * DA_W // LANES
_KC = _QC + MB_W // LANES
_VC = _KC + MB_W // LANES
_QB = 3 * DA_W + 3 * MB_W
_KB = (_QB + SW_QW) // LANES
_VB = _KB + 1
_SW_DIRECT = tuple(h for h in range(SW_HEADS) if h % 2 == h // (SW_HEADS // SW_KV))
_SW_SWAPPED = tuple(h for h in range(SW_HEADS) if h % 2 != h // (SW_HEADS // SW_KV))
_SW_ORDER = _SW_DIRECT + _SW_SWAPPED

F32 = jnp.float32
BF16 = jnp.bfloat16


def _pick(n, candidates):
    for c in candidates:
        if n % c == 0:
            return c
    raise ValueError(f"no tile in {candidates} divides {n}")


def _params(*sem):
    return pltpu.CompilerParams(dimension_semantics=sem, vmem_limit_bytes=VMEM_LIMIT)


def _rms(x, g):
    return x * lax.rsqrt(jnp.mean(x * x, axis=-1, keepdims=True) + EPS) * g


def _dot_t(a, b, **kw):
    return lax.dot_general(a, b, (((1,), (1,)), ((), ())), preferred_element_type=F32, **kw)


def _rel_bucket(dist):
    n = jnp.maximum(dist, 0)
    max_exact = N_BUCKETS // 2
    nf = jnp.maximum(n, 1).astype(F32)
    large = max_exact + (jnp.log(nf / max_exact) / math.log(MAX_DIST / max_exact)
                         * (N_BUCKETS - max_exact)).astype(jnp.int32)
    large = jnp.minimum(large, N_BUCKETS - 1)
    return jnp.where(n < max_exact, n, large)


def _bias_lookup(tab_ref, head, dist):
    bucket = _rel_bucket(dist)
    acc = jnp.zeros(dist.shape, F32)
    for b in range(N_BUCKETS):
        acc = jnp.where(bucket == b, tab_ref[b * N_ATT_HEADS + head], acc)
    return acc * LOG2E


FAR, NEAR, DIAG = 0, 1, 2


def _tile_kind(kj, qi):
    return jnp.clip(kj - qi + DIAG, FAR, DIAG)


def _bias_kernel(tab_ref, o_ref):
    e = pl.program_id(0)
    moba0 = DA_HEADS + SW_HEADS + 2 * (e - DA_HEADS)
    heads = (jnp.where(e < DA_HEADS, e, moba0), jnp.where(e < DA_HEADS, e, moba0 + 1))
    d = (lax.broadcasted_iota(jnp.int32, (ATT_T, ATT_T), 1)
         - lax.broadcasted_iota(jnp.int32, (ATT_T, ATT_T), 0))
    half = ATT_T // 2
    for c, head in enumerate(heads):
        cols = slice(c * ATT_T, (c + 1) * ATT_T)
        last = tab_ref[(N_BUCKETS - 1) * N_ATT_HEADS + head] * LOG2E
        o_ref[0, FAR, :, cols] = jnp.full((ATT_T, ATT_T), last, F32)
        o_ref[0, NEAR, 0:half, cols] = jnp.full((half, ATT_T), last, F32)
        o_ref[0, NEAR, half:ATT_T, cols] = _bias_lookup(tab_ref, head, d[half:] + ATT_T)
        o_ref[0, DIAG, :, cols] = jnp.where(d >= 0, _bias_lookup(tab_ref, head, d), NEG)


def _window_kernel(tab_ref, win_ref):
    c = pl.program_id(0)
    head = jnp.int32(DA_HEADS + _SW_ORDER[-1])
    for pos, h in enumerate(_SW_ORDER[:-1]):
        head = jnp.where(c == pos, DA_HEADS + h, head)
    d = (lax.broadcasted_iota(jnp.int32, (2 * SW_T, SW_T), 1) + SW_T
         - lax.broadcasted_iota(jnp.int32, (2 * SW_T, SW_T), 0))
    win_ref[...] = jnp.where((d >= 0) & (d < WINDOW), _bias_lookup(tab_ref, head, d), NEG)


def _bias_tiles(tab_flat):
    assert ATT_T // 2 + 1 >= MAX_DIST
    n = DA_HEADS + MB_HEADS // 2
    tiles = pl.pallas_call(
        _bias_kernel,
        grid=(n,),
        in_specs=[pl.BlockSpec(memory_space=pltpu.SMEM)],
        out_specs=pl.BlockSpec((1, 3, ATT_T, 2 * ATT_T), lambda e: (e, 0, 0, 0)),
        out_shape=jax.ShapeDtypeStruct((n, 3, ATT_T, 2 * ATT_T), F32),
        compiler_params=_params("arbitrary"),
        name="bias_tiles",
    )(tab_flat)
    win = pl.pallas_call(
        _window_kernel,
        grid=(SW_HEADS,),
        in_specs=[pl.BlockSpec(memory_space=pltpu.SMEM)],
        out_specs=pl.BlockSpec((2 * SW_T, SW_T), lambda c: (0, c)),
        out_shape=jax.ShapeDtypeStruct((2 * SW_T, SW_HEADS * SW_T), F32),
        compiler_params=_params("arbitrary"),
        name="window_tiles",
    )(tab_flat)
    return tiles, win


def _norm_kernel(x_ref, g_ref, o_ref):
    o_ref[...] = _rms(x_ref[...], g_ref[...]).astype(o_ref.dtype)


def _prenorm(x2, g):
    m, d = x2.shape
    tm = _pick(m, (512, 256, 128))
    return pl.pallas_call(
        _norm_kernel,
        grid=(m // tm,),
        in_specs=[pl.BlockSpec((tm, d), lambda i: (i, 0)), pl.BlockSpec((1, d), lambda i: (0, 0))],
        out_specs=pl.BlockSpec((tm, d), lambda i: (i, 0)),
        out_shape=jax.ShapeDtypeStruct((m, d), BF16),
        compiler_params=_params("arbitrary"),
        name="prenorm",
    )(x2, g.reshape(1, d))


def _proj_kernel(h_ref, *refs):
    w_refs, scale_ref, o_ref, wb_s = refs[:-3], refs[-3], refs[-2], refs[-1]

    @pl.when(pl.program_id(1) == 0)
    def _():
        for c, w_ref in enumerate(w_refs):
            cols = slice(c * PROJ_BLOCK, (c + 1) * PROJ_BLOCK)
            wb_s[:, cols] = (w_ref[...] * scale_ref[:, cols]).astype(BF16)

    o_ref[...] = jnp.dot(h_ref[...], wb_s[...], preferred_element_type=F32).astype(o_ref.dtype)


PROJ_BLOCK = 256
PROJ_BLOCKS_PER_STEP = 5


def _proj_source_block(jb):
    n_a, n_b, n_c = (3 * DA_W // PROJ_BLOCK, (SW_QW + 2 * SW_KW) // PROJ_BLOCK, 3 * MB_W // PROJ_BLOCK)
    return jnp.where(jb < n_a, jb, jnp.where(jb < n_a + n_c, jb + n_b, jb - n_c))


def _in_proj(h, w_in, layer):
    m, d = h.shape
    tm = _pick(m, (1024, 512, 256, 128))
    tn = PROJ_BLOCKS_PER_STEP * PROJ_BLOCK
    assert QKV_W % tn == 0 and (SW_QW + 2 * SW_KW) % PROJ_BLOCK == 0 and DA_W % PROJ_BLOCK == 0
    scale = np.ones((1, QKV_W), np.float32)
    for q0 in (_QA * LANES, _QC * LANES, _QB):
        scale[:, q0:q0 + DA_W] = Q_SCALE
    w_specs = [pl.BlockSpec((None, d, PROJ_BLOCK),
                            lambda j, i, c=c: (layer, 0, _proj_source_block(j * PROJ_BLOCKS_PER_STEP + c)))
               for c in range(PROJ_BLOCKS_PER_STEP)]
    return pl.pallas_call(
        _proj_kernel,
        grid=(QKV_W // tn, m // tm),
        in_specs=[pl.BlockSpec((tm, d), lambda j, i: (i, 0))] + w_specs + [pl.BlockSpec((1, tn), lambda j, i: (0, j))],
        out_specs=pl.BlockSpec((tm, tn), lambda j, i: (i, j)),
        out_shape=jax.ShapeDtypeStruct((m, QKV_W), BF16),
        scratch_shapes=[pltpu.VMEM((d, tn), BF16)],
        compiler_params=_params("arbitrary", "arbitrary"),
        name="in_proj",
    )(h, *([w_in] * PROJ_BLOCKS_PER_STEP), jnp.asarray(scale))


def _gate_kernel(h_ref, w_ref, b_ref, o_ref, wb_s):
    @pl.when(pl.program_id(1) == 0)
    def _():
        wb_s[...] = w_ref[...].astype(BF16)

    acc = jnp.dot(h_ref[...], wb_s[...], preferred_element_type=F32)
    o_ref[...] = (0.5 * jnp.tanh(0.5 * (acc + b_ref[...])) + 0.5).astype(o_ref.dtype)


def _in_gates(h, w_in, b_gate, layer):
    m, d = h.shape
    n = w_in.shape[2] - QKV_W
    tm = _pick(m, (1024, 512, 256, 128))
    tn = _pick(math.gcd(n, QKV_W), (768, 512, 256, 128))
    col0 = QKV_W // tn
    return pl.pallas_call(
        _gate_kernel,
        grid=(n // tn, m // tm),
        in_specs=[pl.BlockSpec((tm, d), lambda j, i: (i, 0)),
                  pl.BlockSpec((None, d, tn), lambda j, i: (layer, 0, col0 + j)),
                  pl.BlockSpec((None, 1, tn), lambda j, i: (layer, 0, j))],
        out_specs=pl.BlockSpec((tm, tn), lambda j, i: (i, j)),
        out_shape=jax.ShapeDtypeStruct((m, n), BF16),
        scratch_shapes=[pltpu.VMEM((d, tn), BF16)],
        compiler_params=_params("arbitrary", "arbitrary"),
        name="in_gates",
    )(h, w_in, b_gate.reshape(b_gate.shape[0], 1, n))


def _split_heads(q):
    lane = lax.broadcasted_iota(jnp.int32, q.shape, 1)
    zero = jnp.zeros_like(q)
    return jnp.concatenate([jnp.where(lane < LANES // 2, q, zero),
                            jnp.where(lane >= LANES // 2, q, zero)], axis=0)


def _merge_heads_t(o):
    t = o.shape[1] // 2
    feat = lax.broadcasted_iota(jnp.int32, (LANES, t), 0)
    return jnp.where(feat < LANES // 2, o[:, :t], o[:, t:]).T


def _pv_t(v, p):
    return lax.dot_general(v, p, (((0,), (0,)), ((), ())), preferred_element_type=F32)


def _softmax_stats(s, m_tile, shift, m, keep=None):
    m_tile = m_tile + shift
    if keep is not None:
        m_tile = jnp.where(keep, m_tile, NEG)
    m_new = jnp.maximum(m, m_tile)
    alpha = jnp.exp2(m - m_new)
    m_sub = m_new - shift
    p = jnp.exp2(s - (m_sub if keep is None else jnp.where(keep, m_sub, -NEG)))
    return m_new, p.astype(BF16), alpha


VT_ROWS = LANES + BF16_ROWS


def _store_values_t(v_ref, group, vt_s):
    for kj in range(vt_s.shape[0]):
        v = v_ref[kj * ATT_T:(kj + 1) * ATT_T, group * LANES:(group + 1) * LANES]
        vt_s[kj, 0:LANES, :] = v.astype(F32).T.astype(BF16)
        vt_s[kj, LANES:VT_ROWS, :] = jnp.ones((VT_ROWS - LANES, ATT_T), BF16)


def _flash(qi, streams):
    def issue_logits(stream, kj, far):
        qk_fn, bias_ref, (s_s, _, _), _ = stream
        s = qk_fn(kj)
        if far:
            shift = jnp.max(bias_ref[FAR, 0:SUBLANES, :], axis=0, keepdims=True)
        else:
            s = s + bias_ref[_tile_kind(kj, qi)]
            shift = jnp.zeros((1, s.shape[1]), F32)
        s_s[...] = s
        return jnp.max(s, axis=0, keepdims=True), shift

    def accumulate(stream, kj, m_tile, shift, m):
        _, _, (s_s, acc_s, vt_s), keep_fn = stream
        m, p, alpha = _softmax_stats(s_s[...], m_tile, shift, m, None if keep_fn is None else keep_fn(kj))
        return m, alpha * acc_s[...] + jnp.dot(vt_s[kj], p, preferred_element_type=F32)

    carry = []
    for stream in streams:
        acc_s = stream[2][1]
        acc_s[...] = jnp.zeros(acc_s.shape, F32)
        carry.append((jnp.full((1, acc_s.shape[1]), -jnp.inf, F32),) + issue_logits(stream, 0, False))

    def body(far, kj, carry):
        out = []
        for stream, (m, m_tile, shift) in zip(streams, carry):
            m, acc = accumulate(stream, kj, m_tile, shift, m)
            stream[2][1][...] = acc
            out.append((m,) + issue_logits(stream, kj + 1, far))
        return tuple(out)

    n_far = jnp.maximum(qi - DIAG, 0)
    carry = lax.fori_loop(0, n_far, functools.partial(body, True), tuple(carry))
    carry = lax.fori_loop(n_far, qi, functools.partial(body, False), carry)
    outs = []
    for stream, (m, m_tile, shift) in zip(streams, carry):
        _, acc = accumulate(stream, qi, m_tile, shift, m)
        outs.append(acc[0:LANES] / acc[LANES:LANES + 1])
    return outs


N_STREAMS = 4


SCRATCH_PER_STREAM = 3


def _flash_scratch(n_queries, n_tiles):
    return N_STREAMS * [pltpu.VMEM((ATT_T, n_queries), F32), pltpu.VMEM((VT_ROWS, n_queries), F32),
                        pltpu.VMEM((n_tiles, VT_ROWS, ATT_T), BF16)]


def _kv_block(ref, kj, group):
    return ref[pl.ds(pl.multiple_of(kj * ATT_T, ATT_T), ATT_T), group * LANES:(group + 1) * LANES]


def _attention_kernel(lam_ref, g_ref, qa_ref, ka_ref, va_ref, qc_ref, kc_ref, vc_ref, bias_ref,
                      sink_ref, qb_ref, kbp_ref, kbo_ref, vbp_ref, vbo_ref, win_ref,
                      oa_ref, ob_ref, oc_ref, kmean_s, *scratch, lam_init, n_blocks):
    qi = pl.program_id(1)
    t = ATT_T
    stride = kmean_s.shape[0] // 3
    scratch = [scratch[SCRATCH_PER_STREAM * g:SCRATCH_PER_STREAM * (g + 1)] for g in range(2 * N_STREAMS)]
    scratch_a, scratch_c = scratch[:N_STREAMS], scratch[N_STREAMS:]

    @pl.when(qi == 0)
    def _():
        for g in range(N_STREAMS):
            _store_values_t(va_ref, g, scratch_a[g][2])
            _store_values_t(vc_ref, g, scratch_c[g][2])
        kf = kc_ref[...].astype(F32).reshape(n_blocks, MB_BLOCK, N_STREAMS * LANES)
        rest = jnp.sum(kf, axis=1) * (1.0 / MB_BLOCK)
        kmean_s[...] = jnp.zeros(kmean_s.shape, BF16)
        for part in range(3):
            term = rest.astype(BF16)
            kmean_s[part * stride:part * stride + n_blocks, :] = term
            rest = rest - term.astype(F32)

    def diff_stream(g):
        q2 = _split_heads(qa_ref[:, g * LANES:(g + 1) * LANES])

        def qk(kj):
            return _dot_t(_kv_block(ka_ref, kj, g), q2)

        return qk, bias_ref.at[g], scratch_a[g], None

    def moba_stream(g):
        q2 = _split_heads(qc_ref[:, g * LANES:(g + 1) * LANES])
        parts = _dot_t(kmean_s[:, g * LANES:(g + 1) * LANES], q2)
        gate = parts[0:n_blocks] + parts[stride:stride + n_blocks] + parts[2 * stride:2 * stride + n_blocks]
        blk = lax.broadcasted_iota(jnp.int32, gate.shape, 0)

        def selected(kj):
            g_kj = jnp.sum(jnp.where(blk == kj, gate, 0.0), axis=0, keepdims=True)
            beats = ((gate > g_kj) | ((gate == g_kj) & (blk < kj))) & (blk < qi)
            rank = jnp.sum(beats.astype(F32), axis=0, keepdims=True)
            return (rank < MB_TOPK) | (kj >= qi)

        def qk(kj):
            return _dot_t(_kv_block(kc_ref, kj, g), q2)

        return qk, bias_ref.at[N_STREAMS + g], scratch_c[g], selected

    streams = []
    for g in range(N_STREAMS):
        streams += [diff_stream(g), moba_stream(g)]
    _swa_tiles(qi, sink_ref, qb_ref, kbp_ref, kbo_ref, vbp_ref, vbo_ref, win_ref, ob_ref)
    outs = _flash(qi, streams)
    lv = lam_ref[...]
    lam = (jnp.exp(jnp.sum(lv[0:1] * lv[1:2], axis=1, keepdims=True))
           - jnp.exp(jnp.sum(lv[2:3] * lv[3:4], axis=1, keepdims=True)) + lam_init)
    for g in range(N_STREAMS):
        o = outs[2 * g]
        o = o[:, :t] - lam * o[:, t:]
        o = o * lax.rsqrt(jnp.mean(o * o, axis=0, keepdims=True) + EPS) * (g_ref[...] * (1.0 - lam_init))
        oa_ref[:, g * LANES:(g + 1) * LANES] = o.T.astype(oa_ref.dtype)
        oc_ref[:, g * LANES:(g + 1) * LANES] = _merge_heads_t(outs[2 * g + 1]).astype(oc_ref.dtype)


def _attention(lam_vecs, subln_g, sink_row, qkv, tiles, win, batch, seq, lam_init):
    assert DA_HEADS == N_STREAMS and MB_HEADS == 2 * N_STREAMS and DA_W == SW_QW == MB_W
    m = batch * seq
    nq = seq // ATT_T
    n_blocks = seq // MB_BLOCK
    w = N_STREAMS * LANES
    sw_tiles = ATT_T // SW_T
    q_spec = lambda col: pl.BlockSpec((ATT_T, w), lambda b, i: (b * nq + i, col // N_STREAMS))
    kv_spec = lambda col: pl.BlockSpec((seq, w), lambda b, i: (b, col // N_STREAMS))
    own_spec = lambda col: pl.BlockSpec((ATT_T, LANES), lambda b, i: (b * nq + i, col))
    prev_spec = lambda col: pl.BlockSpec(
        (SW_T, LANES), lambda b, i: (jnp.maximum((b * nq + i) * sw_tiles - 1, 0), col))
    const = lambda b, i: (0, 0)
    out_spec = pl.BlockSpec((ATT_T, w), lambda b, i: (b * nq + i, 0))
    out_shape = jax.ShapeDtypeStruct((m, w), BF16)
    return pl.pallas_call(
        functools.partial(_attention_kernel, lam_init=lam_init, n_blocks=n_blocks),
        grid=(batch, nq),
        in_specs=[pl.BlockSpec((4, DA_HD), const),
                  pl.BlockSpec((2 * DA_HD, 1), const),
                  q_spec(_QA), kv_spec(_KA), kv_spec(_VA), q_spec(_QC), kv_spec(_KC), kv_spec(_VC),
                  pl.BlockSpec((2 * N_STREAMS, 3, ATT_T, 2 * ATT_T), lambda b, i: (0, 0, 0, 0),
                               pipeline_mode=pl.Buffered(1)),
                  pl.BlockSpec((1, SW_HEADS * SW_T), const),
                  q_spec(_QB // LANES), prev_spec(_KB), own_spec(_KB), prev_spec(_VB), own_spec(_VB),
                  pl.BlockSpec((2 * SW_T, SW_HEADS * SW_T), const)],
        out_specs=[out_spec, out_spec, out_spec],
        out_shape=[out_shape, out_shape, out_shape],
        scratch_shapes=[pltpu.VMEM((3 * pl.cdiv(n_blocks, BF16_ROWS) * BF16_ROWS, w), BF16)]
        + 2 * _flash_scratch(2 * ATT_T, nq),
        compiler_params=_params("arbitrary", "arbitrary"),
        name="attention",
    )(lam_vecs, subln_g.astype(F32).reshape(2 * DA_HD, 1), qkv, qkv, qkv, qkv, qkv, qkv, tiles,
      sink_row, qkv, qkv, qkv, qkv, qkv, win)


def _swap_halves(x):
    return pltpu.roll(x.astype(F32), LANES // 2, 1).astype(x.dtype)


def _swa_tiles(qi, sink_ref, q_ref, kp_ref, ko_ref, vp_ref, vo_ref, bias_ref, o_ref):
    t = SW_T
    half = LANES // 2
    lane = lax.broadcasted_iota(jnp.int32, (t, LANES), 1)
    feat = lax.broadcasted_iota(jnp.int32, (LANES, t), 0)
    first_of_seq = jnp.where(qi > 0, 0.0, NEG)
    for j in range(q_ref.shape[0] // t):
        rows = slice(j * t, (j + 1) * t)
        before = slice((j - 1) * t, j * t)
        out = {}
        for s, heads in enumerate((_SW_DIRECT, _SW_SWAPPED)):
            kp, vp = (kp_ref[...], vp_ref[...]) if j == 0 else (ko_ref[before, :], vo_ref[before, :])
            ko, vo = ko_ref[rows, :], vo_ref[rows, :]
            if s == 1:
                kp, ko, vp, vo = (_swap_halves(x) for x in (kp, ko, vp, vo))
            qs = []
            for h in heads:
                q = q_ref[rows, (h // 2) * LANES:(h // 2 + 1) * LANES]
                qs.append(jnp.where(lane >= half if h % 2 else lane < half, q, jnp.zeros_like(q)))
            qs = jnp.concatenate(qs, axis=0)
            cols = slice(s * len(heads) * t, (s + 1) * len(heads) * t)
            s_prev = _dot_t(kp, qs) + bias_ref[0:t, cols]
            if j == 0:
                s_prev = s_prev + first_of_seq
            s_own = _dot_t(ko, qs) + bias_ref[t:2 * t, cols]
            sink = sink_ref[:, cols]
            m = jnp.maximum(jnp.maximum(jnp.max(s_prev, axis=0, keepdims=True),
                                        jnp.max(s_own, axis=0, keepdims=True)), sink)
            p_prev = jnp.exp2(s_prev - m)
            p_own = jnp.exp2(s_own - m)
            l = (jnp.sum(p_prev, axis=0, keepdims=True) + jnp.sum(p_own, axis=0, keepdims=True)
                 + jnp.exp2(sink - m))
            o = (_pv_t(vp, p_prev.astype(BF16)) + _pv_t(vo, p_own.astype(BF16))) / l
            for c, h in enumerate(heads):
                out[h] = o[:, c * t:(c + 1) * t]
        for g in range(SW_QW // LANES):
            o_ref[rows, g * LANES:(g + 1) * LANES] = jnp.where(feat < half, out[2 * g], out[2 * g + 1]).T.astype(o_ref.dtype)


def _mix_out_kernel(ya_ref, yb_ref, yc_ref, ga_ref, gb_ref, gc_ref, x_ref, woa_ref, wob_ref, woc_ref,
                    wout_ref, pg_ref, ng_ref, xo_ref, ho_ref):
    def branch(y_ref, w_ref, g_ref, rows):
        return g_ref[rows, :].astype(F32) * jnp.dot(y_ref[rows, :], w_ref[...], preferred_element_type=F32)

    tm = x_ref.shape[0]
    sub = MXU_N if tm % MXU_N == 0 else tm
    for r0 in range(0, tm, sub):
        rows = slice(r0, r0 + sub)
        mix = (branch(ya_ref, woa_ref, ga_ref, rows) + branch(yb_ref, wob_ref, gb_ref, rows)
               + branch(yc_ref, woc_ref, gc_ref, rows))
        z = jnp.dot(mix.astype(BF16), wout_ref[...], preferred_element_type=F32)
        xn = x_ref[rows, :] + _rms(z, pg_ref[...])
        xo_ref[rows, :] = xn
        ho_ref[rows, :] = _rms(xn, ng_ref[...]).astype(ho_ref.dtype)


def _mix_out(ya, yb, yc, gates, x2, woa, wob, woc, wout, layer, post_g, next_g):
    m, d = x2.shape
    tm = _pick(m, (512, 256, 128))
    row = lambda i: (i, 0)
    const = lambda i: (0, 0)
    weight = lambda rows: pl.BlockSpec((None, rows, d), lambda i: (layer, 0, 0), pipeline_mode=pl.Buffered(1))
    return pl.pallas_call(
        _mix_out_kernel,
        grid=(m // tm,),
        in_specs=[pl.BlockSpec((tm, DA_W), row), pl.BlockSpec((tm, SW_QW), row), pl.BlockSpec((tm, MB_W), row),
                  pl.BlockSpec((tm, d), lambda i: (i, 0)), pl.BlockSpec((tm, d), lambda i: (i, 1)),
                  pl.BlockSpec((tm, d), lambda i: (i, 2)),
                  pl.BlockSpec((tm, d), row),
                  weight(DA_W), weight(SW_QW), weight(MB_W), weight(d),
                  pl.BlockSpec((1, d), const), pl.BlockSpec((1, d), const)],
        out_specs=[pl.BlockSpec((tm, d), row), pl.BlockSpec((tm, d), row)],
        out_shape=[jax.ShapeDtypeStruct((m, d), F32), jax.ShapeDtypeStruct((m, d), BF16)],
        compiler_params=_params("arbitrary"),
        name="mix_out",
    )(ya, yb, yc, gates, gates, gates, x2, woa, wob, woc, wout, post_g.reshape(1, d), next_g.reshape(1, d))


CONV_W = 3
HALO = 8


def _ffn_up_kernel(h_ref, wgf_ref, wvf_ref, cwg_ref, cwv_ref, cbg_ref, cbv_ref, o_ref, halo_g, halo_v, wg_ref, wv_ref,
                   *, tiles_per_seq):
    i = pl.program_id(1)
    tm = h_ref.shape[0]

    @pl.when(i == 0)
    def _():
        wg_ref[...] = wgf_ref[...].astype(BF16)
        wv_ref[...] = wvf_ref[...].astype(BF16)

    @pl.when(i % tiles_per_seq == 0)
    def _():
        halo_g[...] = jnp.zeros(halo_g.shape, F32)
        halo_v[...] = jnp.zeros(halo_v.shape, F32)

    h = h_ref[...]

    def conv(w_ref, cw_ref, cb_ref, halo_s):
        u = jnp.dot(h, w_ref[...], preferred_element_type=F32)
        cw, cb = cw_ref[...], cb_ref[...]

        def taps(u2, u1, u0):
            return cb + u2 * cw[0:1] + u1 * cw[1:2] + u0 * cw[2:3]

        head = jnp.concatenate([halo_s[...], u[0:HALO]], axis=0)
        first = taps(head[HALO - 2:2 * HALO - 2], head[HALO - 1:2 * HALO - 1], head[HALO:])
        body = taps(pltpu.roll(u, 2, 0), pltpu.roll(u, 1, 0), u)
        halo_s[...] = u[tm - HALO:tm]
        return jnp.concatenate([first, body[HALO:]], axis=0)

    gate = conv(wg_ref, cwg_ref, cbg_ref, halo_g)
    val = conv(wv_ref, cwv_ref, cbv_ref, halo_v)
    o_ref[...] = (jax.nn.gelu(gate, approximate=True) * val).astype(o_ref.dtype)


def _ffn_up(h, w_up, conv_w, conv_b, layer, seq):
    m, d = h.shape
    f = w_up.shape[2] // 2
    tm = _pick(seq, (1024, 512, 256, 128))
    tn = _pick(f, (512, 384, 256, 128))
    nj = f // tn
    gate_col = lambda j, i: (layer, 0, j)
    val_col = lambda j, i: (layer, 0, nj + j)
    conv_b = conv_b.reshape(conv_b.shape[0], 1, 2 * f)
    return pl.pallas_call(
        functools.partial(_ffn_up_kernel, tiles_per_seq=seq // tm),
        grid=(nj, m // tm),
        in_specs=[pl.BlockSpec((tm, d), lambda j, i: (i, 0)),
                  pl.BlockSpec((None, d, tn), gate_col), pl.BlockSpec((None, d, tn), val_col),
                  pl.BlockSpec((None, CONV_W, tn), gate_col), pl.BlockSpec((None, CONV_W, tn), val_col),
                  pl.BlockSpec((None, 1, tn), gate_col), pl.BlockSpec((None, 1, tn), val_col)],
        out_specs=pl.BlockSpec((tm, tn), lambda j, i: (i, j)),
        out_shape=jax.ShapeDtypeStruct((m, f), BF16),
        scratch_shapes=[pltpu.VMEM((HALO, tn), F32), pltpu.VMEM((HALO, tn), F32),
                        pltpu.VMEM((d, tn), BF16), pltpu.VMEM((d, tn), BF16)],
        compiler_params=_params("arbitrary", "arbitrary"),
        name="ffn_up",
    )(h, w_up, w_up, conv_w, conv_w, conv_b, conv_b)


def _ffn_down_kernel(a_ref, w_ref, x_ref, pg_ref, ng_ref, xo_ref, *ho_ref):
    z = jnp.dot(a_ref[...], w_ref[...], preferred_element_type=F32)
    xn = x_ref[...] + _rms(z, pg_ref[...])
    xo_ref[...] = xn
    if ho_ref:
        ho_ref[0][...] = _rms(xn, ng_ref[...]).astype(ho_ref[0].dtype)


def _ffn_down(a, w_down, layer, x2, post_g, next_g):
    m, d = x2.shape
    f = a.shape[1]
    tm = _pick(m, (256, 128))
    emit_next = next_g is not None
    row = lambda i: (i, 0)
    const = lambda i: (0, 0)
    out_specs = [pl.BlockSpec((tm, d), row)]
    out_shape = [jax.ShapeDtypeStruct((m, d), F32)]
    if emit_next:
        out_specs.append(pl.BlockSpec((tm, d), row))
        out_shape.append(jax.ShapeDtypeStruct((m, d), BF16))
    ng = next_g if emit_next else post_g
    res = pl.pallas_call(
        _ffn_down_kernel,
        grid=(m // tm,),
        in_specs=[pl.BlockSpec((tm, f), row),
                  pl.BlockSpec((None, f, d), lambda i: (layer, 0, 0), pipeline_mode=pl.Buffered(1)),
                  pl.BlockSpec((tm, d), row), pl.BlockSpec((1, d), const), pl.BlockSpec((1, d), const)],
        out_specs=out_specs,
        out_shape=out_shape,
        compiler_params=_params("arbitrary"),
        name="ffn_down",
    )(a, w_down, x2, post_g.reshape(1, d), ng.reshape(1, d))
    return (res[0], res[1]) if emit_next else (res[0], None)


@jax.jit
def _trunk(x, rel_bias_table, w_in, b_gate, lam_q1, lam_k1, lam_q2, lam_k2, diff_subln_g, sinks, w_oa, w_ob,
           w_oc, w_out, pre_mix_g, post_mix_g, pre_ffn_g, post_ffn_g, w_up, conv_w, conv_b, w_down):
    batch, seq, d = x.shape
    depth = w_in.shape[0]
    assert seq % ATT_T == 0 and seq % MB_BLOCK == 0 and d % LANES == 0
    assert w_in.shape[2] == QKV_W + N_BRANCH * d
    m = batch * seq

    tab_flat = rel_bias_table.astype(F32).reshape(-1)
    tiles, win = _bias_tiles(tab_flat)
    sw_order = jnp.array(_SW_ORDER)
    w_oa, w_ob, w_oc, w_out, w_down = (w.astype(BF16) for w in (w_oa, w_ob, w_oc, w_out, w_down))

    x2 = x.reshape(m, d)
    h = _prenorm(x2, pre_mix_g[0])
    for l in range(depth):
        qkv = _in_proj(h, w_in, l)
        gates = _in_gates(h, w_in, b_gate, l)

        lam_init = 0.8 - 0.6 * math.exp(-0.3 * l)
        lam_vecs = jnp.stack([lam_q1[l], lam_k1[l], lam_q2[l], lam_k2[l]]).astype(F32)
        sink_row = jnp.repeat(sinks[l].astype(F32)[sw_order] * LOG2E, SW_T).reshape(1, SW_HEADS * SW_T)
        ya, yb, yc = _attention(lam_vecs, diff_subln_g[l], sink_row, qkv, tiles, win, batch, seq, lam_init)

        x2, h = _mix_out(ya, yb, yc, gates, x2, w_oa, w_ob, w_oc, w_out, l, post_mix_g[l], pre_ffn_g[l])

        a = _ffn_up(h, w_up, conv_w, conv_b, l, seq)
        next_g = pre_mix_g[l + 1] if l + 1 < depth else None
        x2, h = _ffn_down(a, w_down, l, x2, post_ffn_g[l], next_g)
    return x2.reshape(batch, seq, d)


def kernel(x, rel_bias_table, w_in, b_gate, lam_q1, lam_k1, lam_q2, lam_k2, diff_subln_g, sinks, w_oa, w_ob, w_oc, w_out, pre_mix_g, post_mix_g, pre_ffn_g, post_ffn_g, w_up, conv_w, conv_b, w_down):
    return _trunk(x, rel_bias_table, w_in, b_gate, lam_q1, lam_k1, lam_q2, lam_k2, diff_subln_g, sinks, w_oa, w_ob,
                  w_oc, w_out, pre_mix_g, post_mix_g, pre_ffn_g, post_ffn_g, w_up, conv_w, conv_b, w_down)
```
